```python
import math
import jax, jax.numpy as jnp
from jax import lax
import numpy as np

D_MODEL = 1024
BATCH = 2
SEQ = 8192
DEPTH = 2

GRID_W = 64
CTX_LEN = 256
HEAD_DIM = 64
BLOCK = 128
EPS = 1e-6
ROPE_THETA = 10000.0
NEG_INF = -1e30

A_HEADS = 4
A_KV_HEADS = 2
WINDOW = 128
B_HEADS = 4
B_Q_RANK = 192
B_KV_RANK = 128
B_NOPE = 64
B_ROPE = 32
B_V = 64
C_HEADS = 4
C_QK = 32
C_V = 64
D_HEADS = 4
D_KV_HEADS = 2
N_EXPERTS = 16
CAPACITY_FACTOR = 2
D_EXPERT = 1024

A_COLS = A_HEADS * HEAD_DIM + 2 * A_KV_HEADS * HEAD_DIM
B_COLS = B_Q_RANK + B_KV_RANK + B_ROPE
C_COLS = 4 * C_HEADS * C_QK + C_HEADS * C_V
D_COLS = D_HEADS * HEAD_DIM + 2 * D_KV_HEADS * HEAD_DIM
IN_COLS = A_COLS + B_COLS + C_COLS + D_COLS
COL_SPLITS = (A_COLS, A_COLS + B_COLS, A_COLS + B_COLS + C_COLS)
MIX_WIDTH = A_HEADS * HEAD_DIM + B_HEADS * B_V + C_HEADS * C_V + D_HEADS * HEAD_DIM

kernel_name = "hybrid_parallel_heads_diffusion_block"


def rmsnorm(x, g):
    xf = x.astype(jnp.float32)
    y = xf * lax.rsqrt(jnp.mean(xf * xf, axis=-1, keepdims=True) + EPS)
    return (y * g.astype(jnp.float32)).astype(x.dtype)


def modulate(x, g, shift, scale):
    return rmsnorm(x, g) * (1 + scale) + shift


def adaln_params(cond, w_mod, b_mod):
    m = jax.nn.silu(cond) @ w_mod + b_mod
    return [t[..., None, :] for t in jnp.split(m, 6, axis=-1)]


def axial_rope_table(n_tokens, rot_dim):
    rows = n_tokens // GRID_W
    row = jnp.repeat(jnp.arange(rows), GRID_W)
    col = jnp.tile(jnp.arange(GRID_W), rows)
    quarter = rot_dim // 4
    inv_freq = ROPE_THETA ** (-jnp.arange(quarter, dtype=jnp.float32) / quarter)
    ang = jnp.concatenate([row[:, None] * inv_freq, col[:, None] * inv_freq], axis=-1)
    return jnp.cos(ang), jnp.sin(ang)


def apply_rope(x, table):
    cos, sin = table
    cos = cos[None, :, None, :].astype(x.dtype)
    sin = sin[None, :, None, :].astype(x.dtype)
    half = x.shape[-1] // 2
    x1, x2 = x[..., :half], x[..., half:]
    return jnp.concatenate([x1 * cos - x2 * sin, x1 * sin + x2 * cos], axis=-1)


def split_heads(t, n):
    return t.reshape(t.shape[0], t.shape[1], n, -1)


def merge_heads(t):
    return t.reshape(t.shape[0], t.shape[1], -1)


def dense_attention(q, k, v, scale, sink=None):
    b, sq, hq, dk = q.shape
    hkv = k.shape[2]
    g = hq // hkv
    nb = sq // BLOCK
    qb = q.reshape(b, nb, BLOCK, hkv, g, dk).transpose(1, 0, 2, 3, 4, 5)

    def one_block(qblk):
        s = jnp.einsum('bqhgd,bkhd->bhgqk', qblk, k).astype(jnp.float32) * scale
        if sink is not None:
            s_sink = jnp.broadcast_to(sink.astype(jnp.float32).reshape(1, hkv, g, 1, 1), s.shape[:-1] + (1,))
            p = jax.nn.softmax(jnp.concatenate([s, s_sink], axis=-1), axis=-1)[..., :-1]
        else:
            p = jax.nn.softmax(s, axis=-1)
        return jnp.einsum('bhgqk,bkhd->bqhgd', p.astype(v.dtype), v)

    o = lax.map(one_block, qb)
    return o.transpose(1, 0, 2, 3, 4, 5).reshape(b, sq, hq, v.shape[-1])


def window_attention_sink(q, k, v, k_ctx, v_ctx, sink, scale):
    b, s_len, hq, d = q.shape
    hkv = k.shape[2]
    g = hq // hkv
    nb = s_len // BLOCK
    pad = ((0, 0), (BLOCK, BLOCK), (0, 0), (0, 0))
    kp = jnp.pad(k, pad).reshape(b, nb + 2, BLOCK, hkv, d)
    vp = jnp.pad(v, pad).reshape(b, nb + 2, BLOCK, hkv, d)
    kb = jnp.concatenate([kp[:, :-2], kp[:, 1:-1], kp[:, 2:]], axis=2)
    vb = jnp.concatenate([vp[:, :-2], vp[:, 1:-1], vp[:, 2:]], axis=2)
    qb = q.reshape(b, nb, BLOCK, hkv, g, d)
    s_band = jnp.einsum('bnqhgd,bnkhd->bnhgqk', qb, kb).astype(jnp.float32) * scale
    qpos = jnp.arange(BLOCK)[:, None]
    kpos = jnp.arange(3 * BLOCK)[None, :] - BLOCK
    abs_k = (jnp.arange(nb) * BLOCK)[:, None, None] + kpos[None]
    valid = (jnp.abs(kpos - qpos) <= WINDOW)[None] & (abs_k >= 0) & (abs_k < s_len)
    s_band = jnp.where(valid[None, :, None, None], s_band, NEG_INF)
    s_ctx = jnp.einsum('bnqhgd,bkhd->bnhgqk', qb, k_ctx).astype(jnp.float32) * scale
    s_sink = jnp.broadcast_to(sink.astype(jnp.float32).reshape(1, 1, hkv, g, 1, 1), s_band.shape[:-1] + (1,))
    n_ctx = k_ctx.shape[1]
    p = jax.nn.softmax(jnp.concatenate([s_ctx, s_band, s_sink], axis=-1), axis=-1).astype(v.dtype)
    o = (jnp.einsum('bnhgqk,bkhd->bnqhgd', p[..., :n_ctx], v_ctx)
         + jnp.einsum('bnhgqk,bnkhd->bnqhgd', p[..., n_ctx:n_ctx + 3 * BLOCK], vb))
    return o.reshape(b, s_len, hq, d)


def mixer_a_qkv(p, rope):
    q, k, v = jnp.split(p, [A_HEADS * HEAD_DIM, (A_HEADS + A_KV_HEADS) * HEAD_DIM], axis=-1)
    q, k, v = split_heads(q, A_HEADS), split_heads(k, A_KV_HEADS), split_heads(v, A_KV_HEADS)
    if rope is not None:
        q, k = apply_rope(q, rope), apply_rope(k, rope)
    return q, k, v


def mixer_b_qkv(p, q_norm, w_uq, kv_norm, w_ukv, rope):
    cq, ckv, k_rope = jnp.split(p, [B_Q_RANK, B_Q_RANK + B_KV_RANK], axis=-1)
    q = split_heads(rmsnorm(cq, q_norm) @ w_uq, B_HEADS)
    kv = split_heads(rmsnorm(ckv, kv_norm) @ w_ukv, B_HEADS)
    q_nope, q_rope = q[..., :B_NOPE], q[..., B_NOPE:]
    k_nope, v = kv[..., :B_NOPE], kv[..., B_NOPE:]
    k_rope = k_rope[:, :, None, :]
    if rope is not None:
        q_rope, k_rope = apply_rope(q_rope, rope), apply_rope(k_rope, rope)
    k = jnp.concatenate([k_nope, jnp.broadcast_to(k_rope, k_nope.shape[:-1] + (B_ROPE,))], axis=-1)
    return jnp.concatenate([q_nope, q_rope], axis=-1), k, v


def mixer_c_qkv(p, rope):
    q, k, v = jnp.split(p, [2 * C_HEADS * C_QK, 4 * C_HEADS * C_QK], axis=-1)
    q, k = split_heads(q, 2 * C_HEADS), split_heads(k, 2 * C_HEADS)
    if rope is not None:
        q, k = apply_rope(q, rope), apply_rope(k, rope)
    v = jnp.repeat(split_heads(v, C_HEADS), 2, axis=2)
    return q, k, v


def diff_combine(o, lam_params, subln, layer):
    b, s_len = o.shape[:2]
    o = o.reshape(b, s_len, C_HEADS, 2, C_V)
    lam_init = 0.8 - 0.6 * math.exp(-0.3 * layer)
    lp = lam_params.astype(jnp.float32)
    lam = jnp.exp(jnp.sum(lp[0] * lp[1])) - jnp.exp(jnp.sum(lp[2] * lp[3])) + lam_init
    out = o[:, :, :, 0] - lam.astype(o.dtype) * o[:, :, :, 1]
    return merge_heads(rmsnorm(out, subln) * (1 - lam_init))


def mixer_d_qkv(p, q_norm, k_norm, rope):
    q, k, v = jnp.split(p, [D_HEADS * HEAD_DIM, (D_HEADS + D_KV_HEADS) * HEAD_DIM], axis=-1)
    q = rmsnorm(split_heads(q, D_HEADS), q_norm)
    k = rmsnorm(split_heads(k, D_KV_HEADS), k_norm)
    v = split_heads(v, D_KV_HEADS)
    if rope is not None:
        q, k = apply_rope(q, rope), apply_rope(k, rope)
    return q, k, v


def token_mixers(hl, hc, layer, with_ctx_out, w_in, a_sink, b_q_norm, b_w_uq, b_kv_norm, b_w_ukv,
                 c_lambda, c_subln, d_q_norm, d_k_norm, w_out, rope64, rope32):
    pa, pb, pc, pd = jnp.split(hl @ w_in, COL_SPLITS, axis=-1)
    ca, cb, cc, cd = jnp.split(hc @ w_in, COL_SPLITS, axis=-1)
    qa, ka, va = mixer_a_qkv(pa, rope64)
    qac, kac, vac = mixer_a_qkv(ca, None)
    qb, kb, vb = mixer_b_qkv(pb, b_q_norm, b_w_uq, b_kv_norm, b_w_ukv, rope32)
    qbc, kbc, vbc = mixer_b_qkv(cb, b_q_norm, b_w_uq, b_kv_norm, b_w_ukv, None)
    qc, kc, vc = mixer_c_qkv(pc, rope32)
    qcc, kcc, vcc = mixer_c_qkv(cc, None)
    qd, kd, vd = mixer_d_qkv(pd, d_q_norm, d_k_norm, rope64)
    qdc, kdc, vdc = mixer_d_qkv(cd, d_q_norm, d_k_norm, None)
    scale64 = HEAD_DIM ** -0.5
    scale_b = (B_NOPE + B_ROPE) ** -0.5
    scale_c = C_QK ** -0.5
    cat = lambda t_ctx, t_lat: jnp.concatenate([t_ctx, t_lat], axis=1)
    oa = window_attention_sink(qa, ka, va, kac, vac, a_sink, scale64)
    ob = dense_attention(qb, cat(kbc, kb), cat(vbc, vb), scale_b)
    oc = diff_combine(dense_attention(qc, cat(kcc, kc), cat(vcc, vc), scale_c), c_lambda, c_subln, layer)
    od = dense_attention(qd, cat(kdc, kd), cat(vdc, vd), scale64)
    out_lat = jnp.concatenate([merge_heads(oa), merge_heads(ob), oc, merge_heads(od)], axis=-1) @ w_out
    if not with_ctx_out:
        return out_lat, None
    oac = dense_attention(qac, kac, vac, scale64, sink=a_sink)
    obc = dense_attention(qbc, kbc, vbc, scale_b)
    occ = diff_combine(dense_attention(qcc, kcc, vcc, scale_c), c_lambda, c_subln, layer)
    odc = dense_attention(qdc, kdc, vdc, scale64)
    out_ctx = jnp.concatenate([merge_heads(oac), merge_heads(obc), occ, merge_heads(odc)], axis=-1) @ w_out
    return out_lat, out_ctx


def expert_choice_ffn(h, w_router, w_gate, w_up, w_down):
    b, n, d = h.shape
    cap = CAPACITY_FACTOR * n // N_EXPERTS
    logits = jnp.einsum('bnd,de->ben', h, w_router).astype(jnp.float32)
    affinity = jax.nn.softmax(logits, axis=1)
    gates, idx = lax.top_k(affinity, cap)
    xs = jax.vmap(lambda hb, ib: hb[ib])(h, idx)
    a = jnp.einsum('becd,edf->becf', xs, w_gate)
    u = jnp.einsum('becd,edf->becf', xs, w_up)
    y = jnp.einsum('becf,efd->becd', jax.nn.silu(a) * u, w_down) * gates[..., None].astype(h.dtype)
    return jax.vmap(lambda yb, ib: jnp.zeros((n, d), yb.dtype).at[ib.reshape(-1)].add(yb.reshape(-1, d)))(y, idx)


def setup_inputs(seed: int = 0) -> dict:
    key = jax.random.key(seed)
    ks = jax.random.split(key, 24)
    D, L = D_MODEL, DEPTH

    def nrm(i, shape, s):
        return jax.random.normal(ks[i], shape, jnp.float32) * s

    def gain(i, shape):
        return 1.0 + nrm(i, shape, 0.02)

    return {
        "x": nrm(0, (BATCH, SEQ, D), 1.0),
        "c": nrm(1, (BATCH, D), 1.0),
        "ctx": nrm(2, (BATCH, CTX_LEN, D), 1.0),
        "c_ctx": nrm(3, (D,), 1.0),
        "w_mod": nrm(4, (L, D, 6 * D), 0.5 * D ** -0.5),
        "b_mod": nrm(5, (L, 6 * D), 0.02),
        "g_attn": gain(6, (L, D)),
        "g_ffn": gain(7, (L, D)),
        "w_in": nrm(8, (L, D, IN_COLS), D ** -0.5),
        "a_sink": nrm(9, (L, A_HEADS), 0.5),
        "b_q_norm": gain(10, (L, B_Q_RANK)),
        "b_w_uq": nrm(11, (L, B_Q_RANK, B_HEADS * (B_NOPE + B_ROPE)), B_Q_RANK ** -0.5),
        "b_kv_norm": gain(12, (L, B_KV_RANK)),
        "b_w_ukv": nrm(13, (L, B_KV_RANK, B_HEADS * (B_NOPE + B_V)), B_KV_RANK ** -0.5),
        "c_lambda": nrm(14, (L, 4, C_QK), 0.1),
        "c_subln": gain(15, (L, C_V)),
        "d_q_norm": gain(16, (L, HEAD_DIM)),
        "d_k_norm": gain(17, (L, HEAD_DIM)),
        "w_out": nrm(18, (L, MIX_WIDTH, D), MIX_WIDTH ** -0.5),
        "w_router": nrm(19, (L, D, N_EXPERTS), D ** -0.5),
        "w_gate": nrm(20, (L, N_EXPERTS, D, D_EXPERT), D ** -0.5),
        "w_up": nrm(21, (L, N_EXPERTS, D, D_EXPERT), D ** -0.5),
        "w_down": nrm(22, (L, N_EXPERTS, D_EXPERT, D), D_EXPERT ** -0.5),
        "g_final": gain(23, (D,)),
    }


def reference(x, c, ctx, c_ctx, w_mod, b_mod, g_attn, g_ffn, w_in, a_sink, b_q_norm, b_w_uq,
              b_kv_norm, b_w_ukv, c_lambda, c_subln, d_q_norm, d_k_norm, w_out, w_router,
              w_gate, w_up, w_down, g_final):
    n_tokens = x.shape[1]
    rope64 = axial_rope_table(n_tokens, HEAD_DIM)
    rope32 = axial_rope_table(n_tokens, C_QK)
    xl, xc = x, ctx
    for l in range(DEPTH):
        last = l == DEPTH - 1
        sh_a, sc_a, gt_a, sh_f, sc_f, gt_f = adaln_params(c, w_mod[l], b_mod[l])
        csh_a, csc_a, cgt_a, csh_f, csc_f, cgt_f = adaln_params(c_ctx, w_mod[l], b_mod[l])
        hl = modulate(xl, g_attn[l], sh_a, sc_a)
        hc = modulate(xc, g_attn[l], csh_a, csc_a)
        out_lat, out_ctx = token_mixers(hl, hc, l, not last, w_in[l], a_sink[l], b_q_norm[l], b_w_uq[l],
                                        b_kv_norm[l], b_w_ukv[l], c_lambda[l], c_subln[l], d_q_norm[l],
                                        d_k_norm[l], w_out[l], rope64, rope32)
        xl = xl + gt_a * out_lat
        hl = modulate(xl, g_ffn[l], sh_f, sc_f)
        xl = xl + gt_f * expert_choice_ffn(hl, w_router[l], w_gate[l], w_up[l], w_down[l])
        if not last:
            xc = xc + cgt_a * out_ctx
            hc = modulate(xc, g_ffn[l], csh_f, csc_f)
            xc = xc + cgt_f * expert_choice_ffn(hc, w_router[l], w_gate[l], w_up[l], w_down[l])
    return rmsnorm(xl, g_final)
```

```python
import functools
import math

import jax
import jax.numpy as jnp
from jax import lax
from jax.experimental import pallas as pl
from jax.experimental.pallas import tpu as pltpu

F32, BF16, I32 = jnp.float32, jnp.bfloat16, jnp.int32
HIGHEST = lax.Precision.HIGHEST

SUBLANES = 8
LANES = 128

GRID_W = 64
HEAD_DIM = 64
WINDOW = 128
EPS = 1e-6
ROPE_THETA = 10000.0
NEG_INF = -1e30
A_HEADS, A_KV = 4, 2
B_HEADS, B_Q_RANK, B_KV_RANK, B_NOPE, B_ROPE, B_V = 4, 192, 128, 64, 32, 64
C_HEADS, C_QK, C_V = 4, 32, 64
D_HEADS, D_KV = 4, 2
N_EXPERTS = 16
CAPACITY_FACTOR = 2

A_COLS = A_HEADS * HEAD_DIM + 2 * A_KV * HEAD_DIM
B_COLS = B_Q_RANK + B_KV_RANK + B_ROPE
C_COLS = 4 * C_HEADS * C_QK + C_HEADS * C_V
D_COLS = D_HEADS * HEAD_DIM + 2 * D_KV * HEAD_DIM
A_OFF, B_OFF, C_OFF, D_OFF = 0, A_COLS, A_COLS + B_COLS, A_COLS + B_COLS + C_COLS

BISECT_STEPS = 152
TOK = 256
QPAD = 128

NT_DIMS = (((1,), (1,)), ((), ()))
TN_DIMS = (((0,), (0,)), ((), ()))


def _params(*sem):
    return pltpu.CompilerParams(dimension_semantics=sem, vmem_limit_bytes=56 * 1024 * 1024)


def _adaln_body(c_ref, w_ref, b_ref, o_ref):
    c = c_ref[...]
    s = c * jax.nn.sigmoid(c)
    o_ref[0] = jnp.dot(s, w_ref[0], precision=HIGHEST, preferred_element_type=F32) + b_ref[0]


def _adaln(cond, w_mod, b_mod):
    depth, d, n = w_mod.shape
    tn = n // 4
    return pl.pallas_call(
        _adaln_body,
        grid=(depth, n // tn),
        in_specs=[pl.BlockSpec((SUBLANES, d), lambda l, j: (0, 0)),
                  pl.BlockSpec((1, d, tn), lambda l, j: (l, 0, j)),
                  pl.BlockSpec((1, 1, tn), lambda l, j: (l, 0, j))],
        out_specs=pl.BlockSpec((1, SUBLANES, tn), lambda l, j: (l, 0, j)),
        out_shape=jax.ShapeDtypeStruct((depth, SUBLANES, n), F32),
        compiler_params=_params("arbitrary", "arbitrary"),
        name="adaln",
    )(cond, w_mod, b_mod.reshape(depth, 1, n))


def _rope_t(xt, cos, sin):
    half = xt.shape[0] // 2
    x1, x2 = xt[:half], xt[half:]
    return jnp.concatenate([x1 * cos - x2 * sin, x1 * sin + x2 * cos], axis=0)


def _rms_t(xt, g):
    ms = jnp.mean(xt * xt, axis=0, keepdims=True)
    return xt * lax.rsqrt(ms + EPS) * g


def _modulate(x, g, shift, scale):
    ms = jnp.mean(x * x, axis=-1, keepdims=True)
    return x * lax.rsqrt(ms + EPS) * g * (1.0 + scale) + shift


def _proj_body(x_ref, g_ref, sh_ref, sc_ref, w_ref, c64_ref, s64_ref, c32_ref, s32_ref,
               bqn_ref, wuq_ref, bkvn_ref, wukv_ref, dqn_ref, dkn_ref,
               qa_ref, ka_ref, va_ref, qb_ref, kb_ref, vb_ref,
               qc_ref, kc_ref, vc_ref, qd_ref, kd_ref, vd_ref):
    tm = x_ref.shape[1]
    h = _modulate(x_ref[0], g_ref[...], sh_ref[0], sc_ref[0]).astype(BF16)
    pt = lax.dot_general(w_ref[...], h, NT_DIMS, preferred_element_type=F32)
    c64, s64, c32, s32 = c64_ref[...], s64_ref[...], c32_ref[...], s32_ref[...]
    z64 = jnp.zeros((64, tm), F32)
    z32 = jnp.zeros((32, tm), F32)
    scale64 = HEAD_DIM ** -0.5
    scale_b = (B_NOPE + B_ROPE) ** -0.5
    scale_c = C_QK ** -0.5

    def gqa(off, q_ref, k_ref, v_ref, qn, kn):
        for hh in range(4):
            q = pt[off + 64 * hh: off + 64 * hh + 64]
            if qn is not None:
                q = _rms_t(q, qn)
            q = _rope_t(q, c64, s64) * scale64
            q_ref[0, hh] = (jnp.concatenate([q, z64], axis=0) if hh // 2 == 0
                            else jnp.concatenate([z64, q], axis=0)).astype(BF16)
        ks = []
        for j in range(2):
            k = pt[off + 256 + 64 * j: off + 256 + 64 * j + 64]
            if kn is not None:
                k = _rms_t(k, kn)
            ks.append(_rope_t(k, c64, s64))
        k_ref[0] = jnp.concatenate(ks, axis=0).T.astype(BF16)
        v_ref[0, 0] = pt[off + 384: off + 512].astype(BF16)

    gqa(A_OFF, qa_ref, ka_ref, va_ref, None, None)
    gqa(D_OFF, qd_ref, kd_ref, vd_ref, dqn_ref[...], dkn_ref[...])

    cq = _rms_t(pt[B_OFF: B_OFF + B_Q_RANK], bqn_ref[...]).astype(BF16)
    ckv = _rms_t(pt[B_OFF + B_Q_RANK: B_OFF + B_Q_RANK + B_KV_RANK], bkvn_ref[...]).astype(BF16)
    kr = _rope_t(pt[B_OFF + B_Q_RANK + B_KV_RANK: B_OFF + B_COLS], c32, s32)
    qt = jnp.dot(wuq_ref[...], cq, preferred_element_type=F32)
    kvt = jnp.dot(wukv_ref[...], ckv, preferred_element_type=F32)
    kparts = []
    dq = B_NOPE + B_ROPE
    for hh in range(B_HEADS):
        qn_ = qt[dq * hh: dq * hh + B_NOPE]
        qr = _rope_t(qt[dq * hh + B_NOPE: dq * hh + dq], c32, s32)
        qb_ref[0, hh] = (jnp.concatenate([qn_, qr, z32], axis=0) * scale_b).astype(BF16)
        kparts.append(jnp.concatenate([kvt[128 * hh: 128 * hh + B_NOPE], kr, z32], axis=0))
        vb_ref[0, 0, 64 * hh: 64 * hh + 64] = kvt[128 * hh + B_NOPE: 128 * hh + 128].astype(BF16)
    kb_ref[0] = jnp.concatenate(kparts, axis=0).T.astype(BF16)

    for j in range(2 * C_HEADS):
        q = _rope_t(pt[C_OFF + 32 * j: C_OFF + 32 * j + 32], c32, s32) * scale_c
        pieces = [z32, z32, z32, z32]
        pieces[j % 4] = q
        qc_ref[0, j] = jnp.concatenate(pieces, axis=0).astype(BF16)
    kc = [_rope_t(pt[C_OFF + 256 + 32 * j: C_OFF + 256 + 32 * j + 32], c32, s32) for j in range(2 * C_HEADS)]
    kc_ref[0] = jnp.concatenate(kc, axis=0).T.astype(BF16)
    vc_ref[0, 0] = pt[C_OFF + 512: C_OFF + 768].astype(BF16)


def _proj(x, g, shift, scale, w_in_t, rope64, rope32, bqn, wuq_t, bkvn, wukv_t, dqn, dkn):
    b, s, d = x.shape
    tm = TOK
    n = s // tm
    full = lambda a: pl.BlockSpec(a.shape, lambda bi, i: (0,) * a.ndim)
    q_spec = lambda nh: pl.BlockSpec((1, nh, QPAD, tm), lambda bi, i: (bi, 0, 0, i))
    k_spec = lambda w: pl.BlockSpec((1, tm, w), lambda bi, i: (bi, i, 0))
    v_spec = lambda r: pl.BlockSpec((1, 1, r, tm), lambda bi, i: (bi, i, 0, 0))
    q_shape = lambda nh: jax.ShapeDtypeStruct((b, nh, QPAD, s), BF16)
    k_shape = lambda w: jax.ShapeDtypeStruct((b, s, w), BF16)
    v_shape = lambda r: jax.ShapeDtypeStruct((b, n, r, tm), BF16)
    tab = lambda t: pl.BlockSpec((t.shape[0], tm), lambda bi, i: (0, i))
    c64, s64 = rope64
    c32, s32 = rope32
    return pl.pallas_call(
        _proj_body,
        grid=(b, n),
        in_specs=[pl.BlockSpec((1, tm, d), lambda bi, i: (bi, i, 0)), full(g),
                  pl.BlockSpec((1, 1, d), lambda bi, i: (bi, 0, 0)),
                  pl.BlockSpec((1, 1, d), lambda bi, i: (bi, 0, 0)),
                  full(w_in_t), tab(c64), tab(s64), tab(c32), tab(s32),
                  full(bqn), full(wuq_t), full(bkvn), full(wukv_t), full(dqn), full(dkn)],
        out_specs=[q_spec(4), k_spec(128), v_spec(128),
                   q_spec(4), k_spec(512), v_spec(256),
                   q_spec(8), k_spec(256), v_spec(256),
                   q_spec(4), k_spec(128), v_spec(128)],
        out_shape=[q_shape(4), k_shape(128), v_shape(128),
                   q_shape(4), k_shape(512), v_shape(256),
                   q_shape(8), k_shape(256), v_shape(256),
                   q_shape(4), k_shape(128), v_shape(128)],
        compiler_params=_params("arbitrary", "arbitrary"),
        name="proj",
    )(x, g, shift, scale, w_in_t, c64, s64, c32, s32, bqn, wuq_t, bkvn, wukv_t, dqn, dkn)


def _softmax_step(kblk, vblk, qt, carry):
    m, l, acc = carry
    s = jnp.dot(kblk, qt, preferred_element_type=F32)
    m_new = jnp.maximum(m, jnp.max(s, axis=0, keepdims=True))
    alpha = jnp.exp(m - m_new)
    p = jnp.exp(s - m_new)
    l = alpha * l + jnp.sum(p, axis=0, keepdims=True)
    acc = alpha * acc + jnp.dot(vblk, p.astype(BF16), preferred_element_type=F32)
    return m_new, l, acc


def _dense_body(*refs, has_lat, has_sink, n_lat):
    refs = list(refs)
    sink_ref = refs.pop(0) if has_sink else None
    q_ref, kc_ref, vc_ref = refs[:3]
    kl_ref, vl_ref = (refs[3], refs[4]) if has_lat else (None, None)
    o_ref = refs[-1]
    tq = q_ref.shape[-1]
    qt = q_ref[0, 0]
    carry = (jnp.full((1, tq), NEG_INF, F32), jnp.zeros((1, tq), F32), jnp.zeros((HEAD_DIM, tq), F32))
    carry = _softmax_step(kc_ref[0], vc_ref[0, 0], qt, carry)
    if has_lat:
        def body(i, c):
            off = pl.multiple_of(i * TOK, TOK)
            return _softmax_step(kl_ref[0, pl.ds(off, TOK), :], vl_ref[0, i], qt, c)
        carry = lax.fori_loop(0, n_lat, body, carry)
    m, l, acc = carry
    if has_sink:
        sk = sink_ref[pl.program_id(1)]
        m2 = jnp.maximum(m, sk)
        a = jnp.exp(m - m2)
        l = l * a + jnp.exp(sk - m2)
        acc = acc * a
    o_ref[0, 0] = acc / l


def _dense_attn(q, k_ctx, v_ctx, k_lat, v_lat, kgroup, vhead, sink=None):
    b, nh, _, s = q.shape
    tq = TOK
    has_lat = k_lat is not None
    has_sink = sink is not None
    sc = k_ctx.shape[1]
    in_specs = [pl.BlockSpec((1, 1, QPAD, tq), lambda bi, h, i: (bi, h, 0, i)),
                pl.BlockSpec((1, sc, LANES), lambda bi, h, i: (bi, 0, kgroup(h))),
                pl.BlockSpec((1, 1, HEAD_DIM, sc), lambda bi, h, i: (bi, 0, vhead(h), 0))]
    args = [q, k_ctx, v_ctx]
    n_lat = 0
    if has_lat:
        sl = k_lat.shape[1]
        n_lat = sl // TOK
        in_specs += [pl.BlockSpec((1, sl, LANES), lambda bi, h, i: (bi, 0, kgroup(h))),
                     pl.BlockSpec((1, n_lat, HEAD_DIM, TOK), lambda bi, h, i: (bi, 0, vhead(h), 0))]
        args += [k_lat, v_lat]
    if has_sink:
        in_specs = [pl.BlockSpec(memory_space=pltpu.SMEM)] + in_specs
        args = [sink] + args
    return pl.pallas_call(
        functools.partial(_dense_body, has_lat=has_lat, has_sink=has_sink, n_lat=n_lat),
        grid=(b, nh, s // tq),
        in_specs=in_specs,
        out_specs=pl.BlockSpec((1, 1, HEAD_DIM, tq), lambda bi, h, i: (bi, h, 0, i)),
        out_shape=jax.ShapeDtypeStruct((b, nh, HEAD_DIM, s), F32),
        compiler_params=_params("arbitrary", "arbitrary", "arbitrary"),
        name="dense_attn",
    )(*args)


def _window_body(sink_ref, q_ref, kc_ref, vc_ref, kp_ref, kq_ref, kn_ref, vp_ref, vq_ref, vn_ref, o_ref):
    i = pl.program_id(1)
    n = pl.num_programs(1)
    tq = q_ref.shape[-1]
    key = lax.broadcasted_iota(I32, (TOK, tq), 0)
    qry = lax.broadcasted_iota(I32, (TOK, tq), 1)
    d = key - qry
    ok_prev = (d >= TOK - WINDOW) & (i > 0)
    ok_cur = jnp.abs(d) <= WINDOW
    ok_next = (d <= WINDOW - TOK) & (i < n - 1)
    for hh in range(A_HEADS):
        qt = q_ref[0, hh]
        rows = slice(HEAD_DIM * (hh // 2), HEAD_DIM * (hh // 2) + HEAD_DIM)
        sk = sink_ref[hh]
        s_c = jnp.dot(kc_ref[0], qt, preferred_element_type=F32)
        s_p = jnp.where(ok_prev, jnp.dot(kp_ref[0], qt, preferred_element_type=F32), NEG_INF)
        s_q = jnp.where(ok_cur, jnp.dot(kq_ref[0], qt, preferred_element_type=F32), NEG_INF)
        s_n = jnp.where(ok_next, jnp.dot(kn_ref[0], qt, preferred_element_type=F32), NEG_INF)
        m = jnp.maximum(jnp.maximum(jnp.max(s_c, axis=0, keepdims=True), jnp.max(s_p, axis=0, keepdims=True)),
                        jnp.maximum(jnp.max(s_q, axis=0, keepdims=True), jnp.max(s_n, axis=0, keepdims=True)))
        m = jnp.maximum(m, sk)
        l = jnp.exp(sk - m)
        acc = jnp.zeros((HEAD_DIM, tq), F32)
        for s_, v_ref in ((s_c, vc_ref), (s_p, vp_ref), (s_q, vq_ref), (s_n, vn_ref)):
            p = jnp.exp(s_ - m)
            l = l + jnp.sum(p, axis=0, keepdims=True)
            acc = acc + jnp.dot(v_ref[0, 0, rows, :], p.astype(BF16), preferred_element_type=F32)
        o_ref[0, hh] = acc / l


def _window_attn(q, k_ctx, v_ctx, k_lat, v_lat, sink):
    b, nh, _, s = q.shape
    n = s // TOK
    sc = k_ctx.shape[1]
    prev = lambda i: jnp.maximum(i - 1, 0)
    nxt = lambda i: jnp.minimum(i + 1, n - 1)
    kspec = lambda f: pl.BlockSpec((1, TOK, LANES), lambda bi, i: (bi, f(i), 0))
    vspec = lambda f: pl.BlockSpec((1, 1, 2 * HEAD_DIM, TOK), lambda bi, i: (bi, f(i), 0, 0))
    same = lambda i: i
    return pl.pallas_call(
        _window_body,
        grid=(b, n),
        in_specs=[pl.BlockSpec(memory_space=pltpu.SMEM),
                  pl.BlockSpec((1, nh, QPAD, TOK), lambda bi, i: (bi, 0, 0, i)),
                  pl.BlockSpec((1, sc, LANES), lambda bi, i: (bi, 0, 0)),
                  pl.BlockSpec((1, 1, 2 * HEAD_DIM, sc), lambda bi, i: (bi, 0, 0, 0)),
                  kspec(prev), kspec(same), kspec(nxt), vspec(prev), vspec(same), vspec(nxt)],
        out_specs=pl.BlockSpec((1, nh, HEAD_DIM, TOK), lambda bi, i: (bi, 0, 0, i)),
        out_shape=jax.ShapeDtypeStruct((b, nh, HEAD_DIM, s), F32),
        compiler_params=_params("arbitrary", "arbitrary"),
        name="window_attn",
    )(sink, q, k_ctx, v_ctx, k_lat, k_lat, k_lat, v_lat, v_lat, v_lat)


def _outproj_body(oa_ref, ob_ref, oc_ref, od_ref, lam_ref, subln_ref, w_ref, x_ref, gt_ref,
                  g_ref, sh_ref, sc_ref, wr_ref, xo_ref, h_ref, lg_ref, *, lam_init):
    lp = lam_ref[...]
    lam = (jnp.exp(jnp.sum(lp[0:1] * lp[1:2], axis=1, keepdims=True))
           - jnp.exp(jnp.sum(lp[2:3] * lp[3:4], axis=1, keepdims=True)) + lam_init)
    parts = [oa_ref[0, hh] for hh in range(4)] + [ob_ref[0, hh] for hh in range(4)]
    for hh in range(C_HEADS):
        o = oc_ref[0, 2 * hh] - lam * oc_ref[0, 2 * hh + 1]
        parts.append(_rms_t(o, subln_ref[...]) * (1.0 - lam_init))
    parts += [od_ref[0, hh] for hh in range(4)]
    ot = jnp.concatenate(parts, axis=0).astype(BF16)
    out = lax.dot_general(ot, w_ref[...], TN_DIMS, preferred_element_type=F32)
    xn = x_ref[0] + gt_ref[0] * out
    xo_ref[0] = xn
    h = _modulate(xn, g_ref[...], sh_ref[0], sc_ref[0])
    h_ref[0] = h.astype(BF16)
    lg_ref[0] = lax.dot_general(wr_ref[...], h, NT_DIMS, precision=HIGHEST, preferred_element_type=F32)


def _outproj(oa, ob, oc, od, lam_p, subln, w_out, x, gate, g_ffn, shift, scale, w_router_t, lam_init):
    b, s, d = x.shape
    tm = TOK
    ospec = lambda nh: pl.BlockSpec((1, nh, HEAD_DIM, tm), lambda bi, i: (bi, 0, 0, i))
    full = lambda a: pl.BlockSpec(a.shape, lambda bi, i: (0,) * a.ndim)
    row = pl.BlockSpec((1, 1, d), lambda bi, i: (bi, 0, 0))
    return pl.pallas_call(
        functools.partial(_outproj_body, lam_init=lam_init),
        grid=(b, s // tm),
        in_specs=[ospec(4), ospec(4), ospec(8), ospec(4), full(lam_p), full(subln), full(w_out),
                  pl.BlockSpec((1, tm, d), lambda bi, i: (bi, i, 0)), row, full(g_ffn), row, row,
                  full(w_router_t)],
        out_specs=[pl.BlockSpec((1, tm, d), lambda bi, i: (bi, i, 0)),
                   pl.BlockSpec((1, tm, d), lambda bi, i: (bi, i, 0)),
                   pl.BlockSpec((1, N_EXPERTS, tm), lambda bi, i: (bi, 0, i))],
        out_shape=[jax.ShapeDtypeStruct((b, s, d), F32), jax.ShapeDtypeStruct((b, s, d), BF16),
                   jax.ShapeDtypeStruct((b, N_EXPERTS, s), F32)],
        compiler_params=_params("arbitrary", "arbitrary"),
        name="outproj",
    )(oa, ob, oc, od, lam_p, subln, w_out, x, gate, g_ffn, shift, scale, w_router_t)


def _router_body(lg_ref, slot_ref, aff_ref, st_ref, *, cap, nblk):
    lg = lg_ref[0]
    ex = jnp.exp(lg - jnp.max(lg, axis=0, keepdims=True))
    aff = ex / jnp.sum(ex, axis=0, keepdims=True)
    aff_ref[0] = aff

    def search(_, bounds):
        lo, hi = bounds
        mid = (lo + hi) * 0.5
        enough = jnp.sum((aff >= mid).astype(I32), axis=1, keepdims=True) >= cap
        return jnp.where(enough, mid, lo), jnp.where(enough, hi, mid)

    lo, hi = lax.fori_loop(0, BISECT_STEPS, search,
                           (jnp.zeros((N_EXPERTS, 1), F32), jnp.full((N_EXPERTS, 1), 2.0, F32)))
    gt = aff >= hi
    eq = (aff >= lo) & jnp.logical_not(gt)
    need = (cap - jnp.sum(gt.astype(I32), axis=1, keepdims=True)).astype(F32)
    tri = (lax.broadcasted_iota(I32, (TOK, TOK), 0) <= lax.broadcasted_iota(I32, (TOK, TOK), 1)).astype(BF16)
    lane = lax.broadcasted_iota(I32, (N_EXPERTS, LANES), 1)
    eq_seen = jnp.zeros((N_EXPERTS, 1), F32)
    base = jnp.zeros((N_EXPERTS, 1), I32)
    starts = jnp.zeros((N_EXPERTS, LANES), I32)
    for j in range(nblk):
        cols = slice(TOK * j, TOK * j + TOK)
        eq_c = eq[:, cols]
        eq_cum = jnp.dot(eq_c.astype(BF16), tri, preferred_element_type=F32) + eq_seen
        eq_seen = eq_seen + jnp.sum(eq_c.astype(F32), axis=1, keepdims=True)
        sel = gt[:, cols] | (eq_c & (eq_cum <= need))
        cum = jnp.dot(sel.astype(BF16), tri, preferred_element_type=F32).astype(I32)
        slot_ref[0, :, cols] = jnp.where(sel, base + cum - 1, -1)
        starts = jnp.where(lane == j, base, starts)
        cnt = jnp.sum(sel.astype(I32), axis=1, keepdims=True)
        base = base + jnp.bitwise_and(cnt + (SUBLANES - 1), -SUBLANES)
    st_ref[0] = jnp.where(lane == nblk, base, starts)


def _router(logits_t, cap):
    b, e, s = logits_t.shape
    nblk = s // TOK
    return pl.pallas_call(
        functools.partial(_router_body, cap=cap, nblk=nblk),
        grid=(b,),
        in_specs=[pl.BlockSpec((1, e, s), lambda bi: (bi, 0, 0))],
        out_specs=[pl.BlockSpec((1, e, s), lambda bi: (bi, 0, 0)),
                   pl.BlockSpec((1, e, s), lambda bi: (bi, 0, 0)),
                   pl.BlockSpec((1, e, LANES), lambda bi: (bi, 0, 0))],
        out_shape=[jax.ShapeDtypeStruct((b, e, s), I32), jax.ShapeDtypeStruct((b, e, s), F32),
                   jax.ShapeDtypeStruct((b, e, LANES), I32)],
        compiler_params=_params("arbitrary"),
        name="router",
    )(logits_t)


def _onehot_t(slot_ref, first, win):
    rows = lax.broadcasted_iota(I32, (win, TOK), 0)
    return [rows == (slot_ref[0, e:e + 1, :] - first[e]) for e in range(N_EXPERTS)]


def _dispatch_body(st_ref, h_ref, slot_ref, aff_ref, xs_ref, stage, sems, count, *, win):
    bi, j = pl.program_id(0), pl.program_id(1)
    d = h_ref.shape[-1]

    @pl.when((bi == 0) & (j == 0))
    def _():
        count[0] = 0

    rows_alloc = xs_ref.shape[2]
    last_block = j == pl.num_programs(1) - 1
    st = [st_ref[bi, e, j] for e in range(N_EXPERTS)]
    ends = [jnp.where(last_block, rows_alloc, st_ref[bi, e, j + 1]) for e in range(N_EXPERTS)]
    width = ends[0] - st[0]
    for e in range(1, N_EXPERTS):
        width = jnp.maximum(width, ends[e] - st[e])
    rounds = jnp.maximum((width + win - 1) // win, 1)

    def copies(buf, first):
        return [pltpu.make_async_copy(stage.at[buf, pl.ds(e * win, win), :],
                                      xs_ref.at[bi, e, pl.ds(pl.multiple_of(first[e], SUBLANES), win), :],
                                      sems.at[e]) for e in range(N_EXPERTS)]

    def one_round(r, carry):
        k = count[0]
        buf = k % 2
        first = [jnp.minimum(st[e] + r * win, rows_alloc - win) for e in range(N_EXPERTS)]
        hot = _onehot_t(slot_ref, first, win)
        p = jnp.concatenate(hot, axis=0).astype(BF16)
        stage[buf, :, 0:d] = jnp.dot(p, h_ref[0], preferred_element_type=F32)
        gates = [jnp.sum(jnp.where(hot[e], aff_ref[0, e:e + 1, :], 0.0), axis=1, keepdims=True)
                 for e in range(N_EXPERTS)]
        stage[buf, :, d:d + LANES] = jnp.broadcast_to(jnp.concatenate(gates, axis=0), (N_EXPERTS * win, LANES))

        @pl.when(k > 0)
        def _():
            for c in copies(1 - buf, st):
                c.wait()

        for c in copies(buf, first):
            c.start()
        count[0] = k + 1
        return carry

    lax.fori_loop(0, rounds, one_round, 0)

    @pl.when((bi == pl.num_programs(0) - 1) & (j == pl.num_programs(1) - 1))
    def _():
        for c in copies(0, st):
            c.wait()


def _dispatch(starts, h, slot, aff, rows_alloc, win):
    b, s, d = h.shape
    nblk = s // TOK
    e = N_EXPERTS
    return pl.pallas_call(
        functools.partial(_dispatch_body, win=win),
        grid_spec=pltpu.PrefetchScalarGridSpec(
            num_scalar_prefetch=1, grid=(b, nblk),
            in_specs=[pl.BlockSpec((1, TOK, d), lambda bi, j, st: (bi, j, 0)),
                      pl.BlockSpec((1, e, TOK), lambda bi, j, st: (bi, 0, j)),
                      pl.BlockSpec((1, e, TOK), lambda bi, j, st: (bi, 0, j))],
            out_specs=pl.BlockSpec(memory_space=pl.ANY),
            scratch_shapes=[pltpu.VMEM((2, e * win, d + LANES), F32),
                            pltpu.SemaphoreType.DMA((e,)),
                            pltpu.SMEM((1,), I32)]),
        out_shape=jax.ShapeDtypeStruct((b, e, rows_alloc, d + LANES), F32),
        compiler_params=_params("arbitrary", "arbitrary"),
        name="dispatch",
    )(starts, h, slot, aff)


def _ffn_body(st_ref, xs_ref, wg_ref, wu_ref, wd_ref, y_ref, *, nblk):
    e, bi, i = pl.program_id(0), pl.program_id(1), pl.program_id(2)
    tm = xs_ref.shape[2]
    d = wg_ref.shape[1]
    used = st_ref[bi, e, nblk]
    base = i * tm

    @pl.when(base < used)
    def _():
        valid = (base + lax.broadcasted_iota(I32, (tm, 1), 0)) < used
        xa = xs_ref[0, 0]
        x = jnp.where(valid, xa[:, 0:d], 0.0).astype(BF16)
        gate = jnp.where(valid, xa[:, d:d + 1], 0.0)
        a = jnp.dot(x, wg_ref[0], preferred_element_type=F32)
        u = jnp.dot(x, wu_ref[0], preferred_element_type=F32)
        mid = (a * jax.nn.sigmoid(a) * u).astype(BF16)
        y_ref[0, 0] = jnp.dot(mid, wd_ref[0], preferred_element_type=F32) * gate

    @pl.when(base >= used)
    def _():
        y_ref[0, 0] = jnp.zeros(y_ref.shape[2:], F32)


def _ffn(starts, xs, wg, wu, wd, rows, tm, nblk):
    b, e = xs.shape[:2]
    d, f = wg.shape[1:]
    return pl.pallas_call(
        functools.partial(_ffn_body, nblk=nblk),
        grid_spec=pltpu.PrefetchScalarGridSpec(
            num_scalar_prefetch=1, grid=(e, b, rows // tm),
            in_specs=[pl.BlockSpec((1, 1, tm, d + LANES), lambda ei, bi, i, st: (bi, ei, i, 0)),
                      pl.BlockSpec((1, d, f), lambda ei, bi, i, st: (ei, 0, 0)),
                      pl.BlockSpec((1, d, f), lambda ei, bi, i, st: (ei, 0, 0)),
                      pl.BlockSpec((1, f, d), lambda ei, bi, i, st: (ei, 0, 0))],
            out_specs=pl.BlockSpec((1, 1, tm, d), lambda ei, bi, i, st: (bi, ei, i, 0))),
        out_shape=jax.ShapeDtypeStruct((b, e, rows, d), F32),
        compiler_params=_params("arbitrary", "arbitrary", "arbitrary"),
        name="expert_ffn",
    )(starts, xs, wg, wu, wd)


def _combine_body(st_ref, slot_ref, x_ref, gt_ref, gf_ref, y_ref, o_ref, ybuf, sems, *, win, rows, final_norm):
    bi, j = pl.program_id(0), pl.program_id(1)
    d = x_ref.shape[-1]
    st = [st_ref[bi, e, j] for e in range(N_EXPERTS)]
    width = st_ref[bi, 0, j + 1] - st[0]
    for e in range(1, N_EXPERTS):
        width = jnp.maximum(width, st_ref[bi, e, j + 1] - st[e])
    rounds = (width + win - 1) // win

    def one_round(r, acc):
        lo = [st[e] + r * win for e in range(N_EXPERTS)]
        first = [jnp.minimum(lo[e], rows - win) for e in range(N_EXPERTS)]
        cps = [pltpu.make_async_copy(y_ref.at[bi, e, pl.ds(pl.multiple_of(first[e], SUBLANES), win), :],
                                     ybuf.at[pl.ds(e * win, win), :], sems.at[e]) for e in range(N_EXPERTS)]
        for c in cps:
            c.start()
        hot = _onehot_t(slot_ref, first, win)
        hot = [hot[e] & (slot_ref[0, e:e + 1, :] >= lo[e]) for e in range(N_EXPERTS)]
        p = jnp.concatenate(hot, axis=0).astype(BF16)
        for c in cps:
            c.wait()
        y = ybuf[...]
        y_hi = y.astype(BF16)
        y_lo = (y - y_hi.astype(F32)).astype(BF16)
        return (acc + lax.dot_general(p, y_hi, TN_DIMS, preferred_element_type=F32)
                + lax.dot_general(p, y_lo, TN_DIMS, preferred_element_type=F32))

    acc = lax.fori_loop(0, rounds, one_round, jnp.zeros((TOK, d), F32))
    out = x_ref[0] + gt_ref[0] * acc
    if final_norm:
        ms = jnp.mean(out * out, axis=-1, keepdims=True)
        out = out * lax.rsqrt(ms + EPS) * gf_ref[...]
    o_ref[0] = out


def _combine(starts, slot, x, gate, g_final, y, win, final_norm):
    b, s, d = x.shape
    e = N_EXPERTS
    rows = y.shape[2]
    return pl.pallas_call(
        functools.partial(_combine_body, win=win, rows=rows, final_norm=final_norm),
        grid_spec=pltpu.PrefetchScalarGridSpec(
            num_scalar_prefetch=1, grid=(b, s // TOK),
            in_specs=[pl.BlockSpec((1, e, TOK), lambda bi, j, st: (bi, 0, j)),
                      pl.BlockSpec((1, TOK, d), lambda bi, j, st: (bi, j, 0)),
                      pl.BlockSpec((1, 1, d), lambda bi, j, st: (bi, 0, 0)),
                      pl.BlockSpec((1, d), lambda bi, j, st: (0, 0)),
                      pl.BlockSpec(memory_space=pl.ANY)],
            out_specs=pl.BlockSpec((1, TOK, d), lambda bi, j, st: (bi, j, 0)),
            scratch_shapes=[pltpu.VMEM((e * win, d), F32), pltpu.SemaphoreType.DMA((e,))]),
        out_shape=jax.ShapeDtypeStruct((b, s, d), F32),
        compiler_params=_params("arbitrary", "arbitrary"),
        name="combine",
    )(starts, slot, x, gate, g_final, y)


def _moe(x, h, logits_t, gate, g_final, wg, wu, wd, final_norm):
    n = x.shape[1]
    nblk = n // TOK
    cap = CAPACITY_FACTOR * n // N_EXPERTS
    win = min(64, cap)
    rows = cap + SUBLANES * nblk
    tm = min(TOK, rows)
    rows = -(-rows // tm) * tm
    slot, aff, starts = _router(logits_t, cap)
    xs = _dispatch(starts, h, slot, aff, rows + win, win)
    y = _ffn(starts, xs, wg, wu, wd, rows, tm, nblk)
    return _combine(starts, slot, x, gate, g_final, y, win, final_norm)


def _rope_tables(n_tokens, rot_dim):
    rows = n_tokens // GRID_W
    row = jnp.repeat(jnp.arange(rows), GRID_W)
    col = jnp.tile(jnp.arange(GRID_W), rows)
    quarter = rot_dim // 4
    inv_freq = ROPE_THETA ** (-jnp.arange(quarter, dtype=F32) / quarter)
    ang = jnp.concatenate([inv_freq[:, None] * row[None, :], inv_freq[:, None] * col[None, :]], axis=0)
    return jnp.cos(ang), jnp.sin(ang)


def _identity_tables(n_tokens, rot_dim):
    return jnp.ones((rot_dim // 2, n_tokens), F32), jnp.zeros((rot_dim // 2, n_tokens), F32)


def kernel(x, c, ctx, c_ctx, w_mod, b_mod, g_attn, g_ffn, w_in, a_sink, b_q_norm, b_w_uq, b_kv_norm, b_w_ukv, c_lambda, c_subln, d_q_norm, d_k_norm, w_out, w_router, w_gate, w_up, w_down, g_final):
    b, s, d = x.shape
    depth = w_mod.shape[0]
    n_ctx = ctx.shape[1]
    rope64, rope32 = _rope_tables(s, HEAD_DIM), _rope_tables(s, C_QK)
    id64, id32 = _identity_tables(n_ctx, HEAD_DIM), _identity_tables(n_ctx, C_QK)

    cond = jnp.zeros((SUBLANES, d), F32).at[:b].set(c).at[b].set(c_ctx)
    mod = _adaln(cond, w_mod, b_mod)

    col = lambda v: v.reshape(-1, 1)
    xl, xc = x, ctx
    for l in range(depth):
        last = l == depth - 1
        lam_init = 0.8 - 0.6 * math.exp(-0.3 * l)
        m6 = mod[l].reshape(SUBLANES, 6, d)
        lat = [m6[:b, k][:, None, :] for k in range(6)]
        cx = [jnp.broadcast_to(m6[b, k][None, None, :], (b, 1, d)) for k in range(6)]
        w_in_t = w_in[l].T.astype(BF16)
        wuq_t = b_w_uq[l].T.astype(BF16)
        wukv_t = b_w_ukv[l].T.astype(BF16)
        w_out_b = w_out[l].astype(BF16)
        w_router_t = w_router[l].T
        wg, wu, wd = w_gate[l].astype(BF16), w_up[l].astype(BF16), w_down[l].astype(BF16)
        g_a, g_f = g_attn[l][None, :], g_ffn[l][None, :]
        small = (col(b_q_norm[l]), wuq_t, col(b_kv_norm[l]), wukv_t, col(d_q_norm[l]), col(d_k_norm[l]))

        pl_ = _proj(xl, g_a, lat[0], lat[1], w_in_t, rope64, rope32, *small)
        pc_ = _proj(xc, g_a, cx[0], cx[1], w_in_t, id64, id32, *small)
        qa, ka, va, qb, kb, vb, qc, kc, vc, qd, kd, vd = pl_
        qa_c, ka_c, va_c, qb_c, kb_c, vb_c, qc_c, kc_c, vc_c, qd_c, kd_c, vd_c = pc_

        zero = lambda h: 0
        oa = _window_attn(qa, ka_c, va_c, ka, va, a_sink[l])
        ob = _dense_attn(qb, kb_c, vb_c, kb, vb, lambda h: h, lambda h: h)
        oc = _dense_attn(qc, kc_c, vc_c, kc, vc, lambda h: h // 4, lambda h: h // 2)
        od = _dense_attn(qd, kd_c, vd_c, kd, vd, zero, lambda h: h // 2)
        outproj = functools.partial(_outproj, lam_p=c_lambda[l], subln=col(c_subln[l]), w_out=w_out_b,
                                    g_ffn=g_f, w_router_t=w_router_t, lam_init=lam_init)
        xl, hl, lg = outproj(oa, ob, oc, od, x=xl, gate=lat[2], shift=lat[3], scale=lat[4])
        xl = _moe(xl, hl, lg, lat[5], g_final[None, :], wg, wu, wd, final_norm=last)
        if not last:
            oa = _dense_attn(qa_c, ka_c, va_c, None, None, zero, lambda h: h // 2, sink=a_sink[l])
            ob = _dense_attn(qb_c, kb_c, vb_c, None, None, lambda h: h, lambda h: h)
            oc = _dense_attn(qc_c, kc_c, vc_c, None, None, lambda h: h // 4, lambda h: h // 2)
            od = _dense_attn(qd_c, kd_c, vd_c, None, None, zero, lambda h: h // 2)
            xc, hc, lgc = outproj(oa, ob, oc, od, x=xc, gate=cx[2], shift=cx[3], scale=cx[4])
            xc = _moe(xc, hc, lgc, cx[5], g_final[None, :], wg, wu, wd, final_norm=False)
    return xl
```

```python
import functools
import math

import jax
import jax.numpy as jnp
from jax import lax
from jax.experimental import pallas as pl
from jax.experimental.pallas import tpu as pltpu

F32, BF16, I32 = jnp.float32, jnp.bfloat16, jnp.int32
HIGHEST = lax.Precision.HIGHEST

SUBLANES = 8
LANES = 128

GRID_W = 64
HEAD_DIM = 64
WINDOW = 128
EPS = 1e-6
ROPE_THETA = 10000.0
NEG_INF = -1e30
A_HEADS, A_KV = 4, 2
B_HEADS, B_Q_RANK, B_KV_RANK, B_NOPE, B_ROPE, B_V = 4, 192, 128, 64, 32, 64
C_HEADS, C_QK, C_V = 4, 32, 64
D_HEADS, D_KV = 4, 2
N_EXPERTS = 16
CAPACITY_FACTOR = 2

A_COLS = A_HEADS * HEAD_DIM + 2 * A_KV * HEAD_DIM
B_COLS = B_Q_RANK + B_KV_RANK + B_ROPE
C_COLS = 4 * C_HEADS * C_QK + C_HEADS * C_V
D_COLS = D_HEADS * HEAD_DIM + 2 * D_KV * HEAD_DIM
A_OFF, B_OFF, C_OFF, D_OFF = 0, A_COLS, A_COLS + B_COLS, A_COLS + B_COLS + C_COLS

BISECT_STEPS = 152
TOK = 256
QPAD = 128
VROWS = HEAD_DIM + 16
TQ = 1024
LOG2E = math.log2(math.e)

NT_DIMS = (((1,), (1,)), ((), ()))
TN_DIMS = (((0,), (0,)), ((), ()))


def _params(*sem):
    return pltpu.CompilerParams(dimension_semantics=sem, vmem_limit_bytes=56 * 1024 * 1024)


def _adaln_body(c_ref, w_ref, b_ref, o_ref):
    c = c_ref[...]
    s = c * jax.nn.sigmoid(c)
    o_ref[0] = jnp.dot(s, w_ref[0], precision=HIGHEST, preferred_element_type=F32) + b_ref[0]


def _adaln(cond, w_mod, b_mod):
    depth, d, n = w_mod.shape
    tn = n // 4
    return pl.pallas_call(
        _adaln_body,
        grid=(depth, n // tn),
        in_specs=[pl.BlockSpec((SUBLANES, d), lambda l, j: (0, 0)),
                  pl.BlockSpec((1, d, tn), lambda l, j: (l, 0, j)),
                  pl.BlockSpec((1, 1, tn), lambda l, j: (l, 0, j))],
        out_specs=pl.BlockSpec((1, SUBLANES, tn), lambda l, j: (l, 0, j)),
        out_shape=jax.ShapeDtypeStruct((depth, SUBLANES, n), F32),
        compiler_params=_params("arbitrary", "arbitrary"),
        name="adaln",
    )(cond, w_mod, b_mod.reshape(depth, 1, n))


def _rope_t(xt, cos, sin):
    half = xt.shape[0] // 2
    x1, x2 = xt[:half], xt[half:]
    return jnp.concatenate([x1 * cos - x2 * sin, x1 * sin + x2 * cos], axis=0)


def _rms_t(xt, g):
    ms = jnp.mean(xt * xt, axis=0, keepdims=True)
    return xt * lax.rsqrt(ms + EPS) * g


def _modulate(x, g, shift, scale):
    ms = jnp.mean(x * x, axis=-1, keepdims=True)
    return x * lax.rsqrt(ms + EPS) * g * (1.0 + scale) + shift


def _proj_body(x_ref, g_ref, sh_ref, sc_ref, w_ref, c64_ref, s64_ref, c32_ref, s32_ref,
               bqn_ref, wuq_ref, bkvn_ref, wukv_ref, dqn_ref, dkn_ref,
               qa_ref, ka_ref, va_ref, qb_ref, kb_ref, vb_ref,
               qc_ref, kc_ref, vc_ref, qd_ref, kd_ref, vd_ref):
    tm = x_ref.shape[1]
    h = _modulate(x_ref[0], g_ref[...], sh_ref[0], sc_ref[0]).astype(BF16)
    pt = lax.dot_general(w_ref[...], h, NT_DIMS, preferred_element_type=F32)
    c64, s64, c32, s32 = c64_ref[...], s64_ref[...], c32_ref[...], s32_ref[...]
    z64 = jnp.zeros((64, tm), F32)
    z32 = jnp.zeros((32, tm), F32)
    scale64 = HEAD_DIM ** -0.5 * LOG2E
    scale_b = (B_NOPE + B_ROPE) ** -0.5 * LOG2E
    scale_c = C_QK ** -0.5 * LOG2E
    ones = jnp.ones((VROWS - HEAD_DIM, tm), BF16)

    def put_values(v_ref, hh, vt):
        v_ref[0, 0, VROWS * hh: VROWS * hh + HEAD_DIM] = vt.astype(BF16)
        v_ref[0, 0, VROWS * hh + HEAD_DIM: VROWS * hh + VROWS] = ones

    def gqa(off, q_ref, k_ref, v_ref, qn, kn):
        for hh in range(4):
            q = pt[off + 64 * hh: off + 64 * hh + 64]
            if qn is not None:
                q = _rms_t(q, qn)
            q = _rope_t(q, c64, s64) * scale64
            q_ref[0, hh] = (jnp.concatenate([q, z64], axis=0) if hh // 2 == 0
                            else jnp.concatenate([z64, q], axis=0)).astype(BF16)
        ks = []
        for j in range(2):
            k = pt[off + 256 + 64 * j: off + 256 + 64 * j + 64]
            if kn is not None:
                k = _rms_t(k, kn)
            ks.append(_rope_t(k, c64, s64))
        k_ref[0] = jnp.concatenate(ks, axis=0).T.astype(BF16)
        for j in range(2):
            put_values(v_ref, j, pt[off + 384 + 64 * j: off + 384 + 64 * j + 64])

    gqa(A_OFF, qa_ref, ka_ref, va_ref, None, None)
    gqa(D_OFF, qd_ref, kd_ref, vd_ref, dqn_ref[...], dkn_ref[...])

    cq = _rms_t(pt[B_OFF: B_OFF + B_Q_RANK], bqn_ref[...]).astype(BF16)
    ckv = _rms_t(pt[B_OFF + B_Q_RANK: B_OFF + B_Q_RANK + B_KV_RANK], bkvn_ref[...]).astype(BF16)
    kr = _rope_t(pt[B_OFF + B_Q_RANK + B_KV_RANK: B_OFF + B_COLS], c32, s32)
    qt = jnp.dot(wuq_ref[...], cq, preferred_element_type=F32)
    kvt = jnp.dot(wukv_ref[...], ckv, preferred_element_type=F32)
    kparts = []
    dq = B_NOPE + B_ROPE
    for hh in range(B_HEADS):
        qn_ = qt[dq * hh: dq * hh + B_NOPE]
        qr = _rope_t(qt[dq * hh + B_NOPE: dq * hh + dq], c32, s32)
        qb_ref[0, hh] = (jnp.concatenate([qn_, qr, z32], axis=0) * scale_b).astype(BF16)
        kparts.append(jnp.concatenate([kvt[128 * hh: 128 * hh + B_NOPE], kr, z32], axis=0))
        put_values(vb_ref, hh, kvt[128 * hh + B_NOPE: 128 * hh + 128])
    kb_ref[0] = jnp.concatenate(kparts, axis=0).T.astype(BF16)

    for j in range(2 * C_HEADS):
        q = _rope_t(pt[C_OFF + 32 * j: C_OFF + 32 * j + 32], c32, s32) * scale_c
        pieces = [z32, z32, z32, z32]
        pieces[j % 4] = q
        qc_ref[0, j] = jnp.concatenate(pieces, axis=0).astype(BF16)
    kc = [_rope_t(pt[C_OFF + 256 + 32 * j: C_OFF + 256 + 32 * j + 32], c32, s32) for j in range(2 * C_HEADS)]
    kc_ref[0] = jnp.concatenate(kc, axis=0).T.astype(BF16)
    for hh in range(C_HEADS):
        put_values(vc_ref, hh, pt[C_OFF + 512 + 64 * hh: C_OFF + 512 + 64 * hh + 64])


def _proj(x, g, shift, scale, w_in_t, rope64, rope32, bqn, wuq_t, bkvn, wukv_t, dqn, dkn):
    b, s, d = x.shape
    tm = TOK
    n = s // tm
    full = lambda a: pl.BlockSpec(a.shape, lambda bi, i: (0,) * a.ndim)
    q_spec = lambda nh: pl.BlockSpec((1, nh, QPAD, tm), lambda bi, i: (bi, 0, 0, i))
    k_spec = lambda w: pl.BlockSpec((1, tm, w), lambda bi, i: (bi, i, 0))
    v_spec = lambda r: pl.BlockSpec((1, 1, r, tm), lambda bi, i: (bi, i, 0, 0))
    q_shape = lambda nh: jax.ShapeDtypeStruct((b, nh, QPAD, s), BF16)
    k_shape = lambda w: jax.ShapeDtypeStruct((b, s, w), BF16)
    v_shape = lambda r: jax.ShapeDtypeStruct((b, n, r, tm), BF16)
    tab = lambda t: pl.BlockSpec((t.shape[0], tm), lambda bi, i: (0, i))
    c64, s64 = rope64
    c32, s32 = rope32
    return pl.pallas_call(
        _proj_body,
        grid=(b, n),
        in_specs=[pl.BlockSpec((1, tm, d), lambda bi, i: (bi, i, 0)), full(g),
                  pl.BlockSpec((1, 1, d), lambda bi, i: (bi, 0, 0)),
                  pl.BlockSpec((1, 1, d), lambda bi, i: (bi, 0, 0)),
                  full(w_in_t), tab(c64), tab(s64), tab(c32), tab(s32),
                  full(bqn), full(wuq_t), full(bkvn), full(wukv_t), full(dqn), full(dkn)],
        out_specs=[q_spec(4), k_spec(128), v_spec(2 * VROWS),
                   q_spec(4), k_spec(512), v_spec(4 * VROWS),
                   q_spec(8), k_spec(256), v_spec(4 * VROWS),
                   q_spec(4), k_spec(128), v_spec(2 * VROWS)],
        out_shape=[q_shape(4), k_shape(128), v_shape(2 * VROWS),
                   q_shape(4), k_shape(512), v_shape(4 * VROWS),
                   q_shape(8), k_shape(256), v_shape(4 * VROWS),
                   q_shape(4), k_shape(128), v_shape(2 * VROWS)],
        compiler_params=_params("arbitrary", "arbitrary"),
        name="proj",
    )(x, g, shift, scale, w_in_t, c64, s64, c32, s32, bqn, wuq_t, bkvn, wukv_t, dqn, dkn)


def _dense_body(*refs, has_lat, has_sink, n_lat):
    refs = list(refs)
    sink_ref = refs.pop(0) if has_sink else None
    q_ref, kc_ref, vc_ref = refs[:3]
    kl_ref, vl_ref = (refs[3], refs[4]) if has_lat else (None, None)
    o_ref, s_even, s_odd, acc_ref = refs[-4:]
    tq = q_ref.shape[-1]
    qt = q_ref[0, 0]

    def produce(kblk, s_ref):
        s = jnp.dot(kblk, qt, preferred_element_type=F32)
        s_ref[...] = s
        return jnp.max(s, axis=0, keepdims=True)

    def consume(s_ref, mx, vblk, m):
        m_new = jnp.maximum(m, mx)
        p = jnp.exp2(s_ref[...] - m_new).astype(BF16)
        acc_ref[...] = jnp.exp2(m - m_new) * acc_ref[...] + jnp.dot(vblk, p, preferred_element_type=F32)
        return m_new

    def lat_keys(i):
        return kl_ref[0, pl.ds(pl.multiple_of(i * TOK, TOK), TOK), :]

    acc_ref[...] = jnp.zeros(acc_ref.shape, F32)
    mx_c = produce(kc_ref[0], s_odd)
    if has_lat:
        mx_e = produce(lat_keys(0), s_even)
    m = consume(s_odd, mx_c, vc_ref[0, 0], jnp.full((1, tq), NEG_INF, F32))
    if has_lat:
        def body(j, carry):
            m, mx_e = carry
            mx_o = produce(lat_keys(2 * j + 1), s_odd)
            m = consume(s_even, mx_e, vl_ref[0, 2 * j], m)
            mx_e = produce(lat_keys(jnp.minimum(2 * j + 2, n_lat - 1)), s_even)
            m = consume(s_odd, mx_o, vl_ref[0, 2 * j + 1], m)
            return m, mx_e
        m, _ = lax.fori_loop(0, n_lat // 2, body, (m, mx_e))
    acc = acc_ref[...]
    num, den = acc[:HEAD_DIM], acc[HEAD_DIM:HEAD_DIM + 1]
    if has_sink:
        sk = sink_ref[pl.program_id(1)] * LOG2E
        m2 = jnp.maximum(m, sk)
        a = jnp.exp2(m - m2)
        den = den * a + jnp.exp2(sk - m2)
        num = num * a
    o_ref[0, 0] = num / den


def _dense_attn(q, k_ctx, v_ctx, k_lat, v_lat, kgroup, vhead, sink=None):
    b, nh, _, s = q.shape
    tq = min(TQ, s)
    has_lat = k_lat is not None
    has_sink = sink is not None
    sc = k_ctx.shape[1]
    in_specs = [pl.BlockSpec((1, 1, QPAD, tq), lambda bi, h, i: (bi, h, 0, i)),
                pl.BlockSpec((1, sc, LANES), lambda bi, h, i: (bi, 0, kgroup(h))),
                pl.BlockSpec((1, 1, VROWS, sc), lambda bi, h, i: (bi, 0, vhead(h), 0))]
    args = [q, k_ctx, v_ctx]
    n_lat = 0
    if has_lat:
        sl = k_lat.shape[1]
        n_lat = sl // TOK
        assert n_lat % 2 == 0
        in_specs += [pl.BlockSpec((1, sl, LANES), lambda bi, h, i: (bi, 0, kgroup(h))),
                     pl.BlockSpec((1, n_lat, VROWS, TOK), lambda bi, h, i: (bi, 0, vhead(h), 0))]
        args += [k_lat, v_lat]
    if has_sink:
        in_specs = [pl.BlockSpec(memory_space=pltpu.SMEM)] + in_specs
        args = [sink] + args
    return pl.pallas_call(
        functools.partial(_dense_body, has_lat=has_lat, has_sink=has_sink, n_lat=n_lat),
        grid=(b, nh, s // tq),
        in_specs=in_specs,
        out_specs=pl.BlockSpec((1, 1, HEAD_DIM, tq), lambda bi, h, i: (bi, h, 0, i)),
        out_shape=jax.ShapeDtypeStruct((b, nh, HEAD_DIM, s), F32),
        scratch_shapes=[pltpu.VMEM((TOK, tq), F32), pltpu.VMEM((TOK, tq), F32), pltpu.VMEM((VROWS, tq), F32)],
        compiler_params=_params("arbitrary", "arbitrary", "arbitrary"),
        name="dense_attn",
    )(*args)


def _window_body(sink_ref, q_ref, kc_ref, vc_ref, kp_ref, kq_ref, kn_ref, vp_ref, vq_ref, vn_ref, o_ref):
    i = pl.program_id(1)
    n = pl.num_programs(1)
    tq = q_ref.shape[-1]
    key = lax.broadcasted_iota(I32, (TOK, tq), 0)
    qry = lax.broadcasted_iota(I32, (TOK, tq), 1)
    d = key - qry
    ok_prev = (d >= TOK - WINDOW) & (i > 0)
    ok_cur = jnp.abs(d) <= WINDOW
    ok_next = (d <= WINDOW - TOK) & (i < n - 1)
    for hh in range(A_HEADS):
        qt = q_ref[0, hh]
        rows = slice(VROWS * (hh // 2), VROWS * (hh // 2) + VROWS)
        sk = sink_ref[hh] * LOG2E
        s_c = jnp.dot(kc_ref[0], qt, preferred_element_type=F32)
        s_p = jnp.where(ok_prev, jnp.dot(kp_ref[0], qt, preferred_element_type=F32), NEG_INF)
        s_q = jnp.where(ok_cur, jnp.dot(kq_ref[0], qt, preferred_element_type=F32), NEG_INF)
        s_n = jnp.where(ok_next, jnp.dot(kn_ref[0], qt, preferred_element_type=F32), NEG_INF)
        m = jnp.maximum(jnp.maximum(jnp.max(s_c, axis=0, keepdims=True), jnp.max(s_p, axis=0, keepdims=True)),
                        jnp.maximum(jnp.max(s_q, axis=0, keepdims=True), jnp.max(s_n, axis=0, keepdims=True)))
        m = jnp.maximum(m, sk)
        acc = jnp.zeros((VROWS, tq), F32)
        for s_, v_ref in ((s_c, vc_ref), (s_p, vp_ref), (s_q, vq_ref), (s_n, vn_ref)):
            p = jnp.exp2(s_ - m).astype(BF16)
            acc = acc + jnp.dot(v_ref[0, 0, rows, :], p, preferred_element_type=F32)
        o_ref[0, hh] = acc[:HEAD_DIM] / (acc[HEAD_DIM:HEAD_DIM + 1] + jnp.exp2(sk - m))


def _window_attn(q, k_ctx, v_ctx, k_lat, v_lat, sink):
    b, nh, _, s = q.shape
    n = s // TOK
    sc = k_ctx.shape[1]
    prev = lambda i: jnp.maximum(i - 1, 0)
    nxt = lambda i: jnp.minimum(i + 1, n - 1)
    kspec = lambda f: pl.BlockSpec((1, TOK, LANES), lambda bi, i: (bi, f(i), 0))
    vspec = lambda f: pl.BlockSpec((1, 1, 2 * VROWS, TOK), lambda bi, i: (bi, f(i), 0, 0))
    same = lambda i: i
    return pl.pallas_call(
        _window_body,
        grid=(b, n),
        in_specs=[pl.BlockSpec(memory_space=pltpu.SMEM),
                  pl.BlockSpec((1, nh, QPAD, TOK), lambda bi, i: (bi, 0, 0, i)),
                  pl.BlockSpec((1, sc, LANES), lambda bi, i: (bi, 0, 0)),
                  pl.BlockSpec((1, 1, 2 * VROWS, sc), lambda bi, i: (bi, 0, 0, 0)),
                  kspec(prev), kspec(same), kspec(nxt), vspec(prev), vspec(same), vspec(nxt)],
        out_specs=pl.BlockSpec((1, nh, HEAD_DIM, TOK), lambda bi, i: (bi, 0, 0, i)),
        out_shape=jax.ShapeDtypeStruct((b, nh, HEAD_DIM, s), F32),
        compiler_params=_params("arbitrary", "arbitrary"),
        name="window_attn",
    )(sink, q, k_ctx, v_ctx, k_lat, k_lat, k_lat, v_lat, v_lat, v_lat)


def _outproj_body(oa_ref, ob_ref, oc_ref, od_ref, lam_ref, subln_ref, w_ref, x_ref, gt_ref,
                  g_ref, sh_ref, sc_ref, wr_ref, xo_ref, h_ref, lg_ref, *, lam_init):
    lp = lam_ref[...]
    lam = (jnp.exp(jnp.sum(lp[0:1] * lp[1:2], axis=1, keepdims=True))
           - jnp.exp(jnp.sum(lp[2:3] * lp[3:4], axis=1, keepdims=True)) + lam_init)
    parts = [oa_ref[0, hh] for hh in range(4)] + [ob_ref[0, hh] for hh in range(4)]
    for hh in range(C_HEADS):
        o = oc_ref[0, 2 * hh] - lam * oc_ref[0, 2 * hh + 1]
        parts.append(_rms_t(o, subln_ref[...]) * (1.0 - lam_init))
    parts += [od_ref[0, hh] for hh in range(4)]
    ot = jnp.concatenate(parts, axis=0).astype(BF16)
    out = lax.dot_general(ot, w_ref[...], TN_DIMS, preferred_element_type=F32)
    xn = x_ref[0] + gt_ref[0] * out
    xo_ref[0] = xn
    h = _modulate(xn, g_ref[...], sh_ref[0], sc_ref[0])
    h_ref[0] = h.astype(BF16)
    lg_ref[0] = lax.dot_general(wr_ref[...], h, NT_DIMS, precision=HIGHEST, preferred_element_type=F32)


def _outproj(oa, ob, oc, od, lam_p, subln, w_out, x, gate, g_ffn, shift, scale, w_router_t, lam_init):
    b, s, d = x.shape
    tm = TOK
    ospec = lambda nh: pl.BlockSpec((1, nh, HEAD_DIM, tm), lambda bi, i: (bi, 0, 0, i))
    full = lambda a: pl.BlockSpec(a.shape, lambda bi, i: (0,) * a.ndim)
    row = pl.BlockSpec((1, 1, d), lambda bi, i: (bi, 0, 0))
    return pl.pallas_call(
        functools.partial(_outproj_body, lam_init=lam_init),
        grid=(b, s // tm),
        in_specs=[ospec(4), ospec(4), ospec(8), ospec(4), full(lam_p), full(subln), full(w_out),
                  pl.BlockSpec((1, tm, d), lambda bi, i: (bi, i, 0)), row, full(g_ffn), row, row,
                  full(w_router_t)],
        out_specs=[pl.BlockSpec((1, tm, d), lambda bi, i: (bi, i, 0)),
                   pl.BlockSpec((1, tm, d), lambda bi, i: (bi, i, 0)),
                   pl.BlockSpec((1, N_EXPERTS, tm), lambda bi, i: (bi, 0, i))],
        out_shape=[jax.ShapeDtypeStruct((b, s, d), F32), jax.ShapeDtypeStruct((b, s, d), BF16),
                   jax.ShapeDtypeStruct((b, N_EXPERTS, s), F32)],
        compiler_params=_params("arbitrary", "arbitrary"),
        name="outproj",
    )(oa, ob, oc, od, lam_p, subln, w_out, x, gate, g_ffn, shift, scale, w_router_t)


def _router_body(lg_ref, slot_ref, aff_ref, st_ref, *, cap, nblk):
    lg = lg_ref[0]
    ex = jnp.exp(lg - jnp.max(lg, axis=0, keepdims=True))
    aff = ex / jnp.sum(ex, axis=0, keepdims=True)
    aff_ref[0] = aff

    def search(_, bounds):
        lo, hi = bounds
        mid = (lo + hi) * 0.5
        enough = jnp.sum((aff >= mid).astype(I32), axis=1, keepdims=True) >= cap
        return jnp.where(enough, mid, lo), jnp.where(enough, hi, mid)

    lo, hi = lax.fori_loop(0, BISECT_STEPS, search,
                           (jnp.zeros((N_EXPERTS, 1), F32), jnp.full((N_EXPERTS, 1), 2.0, F32)))
    gt = aff >= hi
    eq = (aff >= lo) & jnp.logical_not(gt)
    need = (cap - jnp.sum(gt.astype(I32), axis=1, keepdims=True)).astype(F32)
    tri = (lax.broadcasted_iota(I32, (TOK, TOK), 0) <= lax.broadcasted_iota(I32, (TOK, TOK), 1)).astype(BF16)
    lane = lax.broadcasted_iota(I32, (N_EXPERTS, LANES), 1)
    eq_seen = jnp.zeros((N_EXPERTS, 1), F32)
    base = jnp.zeros((N_EXPERTS, 1), I32)
    starts = jnp.zeros((N_EXPERTS, LANES), I32)
    for j in range(nblk):
        cols = slice(TOK * j, TOK * j + TOK)
        eq_c = eq[:, cols]
        eq_cum = jnp.dot(eq_c.astype(BF16), tri, preferred_element_type=F32) + eq_seen
        eq_seen = eq_seen + jnp.sum(eq_c.astype(F32), axis=1, keepdims=True)
        sel = gt[:, cols] | (eq_c & (eq_cum <= need))
        cum = jnp.dot(sel.astype(BF16), tri, preferred_element_type=F32).astype(I32)
        slot_ref[0, :, cols] = jnp.where(sel, base + cum - 1, -1)
        starts = jnp.where(lane == j, base, starts)
        cnt = jnp.sum(sel.astype(I32), axis=1, keepdims=True)
        base = base + jnp.bitwise_and(cnt + (SUBLANES - 1), -SUBLANES)
    st_ref[0] = jnp.where(lane == nblk, base, starts)


def _router(logits_t, cap):
    b, e, s = logits_t.shape
    nblk = s // TOK
    return pl.pallas_call(
        functools.partial(_router_body, cap=cap, nblk=nblk),
        grid=(b,),
        in_specs=[pl.BlockSpec((1, e, s), lambda bi: (bi, 0, 0))],
        out_specs=[pl.BlockSpec((1, e, s), lambda bi: (bi, 0, 0)),
                   pl.BlockSpec((1, e, s), lambda bi: (bi, 0, 0)),
                   pl.BlockSpec((1, e, LANES), lambda bi: (bi, 0, 0))],
        out_shape=[jax.ShapeDtypeStruct((b, e, s), I32), jax.ShapeDtypeStruct((b, e, s), F32),
                   jax.ShapeDtypeStruct((b, e, LANES), I32)],
        compiler_params=_params("arbitrary"),
        name="router",
    )(logits_t)


def _onehot_t(slot_ref, first, win):
    rows = lax.broadcasted_iota(I32, (win, TOK), 0)
    return [rows == (slot_ref[0, e:e + 1, :] - first[e]) for e in range(N_EXPERTS)]


def _dispatch_body(st_ref, h_ref, slot_ref, aff_ref, xs_ref, stage, sems, count, *, win):
    bi, j = pl.program_id(0), pl.program_id(1)
    d = h_ref.shape[-1]

    @pl.when((bi == 0) & (j == 0))
    def _():
        count[0] = 0

    rows_alloc = xs_ref.shape[2]
    last_block = j == pl.num_programs(1) - 1
    st = [st_ref[bi, e, j] for e in range(N_EXPERTS)]
    ends = [jnp.where(last_block, rows_alloc, st_ref[bi, e, j + 1]) for e in range(N_EXPERTS)]
    width = ends[0] - st[0]
    for e in range(1, N_EXPERTS):
        width = jnp.maximum(width, ends[e] - st[e])
    rounds = jnp.maximum((width + win - 1) // win, 1)

    def copies(buf, first):
        return [pltpu.make_async_copy(stage.at[buf, pl.ds(e * win, win), :],
                                      xs_ref.at[bi, e, pl.ds(pl.multiple_of(first[e], SUBLANES), win), :],
                                      sems.at[e]) for e in range(N_EXPERTS)]

    def one_round(r, carry):
        k = count[0]
        buf = k % 2
        first = [jnp.minimum(st[e] + r * win, rows_alloc - win) for e in range(N_EXPERTS)]
        hot = _onehot_t(slot_ref, first, win)
        p = jnp.concatenate(hot, axis=0).astype(BF16)
        stage[buf, :, 0:d] = jnp.dot(p, h_ref[0], preferred_element_type=F32)
        gates = [jnp.sum(jnp.where(hot[e], aff_ref[0, e:e + 1, :], 0.0), axis=1, keepdims=True)
                 for e in range(N_EXPERTS)]
        stage[buf, :, d:d + LANES] = jnp.broadcast_to(jnp.concatenate(gates, axis=0), (N_EXPERTS * win, LANES))

        @pl.when(k > 0)
        def _():
            for c in copies(1 - buf, st):
                c.wait()

        for c in copies(buf, first):
            c.start()
        count[0] = k + 1
        return carry

    lax.fori_loop(0, rounds, one_round, 0)

    @pl.when((bi == pl.num_programs(0) - 1) & (j == pl.num_programs(1) - 1))
    def _():
        for c in copies(0, st):
            c.wait()


def _dispatch(starts, h, slot, aff, rows_alloc, win):
    b, s, d = h.shape
    nblk = s // TOK
    e = N_EXPERTS
    return pl.pallas_call(
        functools.partial(_dispatch_body, win=win),
        grid_spec=pltpu.PrefetchScalarGridSpec(
            num_scalar_prefetch=1, grid=(b, nblk),
            in_specs=[pl.BlockSpec((1, TOK, d), lambda bi, j, st: (bi, j, 0)),
                      pl.BlockSpec((1, e, TOK), lambda bi, j, st: (bi, 0, j)),
                      pl.BlockSpec((1, e, TOK), lambda bi, j, st: (bi, 0, j))],
            out_specs=pl.BlockSpec(memory_space=pl.ANY),
            scratch_shapes=[pltpu.VMEM((2, e * win, d + LANES), F32),
                            pltpu.SemaphoreType.DMA((e,)),
                            pltpu.SMEM((1,), I32)]),
        out_shape=jax.ShapeDtypeStruct((b, e, rows_alloc, d + LANES), F32),
        compiler_params=_params("arbitrary", "arbitrary"),
        name="dispatch",
    )(starts, h, slot, aff)


def _ffn_body(st_ref, xs_ref, wg_ref, wu_ref, wd_ref, y_ref, *, nblk):
    e, bi, i = pl.program_id(0), pl.program_id(1), pl.program_id(2)
    tm = xs_ref.shape[2]
    d = wg_ref.shape[1]
    used = st_ref[bi, e, nblk]
    base = i * tm

    @pl.when(base < used)
    def _():
        valid = (base + lax.broadcasted_iota(I32, (tm, 1), 0)) < used
        xa = xs_ref[0, 0]
        x = jnp.where(valid, xa[:, 0:d], 0.0).astype(BF16)
        gate = jnp.where(valid, xa[:, d:d + 1], 0.0)
        a = jnp.dot(x, wg_ref[0], preferred_element_type=F32)
        u = jnp.dot(x, wu_ref[0], preferred_element_type=F32)
        mid = (a * jax.nn.sigmoid(a) * u).astype(BF16)
        y_ref[0, 0] = jnp.dot(mid, wd_ref[0], preferred_element_type=F32) * gate

    @pl.when(base >= used)
    def _():
        y_ref[0, 0] = jnp.zeros(y_ref.shape[2:], F32)


def _ffn(starts, xs, wg, wu, wd, rows, tm, nblk):
    b, e = xs.shape[:2]
    d, f = wg.shape[1:]
    return pl.pallas_call(
        functools.partial(_ffn_body, nblk=nblk),
        grid_spec=pltpu.PrefetchScalarGridSpec(
            num_scalar_prefetch=1, grid=(e, b, rows // tm),
            in_specs=[pl.BlockSpec((1, 1, tm, d + LANES), lambda ei, bi, i, st: (bi, ei, i, 0)),
                      pl.BlockSpec((1, d, f), lambda ei, bi, i, st: (ei, 0, 0)),
                      pl.BlockSpec((1, d, f), lambda ei, bi, i, st: (ei, 0, 0)),
                      pl.BlockSpec((1, f, d), lambda ei, bi, i, st: (ei, 0, 0))],
            out_specs=pl.BlockSpec((1, 1, tm, d), lambda ei, bi, i, st: (bi, ei, i, 0))),
        out_shape=jax.ShapeDtypeStruct((b, e, rows, d), F32),
        compiler_params=_params("arbitrary", "arbitrary", "arbitrary"),
        name="expert_ffn",
    )(starts, xs, wg, wu, wd)


def _combine_body(st_ref, slot_ref, x_ref, gt_ref, gf_ref, y_ref, o_ref, ybuf, sems, *, win, rows, final_norm):
    bi, j = pl.program_id(0), pl.program_id(1)
    d = x_ref.shape[-1]
    st = [st_ref[bi, e, j] for e in range(N_EXPERTS)]
    width = st_ref[bi, 0, j + 1] - st[0]
    for e in range(1, N_EXPERTS):
        width = jnp.maximum(width, st_ref[bi, e, j + 1] - st[e])
    rounds = (width + win - 1) // win

    def one_round(r, acc):
        lo = [st[e] + r * win for e in range(N_EXPERTS)]
        first = [jnp.minimum(lo[e], rows - win) for e in range(N_EXPERTS)]
        cps = [pltpu.make_async_copy(y_ref.at[bi, e, pl.ds(pl.multiple_of(first[e], SUBLANES), win), :],
                                     ybuf.at[pl.ds(e * win, win), :], sems.at[e]) for e in range(N_EXPERTS)]
        for c in cps:
            c.start()
        hot = _onehot_t(slot_ref, first, win)
        hot = [hot[e] & (slot_ref[0, e:e + 1, :] >= lo[e]) for e in range(N_EXPERTS)]
        p = jnp.concatenate(hot, axis=0).astype(BF16)
        for c in cps:
            c.wait()
        y = ybuf[...]
        y_hi = y.astype(BF16)
        y_lo = (y - y_hi.astype(F32)).astype(BF16)
        return (acc + lax.dot_general(p, y_hi, TN_DIMS, preferred_element_type=F32)
                + lax.dot_general(p, y_lo, TN_DIMS, preferred_element_type=F32))

    acc = lax.fori_loop(0, rounds, one_round, jnp.zeros((TOK, d), F32))
    out = x_ref[0] + gt_ref[0] * acc
    if final_norm:
        ms = jnp.mean(out * out, axis=-1, keepdims=True)
        out = out * lax.rsqrt(ms + EPS) * gf_ref[...]
    o_ref[0] = out


def _combine(starts, slot, x, gate, g_final, y, win, final_norm):
    b, s, d = x.shape
    e = N_EXPERTS
    rows = y.shape[2]
    return pl.pallas_call(
        functools.partial(_combine_body, win=win, rows=rows, final_norm=final_norm),
        grid_spec=pltpu.PrefetchScalarGridSpec(
            num_scalar_prefetch=1, grid=(b, s // TOK),
            in_specs=[pl.BlockSpec((1, e, TOK), lambda bi, j, st: (bi, 0, j)),
                      pl.BlockSpec((1, TOK, d), lambda bi, j, st: (bi, j, 0)),
                      pl.BlockSpec((1, 1, d), lambda bi, j, st: (bi, 0, 0)),
                      pl.BlockSpec((1, d), lambda bi, j, st: (0, 0)),
                      pl.BlockSpec(memory_space=pl.ANY)],
            out_specs=pl.BlockSpec((1, TOK, d), lambda bi, j, st: (bi, j, 0)),
            scratch_shapes=[pltpu.VMEM((e * win, d), F32), pltpu.SemaphoreType.DMA((e,))]),
        out_shape=jax.ShapeDtypeStruct((b, s, d), F32),
        compiler_params=_params("arbitrary", "arbitrary"),
        name="combine",
    )(starts, slot, x, gate, g_final, y)


def _moe(x, h, logits_t, gate, g_final, wg, wu, wd, final_norm):
    n = x.shape[1]
    nblk = n // TOK
    cap = CAPACITY_FACTOR * n // N_EXPERTS
    win = min(64, cap)
    rows = cap + SUBLANES * nblk
    tm = min(TOK, rows)
    rows = -(-rows // tm) * tm
    slot, aff, starts = _router(logits_t, cap)
    xs = _dispatch(starts, h, slot, aff, rows + win, win)
    y = _ffn(starts, xs, wg, wu, wd, rows, tm, nblk)
    return _combine(starts, slot, x, gate, g_final, y, win, final_norm)


def _rope_tables(n_tokens, rot_dim):
    rows = n_tokens // GRID_W
    row = jnp.repeat(jnp.arange(rows), GRID_W)
    col = jnp.tile(jnp.arange(GRID_W), rows)
    quarter = rot_dim // 4
    inv_freq = ROPE_THETA ** (-jnp.arange(quarter, dtype=F32) / quarter)
    ang = jnp.concatenate([inv_freq[:, None] * row[None, :], inv_freq[:, None] * col[None, :]], axis=0)
    return jnp.cos(ang), jnp.sin(ang)


def _identity_tables(n_tokens, rot_dim):
    return jnp.ones((rot_dim // 2, n_tokens), F32), jnp.zeros((rot_dim // 2, n_tokens), F32)


def kernel(x, c, ctx, c_ctx, w_mod, b_mod, g_attn, g_ffn, w_in, a_sink, b_q_norm, b_w_uq, b_kv_norm, b_w_ukv, c_lambda, c_subln, d_q_norm, d_k_norm, w_out, w_router, w_gate, w_up, w_down, g_final):
    b, s, d = x.shape
    depth = w_mod.shape[0]
    n_ctx = ctx.shape[1]
    rope64, rope32 = _rope_tables(s, HEAD_DIM), _rope_tables(s, C_QK)
    id64, id32 = _identity_tables(n_ctx, HEAD_DIM), _identity_tables(n_ctx, C_QK)

    cond = jnp.zeros((SUBLANES, d), F32).at[:b].set(c).at[b].set(c_ctx)
    mod = _adaln(cond, w_mod, b_mod)

    col = lambda v: v.reshape(-1, 1)
    xl, xc = x, ctx
    for l in range(depth):
        last = l == depth - 1
        lam_init = 0.8 - 0.6 * math.exp(-0.3 * l)
        m6 = mod[l].reshape(SUBLANES, 6, d)
        lat = [m6[:b, k][:, None, :] for k in range(6)]
        cx = [jnp.broadcast_to(m6[b, k][None, None, :], (b, 1, d)) for k in range(6)]
        w_in_t = w_in[l].T.astype(BF16)
        wuq_t = b_w_uq[l].T.astype(BF16)
        wukv_t = b_w_ukv[l].T.astype(BF16)
        w_out_b = w_out[l].astype(BF16)
        w_router_t = w_router[l].T
        wg, wu, wd = w_gate[l].astype(BF16), w_up[l].astype(BF16), w_down[l].astype(BF16)
        g_a, g_f = g_attn[l][None, :], g_ffn[l][None, :]
        small = (col(b_q_norm[l]), wuq_t, col(b_kv_norm[l]), wukv_t, col(d_q_norm[l]), col(d_k_norm[l]))

        pl_ = _proj(xl, g_a, lat[0], lat[1], w_in_t, rope64, rope32, *small)
        pc_ = _proj(xc, g_a, cx[0], cx[1], w_in_t, id64, id32, *small)
        qa, ka, va, qb, kb, vb, qc, kc, vc, qd, kd, vd = pl_
        qa_c, ka_c, va_c, qb_c, kb_c, vb_c, qc_c, kc_c, vc_c, qd_c, kd_c, vd_c = pc_

        zero = lambda h: 0
        oa = _window_attn(qa, ka_c, va_c, ka, va, a_sink[l])
        ob = _dense_attn(qb, kb_c, vb_c, kb, vb, lambda h: h, lambda h: h)
        oc = _dense_attn(qc, kc_c, vc_c, kc, vc, lambda h: h // 4, lambda h: h // 2)
        od = _dense_attn(qd, kd_c, vd_c, kd, vd, zero, lambda h: h // 2)
        outproj = functools.partial(_outproj, lam_p=c_lambda[l], subln=col(c_subln[l]), w_out=w_out_b,
                                    g_ffn=g_f, w_router_t=w_router_t, lam_init=lam_init)
        xl, hl, lg = outproj(oa, ob, oc, od, x=xl, gate=lat[2], shift=lat[3], scale=lat[4])
        xl = _moe(xl, hl, lg, lat[5], g_final[None, :], wg, wu, wd, final_norm=last)
        if not last:
            oa = _dense_attn(qa_c, ka_c, va_c, None, None, zero, lambda h: h // 2, sink=a_sink[l])
            ob = _dense_attn(qb_c, kb_c, vb_c, None, None, lambda h: h, lambda h: h)
            oc = _dense_attn(qc_c, kc_c, vc_c, None, None, lambda h: h // 4, lambda h: h // 2)
            od = _dense_attn(qd_c, kd_c, vd_c, None, None, zero, lambda h: h // 2)
            xc, hc, lgc = outproj(oa, ob, oc, od, x=xc, gate=cx[2], shift=cx[3], scale=cx[4])
            xc = _moe(xc, hc, lgc, cx[5], g_final[None, :], wg, wu, wd, final_norm=False)
    return xl
```

```python
import functools
import math

import jax
import jax.numpy as jnp
from jax import lax
from jax.experimental import pallas as pl
from jax.experimental.pallas import tpu as pltpu

F32, BF16, I32 = jnp.float32, jnp.bfloat16, jnp.int32
HIGHEST = lax.Precision.HIGHEST

SUBLANES = 8
LANES = 128

GRID_W = 64
HEAD_DIM = 64
WINDOW = 128
EPS = 1e-6
ROPE_THETA = 10000.0
NEG_INF = -1e30
A_HEADS, A_KV = 4, 2
B_HEADS, B_Q_RANK, B_KV_RANK, B_NOPE, B_ROPE, B_V = 4, 192, 128, 64, 32, 64
C_HEADS, C_QK, C_V = 4, 32, 64
D_HEADS, D_KV = 4, 2
N_EXPERTS = 16
CAPACITY_FACTOR = 2

A_COLS = A_HEADS * HEAD_DIM + 2 * A_KV * HEAD_DIM
B_COLS = B_Q_RANK + B_KV_RANK + B_ROPE
C_COLS = 4 * C_HEADS * C_QK + C_HEADS * C_V
D_COLS = D_HEADS * HEAD_DIM + 2 * D_KV * HEAD_DIM
A_OFF, B_OFF, C_OFF, D_OFF = 0, A_COLS, A_COLS + B_COLS, A_COLS + B_COLS + C_COLS

BISECT_STEPS = 152
TOK = 256
QPAD = 128
VROWS = HEAD_DIM + 16
TQ = 1024
LOG2E = math.log2(math.e)

NT_DIMS = (((1,), (1,)), ((), ()))
TN_DIMS = (((0,), (0,)), ((), ()))


def _params(*sem):
    return pltpu.CompilerParams(dimension_semantics=sem, vmem_limit_bytes=56 * 1024 * 1024)


def _adaln_body(c_ref, w_ref, b_ref, o_ref):
    c = c_ref[...]
    s = c * jax.nn.sigmoid(c)
    o_ref[0] = jnp.dot(s, w_ref[0], precision=HIGHEST, preferred_element_type=F32) + b_ref[0]


def _adaln(cond, w_mod, b_mod):
    depth, d, n = w_mod.shape
    tn = n // 4
    return pl.pallas_call(
        _adaln_body,
        grid=(depth, n // tn),
        in_specs=[pl.BlockSpec((SUBLANES, d), lambda l, j: (0, 0)),
                  pl.BlockSpec((1, d, tn), lambda l, j: (l, 0, j)),
                  pl.BlockSpec((1, 1, tn), lambda l, j: (l, 0, j))],
        out_specs=pl.BlockSpec((1, SUBLANES, tn), lambda l, j: (l, 0, j)),
        out_shape=jax.ShapeDtypeStruct((depth, SUBLANES, n), F32),
        compiler_params=_params("arbitrary", "arbitrary"),
        name="adaln",
    )(cond, w_mod, b_mod.reshape(depth, 1, n))


def _rope_t(xt, cos, sin):
    half = xt.shape[0] // 2
    x1, x2 = xt[:half], xt[half:]
    return jnp.concatenate([x1 * cos - x2 * sin, x1 * sin + x2 * cos], axis=0)


def _rms_t(xt, g):
    ms = jnp.mean(xt * xt, axis=0, keepdims=True)
    return xt * lax.rsqrt(ms + EPS) * g


def _modulate(x, g, shift, scale):
    ms = jnp.mean(x * x, axis=-1, keepdims=True)
    return x * lax.rsqrt(ms + EPS) * g * (1.0 + scale) + shift


def _proj_body(x_ref, g_ref, sh_ref, sc_ref, w_ref, c64_ref, s64_ref, c32_ref, s32_ref,
               bqn_ref, wuq_ref, bkvn_ref, wukv_ref, dqn_ref, dkn_ref,
               qa_ref, ka_ref, va_ref, qb_ref, kb_ref, vb_ref,
               qc_ref, kc_ref, vc_ref, qd_ref, kd_ref, vd_ref):
    tm = x_ref.shape[1]
    h = _modulate(x_ref[0], g_ref[...], sh_ref[0], sc_ref[0]).astype(BF16)
    pt = lax.dot_general(w_ref[...], h, NT_DIMS, preferred_element_type=F32)
    c64, s64, c32, s32 = c64_ref[...], s64_ref[...], c32_ref[...], s32_ref[...]
    z64 = jnp.zeros((64, tm), F32)
    z32 = jnp.zeros((32, tm), F32)
    scale64 = HEAD_DIM ** -0.5 * LOG2E
    scale_b = (B_NOPE + B_ROPE) ** -0.5 * LOG2E
    scale_c = C_QK ** -0.5 * LOG2E
    ones = jnp.ones((VROWS - HEAD_DIM, tm), BF16)

    def put_values(v_ref, hh, vt):
        v_ref[0, 0, VROWS * hh: VROWS * hh + HEAD_DIM] = vt.astype(BF16)
        v_ref[0, 0, VROWS * hh + HEAD_DIM: VROWS * hh + VROWS] = ones

    def gqa(off, q_ref, k_ref, v_ref, qn, kn):
        for hh in range(4):
            q = pt[off + 64 * hh: off + 64 * hh + 64]
            if qn is not None:
                q = _rms_t(q, qn)
            q = _rope_t(q, c64, s64) * scale64
            q_ref[0, hh] = (jnp.concatenate([q, z64], axis=0) if hh // 2 == 0
                            else jnp.concatenate([z64, q], axis=0)).astype(BF16)
        ks = []
        for j in range(2):
            k = pt[off + 256 + 64 * j: off + 256 + 64 * j + 64]
            if kn is not None:
                k = _rms_t(k, kn)
            ks.append(_rope_t(k, c64, s64))
        k_ref[0] = jnp.concatenate(ks, axis=0).T.astype(BF16)
        for j in range(2):
            put_values(v_ref, j, pt[off + 384 + 64 * j: off + 384 + 64 * j + 64])

    gqa(A_OFF, qa_ref, ka_ref, va_ref, None, None)
    gqa(D_OFF, qd_ref, kd_ref, vd_ref, dqn_ref[...], dkn_ref[...])

    cq = _rms_t(pt[B_OFF: B_OFF + B_Q_RANK], bqn_ref[...]).astype(BF16)
    ckv = _rms_t(pt[B_OFF + B_Q_RANK: B_OFF + B_Q_RANK + B_KV_RANK], bkvn_ref[...]).astype(BF16)
    kr = _rope_t(pt[B_OFF + B_Q_RANK + B_KV_RANK: B_OFF + B_COLS], c32, s32)
    qt = jnp.dot(wuq_ref[...], cq, preferred_element_type=F32)
    kvt = jnp.dot(wukv_ref[...], ckv, preferred_element_type=F32)
    kparts = []
    dq = B_NOPE + B_ROPE
    for hh in range(B_HEADS):
        qn_ = qt[dq * hh: dq * hh + B_NOPE]
        qr = _rope_t(qt[dq * hh + B_NOPE: dq * hh + dq], c32, s32)
        qb_ref[0, hh] = (jnp.concatenate([qn_, qr, z32], axis=0) * scale_b).astype(BF16)
        kparts.append(jnp.concatenate([kvt[128 * hh: 128 * hh + B_NOPE], kr, z32], axis=0))
        put_values(vb_ref, hh, kvt[128 * hh + B_NOPE: 128 * hh + 128])
    kb_ref[0] = jnp.concatenate(kparts, axis=0).T.astype(BF16)

    for j in range(2 * C_HEADS):
        q = _rope_t(pt[C_OFF + 32 * j: C_OFF + 32 * j + 32], c32, s32) * scale_c
        pieces = [z32, z32, z32, z32]
        pieces[j % 4] = q
        qc_ref[0, j] = jnp.concatenate(pieces, axis=0).astype(BF16)
    kc = [_rope_t(pt[C_OFF + 256 + 32 * j: C_OFF + 256 + 32 * j + 32], c32, s32) for j in range(2 * C_HEADS)]
    kc_ref[0] = jnp.concatenate(kc, axis=0).T.astype(BF16)
    for hh in range(C_HEADS):
        put_values(vc_ref, hh, pt[C_OFF + 512 + 64 * hh: C_OFF + 512 + 64 * hh + 64])


def _proj(x, g, shift, scale, w_in_t, rope64, rope32, bqn, wuq_t, bkvn, wukv_t, dqn, dkn):
    b, s, d = x.shape
    tm = TOK
    n = s // tm
    full = lambda a: pl.BlockSpec(a.shape, lambda bi, i: (0,) * a.ndim)
    q_spec = lambda nh: pl.BlockSpec((1, nh, QPAD, tm), lambda bi, i: (bi, 0, 0, i))
    k_spec = lambda w: pl.BlockSpec((1, tm, w), lambda bi, i: (bi, i, 0))
    v_spec = lambda r: pl.BlockSpec((1, 1, r, tm), lambda bi, i: (bi, i, 0, 0))
    q_shape = lambda nh: jax.ShapeDtypeStruct((b, nh, QPAD, s), BF16)
    k_shape = lambda w: jax.ShapeDtypeStruct((b, s, w), BF16)
    v_shape = lambda r: jax.ShapeDtypeStruct((b, n, r, tm), BF16)
    tab = lambda t: pl.BlockSpec((t.shape[0], tm), lambda bi, i: (0, i))
    c64, s64 = rope64
    c32, s32 = rope32
    return pl.pallas_call(
        _proj_body,
        grid=(b, n),
        in_specs=[pl.BlockSpec((1, tm, d), lambda bi, i: (bi, i, 0)), full(g),
                  pl.BlockSpec((1, 1, d), lambda bi, i: (bi, 0, 0)),
                  pl.BlockSpec((1, 1, d), lambda bi, i: (bi, 0, 0)),
                  full(w_in_t), tab(c64), tab(s64), tab(c32), tab(s32),
                  full(bqn), full(wuq_t), full(bkvn), full(wukv_t), full(dqn), full(dkn)],
        out_specs=[q_spec(4), k_spec(128), v_spec(2 * VROWS),
                   q_spec(4), k_spec(512), v_spec(4 * VROWS),
                   q_spec(8), k_spec(256), v_spec(4 * VROWS),
                   q_spec(4), k_spec(128), v_spec(2 * VROWS)],
        out_shape=[q_shape(4), k_shape(128), v_shape(2 * VROWS),
                   q_shape(4), k_shape(512), v_shape(4 * VROWS),
                   q_shape(8), k_shape(256), v_shape(4 * VROWS),
                   q_shape(4), k_shape(128), v_shape(2 * VROWS)],
        compiler_params=_params("arbitrary", "arbitrary"),
        name="proj",
    )(x, g, shift, scale, w_in_t, c64, s64, c32, s32, bqn, wuq_t, bkvn, wukv_t, dqn, dkn)


def _dense_body(*refs, has_lat, has_sink, n_lat):
    refs = list(refs)
    sink_ref = refs.pop(0) if has_sink else None
    q_ref, kc_ref, vc_ref = refs[:3]
    kl_ref, vl_ref = (refs[3], refs[4]) if has_lat else (None, None)
    o_ref, s_even, s_odd, acc_ref = refs[-4:]
    tq = q_ref.shape[-1]
    qt = q_ref[0, 0]

    def produce(kblk, s_ref):
        s = jnp.dot(kblk, qt, preferred_element_type=F32)
        s_ref[...] = s
        return jnp.max(s, axis=0, keepdims=True)

    def consume(s_ref, mx, vblk, m):
        m_new = jnp.maximum(m, mx)
        p = jnp.exp2(s_ref[...] - m_new).astype(BF16)
        acc_ref[...] =jnp.exp2(m - m_new) * acc_ref[...] + jnp.dot(vblk, p, preferred_element_type=F32)
        return m_new

    def lat_keys(i):
        return kl_ref[0, pl.ds(pl.multiple_of(i * TOK, TOK), TOK), :]

    acc_ref[...] = jnp.zeros(acc_ref.shape, F32)
    mx_c = produce(kc_ref[0], s_odd)
    if has_lat:
        mx_e = produce(lat_keys(0), s_even)
    m = consume(s_odd, mx_c, vc_ref[0, 0], jnp.full((1, tq), NEG_INF, F32))
    if has_lat:
        def body(j, carry):
            m, mx_e = carry
            mx_o = produce(lat_keys(2 * j + 1), s_odd)
            m = consume(s_even, mx_e, vl_ref[0, 2 * j], m)
            mx_e = produce(lat_keys(jnp.minimum(2 * j + 2, n_lat - 1)), s_even)
            m = consume(s_odd, mx_o, vl_ref[0, 2 * j + 1], m)
            return m, mx_e
        m, _ = lax.fori_loop(0, n_lat // 2, body, (m, mx_e), unroll=4)
    acc = acc_ref[...]
    num, den = acc[:HEAD_DIM], acc[HEAD_DIM:HEAD_DIM + 1]
    if has_sink:
        sk = sink_ref[pl.program_id(1)] * LOG2E
        m2 = jnp.maximum(m, sk)
        a = jnp.exp2(m - m2)
        den = den * a + jnp.exp2(sk - m2)
        num = num * a
    o_ref[0, 0] = num / den


def _dense_attn(q, k_ctx, v_ctx, k_lat, v_lat, kgroup, vhead, sink=None):
    b, nh, _, s = q.shape
    tq = min(TQ, s)
    has_lat = k_lat is not None
    has_sink = sink is not None
    sc = k_ctx.shape[1]
    in_specs = [pl.BlockSpec((1, 1, QPAD, tq), lambda bi, h, i: (bi, h, 0, i)),
                pl.BlockSpec((1, sc, LANES), lambda bi, h, i: (bi, 0, kgroup(h))),
                pl.BlockSpec((1, 1, VROWS, sc), lambda bi, h, i: (bi, 0, vhead(h), 0))]
    args = [q, k_ctx, v_ctx]
    n_lat = 0
    if has_lat:
        sl = k_lat.shape[1]
        n_lat = sl // TOK
        assert n_lat % 2 == 0
        in_specs += [pl.BlockSpec((1, sl, LANES), lambda bi, h, i: (bi, 0, kgroup(h))),
                     pl.BlockSpec((1, n_lat, VROWS, TOK), lambda bi, h, i: (bi, 0, vhead(h), 0))]
        args += [k_lat, v_lat]
    if has_sink:
        in_specs = [pl.BlockSpec(memory_space=pltpu.SMEM)] + in_specs
        args = [sink] + args
    return pl.pallas_call(
        functools.partial(_dense_body, has_lat=has_lat, has_sink=has_sink, n_lat=n_lat),
        grid=(b, nh, s // tq),
        in_specs=in_specs,
        out_specs=pl.BlockSpec((1, 1, HEAD_DIM, tq), lambda bi, h, i: (bi, h, 0, i)),
        out_shape=jax.ShapeDtypeStruct((b, nh, HEAD_DIM, s), F32),
        scratch_shapes=[pltpu.VMEM((TOK, tq), F32), pltpu.VMEM((TOK, tq), F32), pltpu.VMEM((VROWS, tq), F32)],
        compiler_params=_params("arbitrary", "arbitrary", "arbitrary"),
        name="dense_attn",
    )(*args)


def _window_body(sink_ref, q_ref, kc_ref, vc_ref, kp_ref, kq_ref, kn_ref, vp_ref, vq_ref, vn_ref, o_ref):
    i = pl.program_id(1)
    n = pl.num_programs(1)
    tq = q_ref.shape[-1]
    key = lax.broadcasted_iota(I32, (TOK, tq), 0)
    qry = lax.broadcasted_iota(I32, (TOK, tq), 1)
    d = key - qry
    ok_prev = (d >= TOK - WINDOW) & (i > 0)
    ok_cur = jnp.abs(d) <= WINDOW
    ok_next = (d <= WINDOW - TOK) & (i < n - 1)
    for hh in range(A_HEADS):
        qt = q_ref[0, hh]
        rows = slice(VROWS * (hh // 2), VROWS * (hh // 2) + VROWS)
        sk = sink_ref[hh] * LOG2E
        s_c = jnp.dot(kc_ref[0], qt, preferred_element_type=F32)
        s_p = jnp.where(ok_prev, jnp.dot(kp_ref[0], qt, preferred_element_type=F32), NEG_INF)
        s_q = jnp.where(ok_cur, jnp.dot(kq_ref[0], qt, preferred_element_type=F32), NEG_INF)
        s_n = jnp.where(ok_next, jnp.dot(kn_ref[0], qt, preferred_element_type=F32), NEG_INF)
        m = jnp.maximum(jnp.maximum(jnp.max(s_c, axis=0, keepdims=True), jnp.max(s_p, axis=0, keepdims=True)),
                        jnp.maximum(jnp.max(s_q, axis=0, keepdims=True), jnp.max(s_n, axis=0, keepdims=True)))
        m = jnp.maximum(m, sk)
        acc = jnp.zeros((VROWS, tq), F32)
        for s_, v_ref in ((s_c, vc_ref), (s_p, vp_ref), (s_q, vq_ref), (s_n, vn_ref)):
            p = jnp.exp2(s_ - m).astype(BF16)
            acc = acc + jnp.dot(v_ref[0, 0, rows, :], p, preferred_element_type=F32)
        o_ref[0, hh] = acc[:HEAD_DIM] / (acc[HEAD_DIM:HEAD_DIM + 1] + jnp.exp2(sk - m))


def _window_attn(q, k_ctx, v_ctx, k_lat, v_lat, sink):
    b, nh, _, s = q.shape
    n = s // TOK
    sc = k_ctx.shape[1]
    prev = lambda i: jnp.maximum(i - 1, 0)
    nxt = lambda i: jnp.minimum(i + 1, n - 1)
    kspec = lambda f: pl.BlockSpec((1, TOK, LANES), lambda bi, i: (bi, f(i), 0))
    vspec = lambda f: pl.BlockSpec((1, 1, 2 * VROWS, TOK), lambda bi, i: (bi, f(i), 0, 0))
    same = lambda i: i
    return pl.pallas_call(
        _window_body,
        grid=(b, n),
        in_specs=[pl.BlockSpec(memory_space=pltpu.SMEM),
                  pl.BlockSpec((1, nh, QPAD, TOK), lambda bi, i: (bi, 0, 0, i)),
                  pl.BlockSpec((1, sc, LANES), lambda bi, i: (bi, 0, 0)),
                  pl.BlockSpec((1, 1, 2 * VROWS, sc), lambda bi, i: (bi, 0, 0, 0)),
                  kspec(prev), kspec(same), kspec(nxt), vspec(prev), vspec(same), vspec(nxt)],
        out_specs=pl.BlockSpec((1, nh, HEAD_DIM, TOK), lambda bi, i: (bi, 0, 0, i)),
        out_shape=jax.ShapeDtypeStruct((b, nh, HEAD_DIM, s), F32),
        compiler_params=_params("arbitrary", "arbitrary"),
        name="window_attn",
    )(sink, q, k_ctx, v_ctx, k_lat, k_lat, k_lat, v_lat, v_lat, v_lat)


def _outproj_body(oa_ref, ob_ref, oc_ref, od_ref, lam_ref, subln_ref, w_ref, x_ref, gt_ref,
                  g_ref, sh_ref, sc_ref, wr_ref, xo_ref, h_ref, lg_ref, *, lam_init):
    lp = lam_ref[...]
    lam = (jnp.exp(jnp.sum(lp[0:1] * lp[1:2], axis=1, keepdims=True))
           - jnp.exp(jnp.sum(lp[2:3] * lp[3:4], axis=1, keepdims=True)) + lam_init)
    parts = [oa_ref[0, hh] for hh in range(4)] + [ob_ref[0, hh] for hh in range(4)]
    for hh in range(C_HEADS):
        o = oc_ref[0, 2 * hh] - lam * oc_ref[0, 2 * hh + 1]
        parts.append(_rms_t(o, subln_ref[...]) * (1.0 - lam_init))
    parts += [od_ref[0, hh] for hh in range(4)]
    ot = jnp.concatenate(parts, axis=0).astype(BF16)
    out = lax.dot_general(ot, w_ref[...], TN_DIMS, preferred_element_type=F32)
    xn = x_ref[0] + gt_ref[0] * out
    xo_ref[0] = xn
    h = _modulate(xn, g_ref[...], sh_ref[0], sc_ref[0])
    h_ref[0] = h.astype(BF16)
    lg_ref[0] = lax.dot_general(wr_ref[...], h, NT_DIMS, precision=HIGHEST, preferred_element_type=F32)


def _outproj(oa, ob, oc, od, lam_p, subln, w_out, x, gate, g_ffn, shift, scale, w_router_t, lam_init):
    b, s, d = x.shape
    tm = min(2 * TOK, s)
    ospec = lambda nh: pl.BlockSpec((1, nh, HEAD_DIM, tm), lambda bi, i: (bi, 0, 0, i))
    full = lambda a: pl.BlockSpec(a.shape, lambda bi, i: (0,) * a.ndim)
    row = pl.BlockSpec((1, 1, d), lambda bi, i: (bi, 0, 0))
    return pl.pallas_call(
        functools.partial(_outproj_body, lam_init=lam_init),
        grid=(b, s // tm),
        in_specs=[ospec(4), ospec(4), ospec(8), ospec(4), full(lam_p), full(subln), full(w_out),
                  pl.BlockSpec((1, tm, d), lambda bi, i: (bi, i, 0)), row, full(g_ffn), row, row,
                  full(w_router_t)],
        out_specs=[pl.BlockSpec((1, tm, d), lambda bi, i: (bi, i, 0)),
                   pl.BlockSpec((1, tm, d), lambda bi, i: (bi, i, 0)),
                   pl.BlockSpec((1, N_EXPERTS, tm), lambda bi, i: (bi, 0, i))],
        out_shape=[jax.ShapeDtypeStruct((b, s, d), F32), jax.ShapeDtypeStruct((b, s, d), BF16),
                   jax.ShapeDtypeStruct((b, N_EXPERTS, s), F32)],
        compiler_params=_params("arbitrary", "arbitrary"),
        name="outproj",
    )(oa, ob, oc, od, lam_p, subln, w_out, x, gate, g_ffn, shift, scale, w_router_t)


def _router_body(lg_ref, slot_ref, aff_ref, st_ref, *, cap, nblk):
    lg = lg_ref[0]
    ex = jnp.exp(lg - jnp.max(lg, axis=0, keepdims=True))
    aff = ex / jnp.sum(ex, axis=0, keepdims=True)
    aff_ref[0] = aff

    def search(_, bounds):
        lo, hi = bounds
        mid = (lo + hi) * 0.5
        enough = jnp.sum((aff >= mid).astype(I32), axis=1, keepdims=True) >= cap
        return jnp.where(enough, mid, lo), jnp.where(enough, hi, mid)

    lo, hi = lax.fori_loop(0, BISECT_STEPS, search,
                           (jnp.zeros((N_EXPERTS, 1), F32), jnp.full((N_EXPERTS, 1), 2.0, F32)))
    gt = aff >= hi
    eq = (aff >= lo) & jnp.logical_not(gt)
    need = (cap - jnp.sum(gt.astype(I32), axis=1, keepdims=True)).astype(F32)
    tri = (lax.broadcasted_iota(I32, (TOK, TOK), 0) <= lax.broadcasted_iota(I32, (TOK, TOK), 1)).astype(BF16)
    lane = lax.broadcasted_iota(I32, (N_EXPERTS, LANES), 1)
    eq_seen = jnp.zeros((N_EXPERTS, 1), F32)
    base = jnp.zeros((N_EXPERTS, 1), I32)
    starts = jnp.zeros((N_EXPERTS, LANES), I32)
    for j in range(nblk):
        cols = slice(TOK * j, TOK * j + TOK)
        eq_c = eq[:, cols]
        eq_cum = jnp.dot(eq_c.astype(BF16), tri, preferred_element_type=F32) + eq_seen
        eq_seen = eq_seen + jnp.sum(eq_c.astype(F32), axis=1, keepdims=True)
        sel = gt[:, cols] | (eq_c & (eq_cum <= need))
        cum = jnp.dot(sel.astype(BF16), tri, preferred_element_type=F32).astype(I32)
        slot_ref[0, :, cols] = jnp.where(sel, base + cum - 1, -1)
        starts = jnp.where(lane == j, base, starts)
        cnt = jnp.sum(sel.astype(I32), axis=1, keepdims=True)
        base = base + jnp.bitwise_and(cnt + (SUBLANES - 1), -SUBLANES)
    st_ref[0] = jnp.where(lane == nblk, base, starts)


def _router(logits_t, cap):
    b, e, s = logits_t.shape
    nblk = s // TOK
    return pl.pallas_call(
        functools.partial(_router_body, cap=cap, nblk=nblk),
        grid=(b,),
        in_specs=[pl.BlockSpec((1, e, s), lambda bi: (bi, 0, 0))],
        out_specs=[pl.BlockSpec((1, e, s), lambda bi: (bi, 0, 0)),
                   pl.BlockSpec((1, e, s), lambda bi: (bi, 0, 0)),
                   pl.BlockSpec((1, e, LANES), lambda bi: (bi, 0, 0))],
        out_shape=[jax.ShapeDtypeStruct((b, e, s), I32), jax.ShapeDtypeStruct((b, e, s), F32),
                   jax.ShapeDtypeStruct((b, e, LANES), I32)],
        compiler_params=_params("arbitrary"),
        name="router",
    )(logits_t)


def _onehot_t(slot_ref, first, win):
    rows = lax.broadcasted_iota(I32, (win, TOK), 0)
    return [rows == (slot_ref[0, e:e + 1, :] - first[e]) for e in range(N_EXPERTS)]


def _dispatch_body(st_ref, h_ref, slot_ref, aff_ref, xs_ref, stage, sems, count, *, win):
    bi, j = pl.program_id(0), pl.program_id(1)
    d = h_ref.shape[-1]

    @pl.when((bi == 0) & (j == 0))
    def _():
        count[0] = 0

    rows_alloc = xs_ref.shape[2]
    last_block = j == pl.num_programs(1) - 1
    st = [st_ref[bi, e, j] for e in range(N_EXPERTS)]
    ends = [jnp.where(last_block, rows_alloc, st_ref[bi, e, j + 1]) for e in range(N_EXPERTS)]
    width = ends[0] - st[0]
    for e in range(1, N_EXPERTS):
        width = jnp.maximum(width, ends[e] - st[e])
    rounds = jnp.maximum((width + win - 1) // win, 1)

    def copies(buf, first):
        return [pltpu.make_async_copy(stage.at[buf, pl.ds(e * win, win), :],
                                      xs_ref.at[bi, e, pl.ds(pl.multiple_of(first[e], SUBLANES), win), :],
                                      sems.at[e]) for e in range(N_EXPERTS)]

    def one_round(r, carry):
        k = count[0]
        buf = k % 2
        first = [jnp.minimum(st[e] + r * win, rows_alloc - win) for e in range(N_EXPERTS)]
        hot = _onehot_t(slot_ref, first, win)
        p = jnp.concatenate(hot, axis=0).astype(BF16)
        stage[buf, :, 0:d] = jnp.dot(p, h_ref[0], preferred_element_type=F32)
        gates = [jnp.sum(jnp.where(hot[e], aff_ref[0, e:e + 1, :], 0.0), axis=1, keepdims=True)
                 for e in range(N_EXPERTS)]
        stage[buf, :, d:d + LANES] = jnp.broadcast_to(jnp.concatenate(gates, axis=0), (N_EXPERTS * win, LANES))

        @pl.when(k > 0)
        def _():
            for c in copies(1 - buf, st):
                c.wait()

        for c in copies(buf, first):
            c.start()
        count[0] = k + 1
        return carry

    lax.fori_loop(0, rounds, one_round, 0)

    @pl.when((bi == pl.num_programs(0) - 1) & (j == pl.num_programs(1) - 1))
    def _():
        for c in copies(0, st):
            c.wait()


def _dispatch(starts, h, slot, aff, rows_alloc, win):
    b, s, d = h.shape
    nblk = s // TOK
    e = N_EXPERTS
    return pl.pallas_call(
        functools.partial(_dispatch_body, win=win),
        grid_spec=pltpu.PrefetchScalarGridSpec(
            num_scalar_prefetch=1, grid=(b, nblk),
            in_specs=[pl.BlockSpec((1, TOK, d), lambda bi, j, st: (bi, j, 0)),
                      pl.BlockSpec((1, e, TOK), lambda bi, j, st: (bi, 0, j)),
                      pl.BlockSpec((1, e, TOK), lambda bi, j, st: (bi, 0, j))],
            out_specs=pl.BlockSpec(memory_space=pl.ANY),
            scratch_shapes=[pltpu.VMEM((2, e * win, d + LANES), F32),
                            pltpu.SemaphoreType.DMA((e,)),
                            pltpu.SMEM((1,), I32)]),
        out_shape=jax.ShapeDtypeStruct((b, e, rows_alloc, d + LANES), F32),
        compiler_params=_params("arbitrary", "arbitrary"),
        name="dispatch",
    )(starts, h, slot, aff)


def _ffn_body(st_ref, xs_ref, wg_in, wu_in, wd_in, y_ref, *bf16_copies, nblk):
    e, bi, i = pl.program_id(0), pl.program_id(1), pl.program_id(2)
    tm = xs_ref.shape[2]
    d = wg_in.shape[1]
    used = st_ref[bi, e, nblk]
    base = i * tm
    if bf16_copies:
        wg_ref, wu_ref, wd_ref = bf16_copies

        @pl.when((bi == 0) & (i == 0))
        def _():
            wg_ref[0] = wg_in[0].astype(BF16)
            wu_ref[0] = wu_in[0].astype(BF16)
            wd_ref[0] = wd_in[0].astype(BF16)
    else:
        wg_ref, wu_ref, wd_ref = wg_in, wu_in, wd_in

    @pl.when(base < used)
    def _():
        valid = (base + lax.broadcasted_iota(I32, (tm, 1), 0)) < used
        xa = xs_ref[0, 0]
        x = jnp.where(valid, xa[:, 0:d], 0.0).astype(BF16)
        gate = jnp.where(valid, xa[:, d:d + 1], 0.0)
        a = jnp.dot(x, wg_ref[0], preferred_element_type=F32)
        u = jnp.dot(x, wu_ref[0], preferred_element_type=F32)
        mid = (a * jax.nn.sigmoid(a) * u).astype(BF16)
        y_ref[0, 0] = jnp.dot(mid, wd_ref[0], preferred_element_type=F32) * gate

    @pl.when(base >= used)
    def _():
        y_ref[0, 0] = jnp.zeros(y_ref.shape[2:], F32)


def _ffn(starts, xs, wg, wu, wd, rows, tm, nblk, layer=None):
    b, e = xs.shape[:2]
    d, f = wg.shape[-2:]
    wspec = lambda r, c: pl.BlockSpec((1, r, c), lambda ei, bi, i, st: (ei, 0, 0))
    out_specs = [pl.BlockSpec((1, 1, tm, d), lambda ei, bi, i, st: (bi, ei, i, 0))]
    out_shape = [jax.ShapeDtypeStruct((b, e, rows, d), F32)]
    if layer is None:
        w_specs = [wspec(d, f), wspec(d, f), wspec(f, d)]
    else:
        lspec = lambda r, c: pl.BlockSpec((None, 1, r, c), lambda ei, bi, i, st: (layer, ei, 0, 0))
        w_specs = [lspec(d, f), lspec(d, f), lspec(f, d)]
        out_specs += [wspec(d, f), wspec(d, f), wspec(f, d)]
        out_shape += [jax.ShapeDtypeStruct(w.shape[1:], BF16) for w in (wg, wu, wd)]
    return pl.pallas_call(
        functools.partial(_ffn_body, nblk=nblk),
        grid_spec=pltpu.PrefetchScalarGridSpec(
            num_scalar_prefetch=1, grid=(e, b, rows // tm),
            in_specs=[pl.BlockSpec((1, 1, tm, d + LANES), lambda ei, bi, i, st: (bi, ei, i, 0))] + w_specs,
            out_specs=out_specs),
        out_shape=out_shape,
        compiler_params=_params("arbitrary", "arbitrary", "arbitrary"),
        name="expert_ffn",
    )(starts, xs, wg, wu, wd)


def _combine_body(st_ref, slot_ref, x_ref, gt_ref, gf_ref, y_ref, o_ref, ybuf, sems, *, win, rows, final_norm):
    bi, j = pl.program_id(0), pl.program_id(1)
    nj = pl.num_programs(1)
    d = x_ref.shape[-1]
    step = bi * nj + j
    cur = step % 2
    st = [st_ref[bi, e, j] for e in range(N_EXPERTS)]
    width = st_ref[bi, 0, j + 1] - st[0]
    for e in range(1, N_EXPERTS):
        width = jnp.maximum(width, st_ref[bi, e, j + 1] - st[e])
    rounds = (width + win - 1) // win

    def window(b_, j_, r):
        lo = [st_ref[b_, e, j_] + r * win for e in range(N_EXPERTS)]
        return lo, [jnp.minimum(lo[e], rows - win) for e in range(N_EXPERTS)]

    def copies(b_, first, buf):
        return [pltpu.make_async_copy(y_ref.at[b_, e, pl.ds(pl.multiple_of(first[e], SUBLANES), win), :],
                                      ybuf.at[buf, pl.ds(e * win, win), :], sems.at[buf, e])
                for e in range(N_EXPERTS)]

    def scatter(lo, first, buf, acc):
        hot = _onehot_t(slot_ref, first, win)
        hot = [hot[e] & (slot_ref[0, e:e + 1, :] >= lo[e]) for e in range(N_EXPERTS)]
        p = jnp.concatenate(hot, axis=0).astype(BF16)
        y = ybuf[buf]
        y_hi = y.astype(BF16)
        y_lo = (y - y_hi.astype(F32)).astype(BF16)
        return (acc + lax.dot_general(p, y_hi, TN_DIMS, preferred_element_type=F32)
                + lax.dot_general(p, y_lo, TN_DIMS, preferred_element_type=F32))

    lo0, first0 = window(bi, j, 0)

    @pl.when(step == 0)
    def _():
        for c in copies(bi, first0, cur):
            c.start()

    @pl.when(step < pl.num_programs(0) * nj - 1)
    def _():
        wrap = j == nj - 1
        b_next, j_next = jnp.where(wrap, bi + 1, bi), jnp.where(wrap, 0, j + 1)
        for c in copies(b_next, window(b_next, j_next, 0)[1], 1 - cur):
            c.start()

    for c in copies(bi, first0, cur):
        c.wait()
    acc = scatter(lo0, first0, cur, jnp.zeros((TOK, d), F32))

    def later_round(r, acc):
        lo, first = window(bi, j, r)
        for c in copies(bi, first, 2):
            c.start()
        for c in copies(bi, first, 2):
            c.wait()
        return scatter(lo, first, 2, acc)

    acc = lax.fori_loop(1, rounds, later_round, acc)
    out = x_ref[0] + gt_ref[0] * acc
    if final_norm:
        ms = jnp.mean(out * out, axis=-1, keepdims=True)
        out = out * lax.rsqrt(ms + EPS) * gf_ref[...]
    o_ref[0] = out


def _combine(starts, slot, x, gate, g_final, y, win, final_norm):
    b, s, d = x.shape
    e = N_EXPERTS
    rows = y.shape[2]
    return pl.pallas_call(
        functools.partial(_combine_body, win=win, rows=rows, final_norm=final_norm),
        grid_spec=pltpu.PrefetchScalarGridSpec(
            num_scalar_prefetch=1, grid=(b, s // TOK),
            in_specs=[pl.BlockSpec((1, e, TOK), lambda bi, j, st: (bi, 0, j)),
                      pl.BlockSpec((1, TOK, d), lambda bi, j, st: (bi, j, 0)),
                      pl.BlockSpec((1, 1, d), lambda bi, j, st: (bi, 0, 0)),
                      pl.BlockSpec((1, d), lambda bi, j, st: (0, 0)),
                      pl.BlockSpec(memory_space=pl.ANY)],
            out_specs=pl.BlockSpec((1, TOK, d), lambda bi, j, st: (bi, j, 0)),
            scratch_shapes=[pltpu.VMEM((3, e * win, d), F32), pltpu.SemaphoreType.DMA((3, e))]),
        out_shape=jax.ShapeDtypeStruct((b, s, d), F32),
        compiler_params=_params("arbitrary", "arbitrary"),
        name="combine",
    )(starts, slot, x, gate, g_final, y)


def _moe(x, h, logits_t, gate, g_final, weights, final_norm, layer=None):
    wg, wu, wd = weights
    n = x.shape[1]
    nblk = n // TOK
    cap = CAPACITY_FACTOR * n // N_EXPERTS
    win = min(64, cap)
    rows = cap + SUBLANES * nblk
    tm = min(TOK, rows)
    rows = -(-rows // tm) * tm
    slot, aff, starts = _router(logits_t, cap)
    xs = _dispatch(starts, h, slot, aff, rows + win, win)
    y, *copies = _ffn(starts, xs, wg, wu, wd, rows, tm, nblk, layer)
    return _combine(starts, slot, x, gate, g_final, y, win, final_norm), copies


def _rope_tables(n_tokens, rot_dim):
    rows = n_tokens // GRID_W
    row = jnp.repeat(jnp.arange(rows), GRID_W)
    col = jnp.tile(jnp.arange(GRID_W), rows)
    quarter = rot_dim // 4
    inv_freq = ROPE_THETA ** (-jnp.arange(quarter, dtype=F32) / quarter)
    ang = jnp.concatenate([inv_freq[:, None] * row[None, :], inv_freq[:, None] * col[None, :]], axis=0)
    return jnp.cos(ang), jnp.sin(ang)


def _identity_tables(n_tokens, rot_dim):
    return jnp.ones((rot_dim // 2, n_tokens), F32), jnp.zeros((rot_dim // 2, n_tokens), F32)


def kernel(x, c, ctx, c_ctx, w_mod, b_mod, g_attn, g_ffn, w_in, a_sink, b_q_norm, b_w_uq, b_kv_norm, b_w_ukv, c_lambda, c_subln, d_q_norm, d_k_norm, w_out, w_router, w_gate, w_up, w_down, g_final):
    b, s, d = x.shape
    depth = w_mod.shape[0]
    n_ctx = ctx.shape[1]
    rope64, rope32 = _rope_tables(s, HEAD_DIM), _rope_tables(s, C_QK)
    id64, id32 = _identity_tables(n_ctx, HEAD_DIM), _identity_tables(n_ctx, C_QK)

    cond = jnp.zeros((SUBLANES, d), F32).at[:b].set(c).at[b].set(c_ctx)
    mod = _adaln(cond, w_mod, b_mod)

    col = lambda v: v.reshape(-1, 1)
    xl, xc = x, ctx
    for l in range(depth):
        last = l == depth - 1
        lam_init = 0.8 - 0.6 * math.exp(-0.3 * l)
        m6 = mod[l].reshape(SUBLANES, 6, d)
        lat = [m6[:b, k][:, None, :] for k in range(6)]
        cx = [jnp.broadcast_to(m6[b, k][None, None, :], (b, 1, d)) for k in range(6)]
        w_in_t = w_in[l].T.astype(BF16)
        wuq_t = b_w_uq[l].T.astype(BF16)
        wukv_t = b_w_ukv[l].T.astype(BF16)
        w_out_b = w_out[l].astype(BF16)
        w_router_t = w_router[l].T
        g_a, g_f = g_attn[l][None, :], g_ffn[l][None, :]
        small = (col(b_q_norm[l]), wuq_t, col(b_kv_norm[l]), wukv_t, col(d_q_norm[l]), col(d_k_norm[l]))

        pl_ = _proj(xl, g_a, lat[0], lat[1], w_in_t, rope64, rope32, *small)
        pc_ = _proj(xc, g_a, cx[0], cx[1], w_in_t, id64, id32, *small)
        qa, ka, va, qb, kb, vb, qc, kc, vc, qd, kd, vd = pl_
        qa_c, ka_c, va_c, qb_c, kb_c, vb_c, qc_c, kc_c, vc_c, qd_c, kd_c, vd_c = pc_

        zero = lambda h: 0
        oa = _window_attn(qa, ka_c, va_c, ka, va, a_sink[l])
        ob = _dense_attn(qb, kb_c, vb_c, kb, vb, lambda h: h, lambda h: h)
        oc = _dense_attn(qc, kc_c, vc_c, kc, vc, lambda h: h // 4, lambda h: h // 2)
        od = _dense_attn(qd, kd_c, vd_c, kd, vd, zero, lambda h: h // 2)
        outproj = functools.partial(_outproj, lam_p=c_lambda[l], subln=col(c_subln[l]), w_out=w_out_b,
                                    g_ffn=g_f, w_router_t=w_router_t, lam_init=lam_init)
        xl, hl, lg = outproj(oa, ob, oc, od, x=xl, gate=lat[2], shift=lat[3], scale=lat[4])
        xl, experts_bf16 = _moe(xl, hl, lg, lat[5], g_final[None, :], (w_gate, w_up, w_down), final_norm=last, layer=l)
        if not last:
            oa = _dense_attn(qa_c, ka_c, va_c, None, None, zero, lambda h: h // 2, sink=a_sink[l])
            ob = _dense_attn(qb_c, kb_c, vb_c, None, None, lambda h: h, lambda h: h)
            oc = _dense_attn(qc_c, kc_c, vc_c, None, None, lambda h: h // 4, lambda h: h // 2)
            od = _dense_attn(qd_c, kd_c, vd_c, None, None, zero, lambda h: h // 2)
            xc, hc, lgc = outproj(oa, ob, oc, od, x=xc, gate=cx[2], shift=cx[3], scale=cx[4])
            xc, _ = _moe(xc, hc, lgc, cx[5], g_final[None, :], experts_bf16, final_norm=False)
    return xl
```

```python
import functools
import math

import jax
import jax.numpy as jnp
from jax import lax
from jax.experimental import pallas as pl
from jax.experimental.pallas import tpu as pltpu

F32, BF16, I32 = jnp.float32, jnp.bfloat16, jnp.int32
HIGHEST = lax.Precision.HIGHEST

SUBLANES = 8
LANES = 128

GRID_W = 64
HEAD_DIM = 64
WINDOW = 128
EPS = 1e-6
ROPE_THETA = 10000.0
NEG_INF = -1e30
A_HEADS, A_KV = 4, 2
B_HEADS, B_Q_RANK, B_KV_RANK, B_NOPE, B_ROPE, B_V = 4, 192, 128, 64, 32, 64
C_HEADS, C_QK, C_V = 4, 32, 64
D_HEADS, D_KV = 4, 2
N_EXPERTS = 16
CAPACITY_FACTOR = 2

A_COLS = A_HEADS * HEAD_DIM + 2 * A_KV * HEAD_DIM
B_COLS = B_Q_RANK + B_KV_RANK + B_ROPE
C_COLS = 4 * C_HEADS * C_QK + C_HEADS * C_V
D_COLS = D_HEADS * HEAD_DIM + 2 * D_KV * HEAD_DIM
A_OFF, B_OFF, C_OFF, D_OFF = 0, A_COLS, A_COLS + B_COLS, A_COLS + B_COLS + C_COLS

BISECT_STEPS = 152
TOK = 256
QPAD = 128
VROWS = HEAD_DIM + 16
TQ = 1024
LOG2E = math.log2(math.e)

NT_DIMS = (((1,), (1,)), ((), ()))
TN_DIMS = (((0,), (0,)), ((), ()))


def _params(*sem):
    return pltpu.CompilerParams(dimension_semantics=sem, vmem_limit_bytes=56 * 1024 * 1024)


def _adaln_body(c_ref, w_ref, b_ref, o_ref):
    c = c_ref[...]
    s = c * jax.nn.sigmoid(c)
    o_ref[0] = jnp.dot(s, w_ref[0], precision=HIGHEST, preferred_element_type=F32) + b_ref[0]


def _adaln(cond, w_mod, b_mod):
    depth, d, n = w_mod.shape
    tn = n // 4
    return pl.pallas_call(
        _adaln_body,
        grid=(depth, n // tn),
        in_specs=[pl.BlockSpec((SUBLANES, d), lambda l, j: (0, 0)),
                  pl.BlockSpec((1, d, tn), lambda l, j: (l, 0, j)),
                  pl.BlockSpec((1, 1, tn), lambda l, j: (l, 0, j))],
        out_specs=pl.BlockSpec((1, SUBLANES, tn), lambda l, j: (l, 0, j)),
        out_shape=jax.ShapeDtypeStruct((depth, SUBLANES, n), F32),
        compiler_params=_params("arbitrary", "arbitrary"),
        name="adaln",
    )(cond, w_mod, b_mod.reshape(depth, 1, n))


def _rope_t(xt, cos, sin):
    half = xt.shape[0] // 2
    x1, x2 = xt[:half], xt[half:]
    return jnp.concatenate([x1 * cos - x2 * sin, x1 * sin + x2 * cos], axis=0)


def _rms_t(xt, g):
    ms = jnp.mean(xt * xt, axis=0, keepdims=True)
    return xt * lax.rsqrt(ms + EPS) * g


def _modulate(x, g, shift, scale):
    ms = jnp.mean(x * x, axis=-1, keepdims=True)
    return x * lax.rsqrt(ms + EPS) * g * (1.0 + scale) + shift


def _proj_body(x_ref, g_ref, sh_ref, sc_ref, w_ref, c64_ref, s64_ref, c32_ref, s32_ref,
               bqn_ref, wuq_ref, bkvn_ref, wukv_ref, dqn_ref, dkn_ref,
               qa_ref, ka_ref, va_ref, qb_ref, kb_ref, vb_ref,
               qc_ref, kc_ref, vc_ref, qd_ref, kd_ref, vd_ref):
    tm = x_ref.shape[1]
    h = _modulate(x_ref[0], g_ref[...], sh_ref[0], sc_ref[0]).astype(BF16)
    pt = lax.dot_general(w_ref[...], h, NT_DIMS, preferred_element_type=F32)
    c64, s64, c32, s32 = c64_ref[...], s64_ref[...], c32_ref[...], s32_ref[...]
    z64 = jnp.zeros((64, tm), F32)
    z32 = jnp.zeros((32, tm), F32)
    scale64 = HEAD_DIM ** -0.5 * LOG2E
    scale_b = (B_NOPE + B_ROPE) ** -0.5 * LOG2E
    scale_c = C_QK ** -0.5 * LOG2E
    ones = jnp.ones((VROWS - HEAD_DIM, TOK), BF16)

    def put_values(v_ref, hh, vt):
        for c in range(tm // TOK):
            v_ref[0, c, VROWS * hh: VROWS * hh + HEAD_DIM] = vt[:, TOK * c: TOK * c + TOK].astype(BF16)
            v_ref[0, c, VROWS * hh + HEAD_DIM: VROWS * hh + VROWS] = ones

    def gqa(off, q_ref, k_ref, v_ref, qn, kn):
        for hh in range(4):
            q = pt[off + 64 * hh: off + 64 * hh + 64]
            if qn is not None:
                q = _rms_t(q, qn)
            q = _rope_t(q, c64, s64) * scale64
            q_ref[0, hh] = (jnp.concatenate([q, z64], axis=0) if hh // 2 == 0
                            else jnp.concatenate([z64, q], axis=0)).astype(BF16)
        ks = []
        for j in range(2):
            k = pt[off + 256 + 64 * j: off + 256 + 64 * j + 64]
            if kn is not None:
                k = _rms_t(k, kn)
            ks.append(_rope_t(k, c64, s64))
        k_ref[0] = jnp.concatenate(ks, axis=0).T.astype(BF16)
        for j in range(2):
            put_values(v_ref, j, pt[off + 384 + 64 * j: off + 384 + 64 * j + 64])

    gqa(A_OFF, qa_ref, ka_ref, va_ref, None, None)
    gqa(D_OFF, qd_ref, kd_ref, vd_ref, dqn_ref[...], dkn_ref[...])

    cq = _rms_t(pt[B_OFF: B_OFF + B_Q_RANK], bqn_ref[...]).astype(BF16)
    ckv = _rms_t(pt[B_OFF + B_Q_RANK: B_OFF + B_Q_RANK + B_KV_RANK], bkvn_ref[...]).astype(BF16)
    kr = _rope_t(pt[B_OFF + B_Q_RANK + B_KV_RANK: B_OFF + B_COLS], c32, s32)
    qt = jnp.dot(wuq_ref[...], cq, preferred_element_type=F32)
    kvt = jnp.dot(wukv_ref[...], ckv, preferred_element_type=F32)
    kparts = []
    dq = B_NOPE + B_ROPE
    for hh in range(B_HEADS):
        qn_ = qt[dq * hh: dq * hh + B_NOPE]
        qr = _rope_t(qt[dq * hh + B_NOPE: dq * hh + dq], c32, s32)
        qb_ref[0, hh] = (jnp.concatenate([qn_, qr, z32], axis=0) * scale_b).astype(BF16)
        kparts.append(jnp.concatenate([kvt[128 * hh: 128 * hh + B_NOPE], kr, z32], axis=0))
        put_values(vb_ref, hh, kvt[128 * hh + B_NOPE: 128 * hh + 128])
    kb_ref[0] = jnp.concatenate(kparts, axis=0).T.astype(BF16)

    for j in range(2 * C_HEADS):
        q = _rope_t(pt[C_OFF + 32 * j: C_OFF + 32 * j + 32], c32, s32) * scale_c
        pieces = [z32, z32, z32, z32]
        pieces[j % 4] = q
        qc_ref[0, j] = jnp.concatenate(pieces, axis=0).astype(BF16)
    kc = [_rope_t(pt[C_OFF + 256 + 32 * j: C_OFF + 256 + 32 * j + 32], c32, s32) for j in range(2 * C_HEADS)]
    kc_ref[0] = jnp.concatenate(kc, axis=0).T.astype(BF16)
    for hh in range(C_HEADS):
        put_values(vc_ref, hh, pt[C_OFF + 512 + 64 * hh: C_OFF + 512 + 64 * hh + 64])


def _proj(x, g, shift, scale, w_in_t, rope64, rope32, bqn, wuq_t, bkvn, wukv_t, dqn, dkn):
    b, s, d = x.shape
    tm = min(2 * TOK, s)
    n = s // tm
    full = lambda a: pl.BlockSpec(a.shape, lambda bi, i: (0,) * a.ndim)
    q_spec = lambda nh: pl.BlockSpec((1, nh, QPAD, tm), lambda bi, i: (bi, 0, 0, i))
    k_spec = lambda w: pl.BlockSpec((1, tm, w), lambda bi, i: (bi, i, 0))
    v_spec = lambda r: pl.BlockSpec((1, tm // TOK, r, TOK), lambda bi, i: (bi, i, 0, 0))
    q_shape = lambda nh: jax.ShapeDtypeStruct((b, nh, QPAD, s), BF16)
    k_shape = lambda w: jax.ShapeDtypeStruct((b, s, w), BF16)
    v_shape = lambda r: jax.ShapeDtypeStruct((b, s // TOK, r, TOK), BF16)
    tab = lambda t: pl.BlockSpec((t.shape[0], tm), lambda bi, i: (0, i))
    c64, s64 = rope64
    c32, s32 = rope32
    return pl.pallas_call(
        _proj_body,
        grid=(b, n),
        in_specs=[pl.BlockSpec((1, tm, d), lambda bi, i: (bi, i, 0)), full(g),
                  pl.BlockSpec((1, 1, d), lambda bi, i: (bi, 0, 0)),
                  pl.BlockSpec((1, 1, d), lambda bi, i: (bi, 0, 0)),
                  full(w_in_t), tab(c64), tab(s64), tab(c32), tab(s32),
                  full(bqn), full(wuq_t), full(bkvn), full(wukv_t), full(dqn), full(dkn)],
        out_specs=[q_spec(4), k_spec(128), v_spec(2 * VROWS),
                   q_spec(4), k_spec(512), v_spec(4 * VROWS),
                   q_spec(8), k_spec(256), v_spec(4 * VROWS),
                   q_spec(4), k_spec(128), v_spec(2 * VROWS)],
        out_shape=[q_shape(4), k_shape(128), v_shape(2 * VROWS),
                   q_shape(4), k_shape(512), v_shape(4 * VROWS),
                   q_shape(8), k_shape(256), v_shape(4 * VROWS),
                   q_shape(4), k_shape(128), v_shape(2 * VROWS)],
        compiler_params=_params("arbitrary", "arbitrary"),
        name="proj",
    )(x, g, shift, scale, w_in_t, c64, s64, c32, s32, bqn, wuq_t, bkvn, wukv_t, dqn, dkn)


def _dense_body(*refs, has_lat, has_sink, n_lat):
    refs = list(refs)
    sink_ref = refs.pop(0) if has_sink else None
    q_ref, kc_ref, vc_ref = refs[:3]
    kl_ref, vl_ref = (refs[3], refs[4]) if has_lat else (None, None)
    o_ref, s_even, s_odd, acc_ref = refs[-4:]
    tq = q_ref.shape[-1]
    qt = q_ref[0, 0]

    def produce(kblk, s_ref):
        s = jnp.dot(kblk, qt, preferred_element_type=F32)
        s_ref[...] = s
        return jnp.max(s, axis=0, keepdims=True)

    def consume(s_ref, mx, vblk, m):
        m_new = jnp.maximum(m, mx)
        p = jnp.exp2(s_ref[...] - m_new).astype(BF16)
        acc_ref[...] =jnp.exp2(m - m_new) * acc_ref[...] + jnp.dot(vblk, p, preferred_element_type=F32)
        return m_new

    def lat_keys(i):
        return kl_ref[0, pl.ds(pl.multiple_of(i * TOK, TOK), TOK), :]

    acc_ref[...] = jnp.zeros(acc_ref.shape, F32)
    mx_c = produce(kc_ref[0], s_odd)
    if has_lat:
        mx_e = produce(lat_keys(0), s_even)
    m = consume(s_odd, mx_c, vc_ref[0, 0], jnp.full((1, tq), NEG_INF, F32))
    if has_lat:
        def body(j, carry):
            m, mx_e = carry
            mx_o = produce(lat_keys(2 * j + 1), s_odd)
            m = consume(s_even, mx_e, vl_ref[0, 2 * j], m)
            mx_e = produce(lat_keys(jnp.minimum(2 * j + 2, n_lat - 1)), s_even)
            m = consume(s_odd, mx_o, vl_ref[0, 2 * j + 1], m)
            return m, mx_e
        m, _ = lax.fori_loop(0, n_lat // 2, body, (m, mx_e), unroll=8)
    acc = acc_ref[...]
    num, den = acc[:HEAD_DIM], acc[HEAD_DIM:HEAD_DIM + 1]
    if has_sink:
        sk = sink_ref[pl.program_id(1)] * LOG2E
        m2 = jnp.maximum(m, sk)
        a = jnp.exp2(m - m2)
        den = den * a + jnp.exp2(sk - m2)
        num = num * a
    o_ref[0, 0] = num / den


def _dense_attn(q, k_ctx, v_ctx, k_lat, v_lat, kgroup, vhead, sink=None):
    b, nh, _, s = q.shape
    tq = min(TQ, s)
    has_lat = k_lat is not None
    has_sink = sink is not None
    sc = k_ctx.shape[1]
    in_specs = [pl.BlockSpec((1, 1, QPAD, tq), lambda bi, h, i: (bi, h, 0, i)),
                pl.BlockSpec((1, sc, LANES), lambda bi, h, i: (bi, 0, kgroup(h))),
                pl.BlockSpec((1, 1, VROWS, sc), lambda bi, h, i: (bi, 0, vhead(h), 0))]
    args = [q, k_ctx, v_ctx]
    n_lat = 0
    if has_lat:
        sl = k_lat.shape[1]
        n_lat = sl // TOK
        assert n_lat % 2 == 0
        in_specs += [pl.BlockSpec((1, sl, LANES), lambda bi, h, i: (bi, 0, kgroup(h))),
                     pl.BlockSpec((1, n_lat, VROWS, TOK), lambda bi, h, i: (bi, 0, vhead(h), 0))]
        args += [k_lat, v_lat]
    if has_sink:
        in_specs = [pl.BlockSpec(memory_space=pltpu.SMEM)] + in_specs
        args = [sink] + args
    return pl.pallas_call(
        functools.partial(_dense_body, has_lat=has_lat, has_sink=has_sink, n_lat=n_lat),
        grid=(b, nh, s // tq),
        in_specs=in_specs,
        out_specs=pl.BlockSpec((1, 1, HEAD_DIM, tq), lambda bi, h, i: (bi, h, 0, i)),
        out_shape=jax.ShapeDtypeStruct((b, nh, HEAD_DIM, s), F32),
        scratch_shapes=[pltpu.VMEM((TOK, tq), F32), pltpu.VMEM((TOK, tq), F32), pltpu.VMEM((VROWS, tq), F32)],
        compiler_params=_params("arbitrary", "arbitrary", "arbitrary"),
        name="dense_attn",
    )(*args)


def _window_body(sink_ref, q_ref, kc_ref, vc_ref, kp_ref, kq_ref, kn_ref, vp_ref, vq_ref, vn_ref, o_ref,
                 s_even, s_odd, acc_ref):
    i = pl.program_id(1)
    n = pl.num_programs(1)
    nh, tq = q_ref.shape[1], q_ref.shape[-1]
    width = nh * tq
    qt = jnp.concatenate([q_ref[0, hh] for hh in range(nh)], axis=1)
    key = lax.broadcasted_iota(I32, (TOK, width), 0)
    qry = jnp.bitwise_and(lax.broadcasted_iota(I32, (TOK, width), 1), tq - 1)
    dist = key - qry

    def produce(k_ref, ok, s_ref):
        s = jnp.dot(k_ref[0], qt, preferred_element_type=F32)
        if ok is not None:
            s = jnp.where(ok, s, NEG_INF)
        s_ref[...] = s
        return jnp.max(s, axis=0, keepdims=True)

    def consume(s_ref, mx, v_ref, m):
        m_new = jnp.maximum(m, mx)
        p = jnp.exp2(s_ref[...] - m_new).astype(BF16)
        pv = [jnp.dot(v_ref[0, 0, VROWS * g: VROWS * g + VROWS, :], p[:, 2 * tq * g: 2 * tq * g + 2 * tq],
                      preferred_element_type=F32) for g in range(A_KV)]
        acc_ref[...] = jnp.exp2(m - m_new) * acc_ref[...] + jnp.concatenate(pv, axis=1)
        return m_new

    acc_ref[...] = jnp.zeros(acc_ref.shape, F32)
    mx_ctx = produce(kc_ref, None, s_even)
    mx_cur = produce(kq_ref, jnp.abs(dist) <= WINDOW, s_odd)
    m = consume(s_even, mx_ctx, vc_ref, jnp.full((1, width), NEG_INF, F32))
    mx_prev = produce(kp_ref, (dist >= TOK - WINDOW) & (i > 0), s_even)
    m = consume(s_odd, mx_cur, vq_ref, m)
    mx_next = produce(kn_ref, (dist <= WINDOW - TOK) & (i < n - 1), s_odd)
    m = consume(s_even, mx_prev, vp_ref, m)
    m = consume(s_odd, mx_next, vn_ref, m)

    sk = jnp.concatenate([jnp.full((1, tq), sink_ref[hh] * LOG2E, F32) for hh in range(nh)], axis=1)
    m2 = jnp.maximum(m, sk)
    a = jnp.exp2(m - m2)
    acc = acc_ref[...]
    out = acc[:HEAD_DIM] * a / (acc[HEAD_DIM:HEAD_DIM + 1] * a + jnp.exp2(sk - m2))
    for hh in range(nh):
        o_ref[0, hh] = out[:, tq * hh: tq * hh + tq]


def _window_attn(q, k_ctx, v_ctx, k_lat, v_lat, sink):
    b, nh, _, s = q.shape
    n = s // TOK
    sc = k_ctx.shape[1]
    prev = lambda i: jnp.maximum(i - 1, 0)
    nxt = lambda i: jnp.minimum(i + 1, n - 1)
    kspec = lambda f: pl.BlockSpec((1, TOK, LANES), lambda bi, i: (bi, f(i), 0))
    vspec = lambda f: pl.BlockSpec((1, 1, 2 * VROWS, TOK), lambda bi, i: (bi, f(i), 0, 0))
    same = lambda i: i
    return pl.pallas_call(
        _window_body,
        grid=(b, n),
        in_specs=[pl.BlockSpec(memory_space=pltpu.SMEM),
                  pl.BlockSpec((1, nh, QPAD, TOK), lambda bi, i: (bi, 0, 0, i)),
                  pl.BlockSpec((1, sc, LANES), lambda bi, i: (bi, 0, 0)),
                  pl.BlockSpec((1, 1, 2 * VROWS, sc), lambda bi, i: (bi, 0, 0, 0)),
                  kspec(prev), kspec(same), kspec(nxt), vspec(prev), vspec(same), vspec(nxt)],
        out_specs=pl.BlockSpec((1, nh, HEAD_DIM, TOK), lambda bi, i: (bi, 0, 0, i)),
        out_shape=jax.ShapeDtypeStruct((b, nh, HEAD_DIM, s), F32),
        scratch_shapes=[pltpu.VMEM((TOK, nh * TOK), F32), pltpu.VMEM((TOK, nh * TOK), F32),
                        pltpu.VMEM((VROWS, nh * TOK), F32)],
        compiler_params=_params("arbitrary", "arbitrary"),
        name="window_attn",
    )(sink, q, k_ctx, v_ctx, k_lat, k_lat, k_lat, v_lat, v_lat, v_lat)


def _outproj_body(oa_ref, ob_ref, oc_ref, od_ref, lam_ref, subln_ref, w_ref, x_ref, gt_ref,
                  g_ref, sh_ref, sc_ref, wr_ref, xo_ref, h_ref, lg_ref, *, lam_init):
    lp = lam_ref[...]
    lam = (jnp.exp(jnp.sum(lp[0:1] * lp[1:2], axis=1, keepdims=True))
           - jnp.exp(jnp.sum(lp[2:3] * lp[3:4], axis=1, keepdims=True)) + lam_init)
    parts = [oa_ref[0, hh] for hh in range(4)] + [ob_ref[0, hh] for hh in range(4)]
    for hh in range(C_HEADS):
        o = oc_ref[0, 2 * hh] - lam * oc_ref[0, 2 * hh + 1]
        parts.append(_rms_t(o, subln_ref[...]) * (1.0 - lam_init))
    parts += [od_ref[0, hh] for hh in range(4)]
    ot = jnp.concatenate(parts, axis=0).astype(BF16)
    out = lax.dot_general(ot, w_ref[...], TN_DIMS, preferred_element_type=F32)
    xn = x_ref[0] + gt_ref[0] * out
    xo_ref[0] = xn
    h = _modulate(xn, g_ref[...], sh_ref[0], sc_ref[0])
    h_ref[0] = h.astype(BF16)
    lg_ref[0] = lax.dot_general(wr_ref[...], h, NT_DIMS, precision=HIGHEST, preferred_element_type=F32)


def _outproj(oa, ob, oc, od, lam_p, subln, w_out, x, gate, g_ffn, shift, scale, w_router_t, lam_init):
    b, s, d = x.shape
    tm = min(2 * TOK, s)
    ospec = lambda nh: pl.BlockSpec((1, nh, HEAD_DIM, tm), lambda bi, i: (bi, 0, 0, i))
    full = lambda a: pl.BlockSpec(a.shape, lambda bi, i: (0,) * a.ndim)
    row = pl.BlockSpec((1, 1, d), lambda bi, i: (bi, 0, 0))
    return pl.pallas_call(
        functools.partial(_outproj_body, lam_init=lam_init),
        grid=(b, s // tm),
        in_specs=[ospec(4), ospec(4), ospec(8), ospec(4), full(lam_p), full(subln), full(w_out),
                  pl.BlockSpec((1, tm, d), lambda bi, i: (bi, i, 0)), row, full(g_ffn), row, row,
                  full(w_router_t)],
        out_specs=[pl.BlockSpec((1, tm, d), lambda bi, i: (bi, i, 0)),
                   pl.BlockSpec((1, tm, d), lambda bi, i: (bi, i, 0)),
                   pl.BlockSpec((1, N_EXPERTS, tm), lambda bi, i: (bi, 0, i))],
        out_shape=[jax.ShapeDtypeStruct((b, s, d), F32), jax.ShapeDtypeStruct((b, s, d), BF16),
                   jax.ShapeDtypeStruct((b, N_EXPERTS, s), F32)],
        compiler_params=_params("arbitrary", "arbitrary"),
        name="outproj",
    )(oa, ob, oc, od, lam_p, subln, w_out, x, gate, g_ffn, shift, scale, w_router_t)


def _router_body(lg_ref, slot_ref, aff_ref, st_ref, *, cap, nblk):
    lg = lg_ref[0]
    ex = jnp.exp(lg - jnp.max(lg, axis=0, keepdims=True))
    aff = ex / jnp.sum(ex, axis=0, keepdims=True)
    aff_ref[0] = aff

    def search(_, bounds):
        lo, hi = bounds
        mid = (lo + hi) * 0.5
        enough = jnp.sum((aff >= mid).astype(I32), axis=1, keepdims=True) >= cap
        return jnp.where(enough, mid, lo), jnp.where(enough, hi, mid)

    lo, hi = lax.fori_loop(0, BISECT_STEPS, search,
                           (jnp.zeros((N_EXPERTS, 1), F32), jnp.full((N_EXPERTS, 1), 2.0, F32)))
    gt = aff >= hi
    eq = (aff >= lo) & jnp.logical_not(gt)
    need = (cap - jnp.sum(gt.astype(I32), axis=1, keepdims=True)).astype(F32)
    tri = (lax.broadcasted_iota(I32, (TOK, TOK), 0) <= lax.broadcasted_iota(I32, (TOK, TOK), 1)).astype(BF16)
    lane = lax.broadcasted_iota(I32, (N_EXPERTS, LANES), 1)
    eq_seen = jnp.zeros((N_EXPERTS, 1), F32)
    base = jnp.zeros((N_EXPERTS, 1), I32)
    starts = jnp.zeros((N_EXPERTS, LANES), I32)
    for j in range(nblk):
        cols = slice(TOK * j, TOK * j + TOK)
        eq_c = eq[:, cols]
        eq_cum = jnp.dot(eq_c.astype(BF16), tri, preferred_element_type=F32) + eq_seen
        eq_seen = eq_seen + jnp.sum(eq_c.astype(F32), axis=1, keepdims=True)
        sel = gt[:, cols] | (eq_c & (eq_cum <= need))
        cum = jnp.dot(sel.astype(BF16), tri, preferred_element_type=F32).astype(I32)
        slot_ref[0, :, cols] = jnp.where(sel, base + cum - 1, -1)
        starts = jnp.where(lane == j, base, starts)
        cnt = jnp.sum(sel.astype(I32), axis=1, keepdims=True)
        base = base + jnp.bitwise_and(cnt + (SUBLANES - 1), -SUBLANES)
    st_ref[0] = jnp.where(lane == nblk, base, starts)


def _router(logits_t, cap):
    b, e, s = logits_t.shape
    nblk = s // TOK
    return pl.pallas_call(
        functools.partial(_router_body, cap=cap, nblk=nblk),
        grid=(b,),
        in_specs=[pl.BlockSpec((1, e, s), lambda bi: (bi, 0, 0))],
        out_specs=[pl.BlockSpec((1, e, s), lambda bi: (bi, 0, 0)),
                   pl.BlockSpec((1, e, s), lambda bi: (bi, 0, 0)),
                   pl.BlockSpec((1, e, LANES), lambda bi: (bi, 0, 0))],
        out_shape=[jax.ShapeDtypeStruct((b, e, s), I32), jax.ShapeDtypeStruct((b, e, s), F32),
                   jax.ShapeDtypeStruct((b, e, LANES), I32)],
        compiler_params=_params("arbitrary"),
        name="router",
    )(logits_t)


def _onehot_t(slot_ref, first, win):
    rows = lax.broadcasted_iota(I32, (win, TOK), 0)
    return [rows == (slot_ref[0, e:e + 1, :] - first[e]) for e in range(N_EXPERTS)]


def _dispatch_body(st_ref, h_ref, slot_ref, aff_ref, xs_ref, stage, sems, count, *, win):
    bi, j = pl.program_id(0), pl.program_id(1)
    d = h_ref.shape[-1]

    @pl.when((bi == 0) & (j == 0))
    def _():
        count[0] = 0

    rows_alloc = xs_ref.shape[2]
    last_block = j == pl.num_programs(1) - 1
    st = [st_ref[bi, e, j] for e in range(N_EXPERTS)]
    ends = [jnp.where(last_block, rows_alloc, st_ref[bi, e, j + 1]) for e in range(N_EXPERTS)]
    width = ends[0] - st[0]
    for e in range(1, N_EXPERTS):
        width = jnp.maximum(width, ends[e] - st[e])
    rounds = jnp.maximum((width + win - 1) // win, 1)

    def copies(buf, first):
        return [pltpu.make_async_copy(stage.at[buf, pl.ds(e * win, win), :],
                                      xs_ref.at[bi, e, pl.ds(pl.multiple_of(first[e], SUBLANES), win), :],
                                      sems.at[e]) for e in range(N_EXPERTS)]

    def one_round(r, carry):
        k = count[0]
        buf = k % 2
        first = [jnp.minimum(st[e] + r * win, rows_alloc - win) for e in range(N_EXPERTS)]
        hot = _onehot_t(slot_ref, first, win)
        p = jnp.concatenate(hot, axis=0).astype(BF16)
        stage[buf, :, 0:d] = jnp.dot(p, h_ref[0], preferred_element_type=F32)
        gates = [jnp.sum(jnp.where(hot[e], aff_ref[0, e:e + 1, :], 0.0), axis=1, keepdims=True)
                 for e in range(N_EXPERTS)]
        stage[buf, :, d:d + LANES] = jnp.broadcast_to(jnp.concatenate(gates, axis=0), (N_EXPERTS * win, LANES))

        @pl.when(k > 0)
        def _():
            for c in copies(1 - buf, st):
                c.wait()

        for c in copies(buf, first):
            c.start()
        count[0] = k + 1
        return carry

    lax.fori_loop(0, rounds, one_round, 0)

    @pl.when((bi == pl.num_programs(0) - 1) & (j == pl.num_programs(1) - 1))
    def _():
        for c in copies(0, st):
            c.wait()


def _dispatch(starts, h, slot, aff, rows_alloc, win):
    b, s, d = h.shape
    nblk = s // TOK
    e = N_EXPERTS
    return pl.pallas_call(
        functools.partial(_dispatch_body, win=win),
        grid_spec=pltpu.PrefetchScalarGridSpec(
            num_scalar_prefetch=1, grid=(b, nblk),
            in_specs=[pl.BlockSpec((1, TOK, d), lambda bi, j, st: (bi, j, 0)),
                      pl.BlockSpec((1, e, TOK), lambda bi, j, st: (bi, 0, j)),
                      pl.BlockSpec((1, e, TOK), lambda bi, j, st: (bi, 0, j))],
            out_specs=pl.BlockSpec(memory_space=pl.ANY),
            scratch_shapes=[pltpu.VMEM((2, e * win, d + LANES), F32),
                            pltpu.SemaphoreType.DMA((e,)),
                            pltpu.SMEM((1,), I32)]),
        out_shape=jax.ShapeDtypeStruct((b, e, rows_alloc, d + LANES), F32),
        compiler_params=_params("arbitrary", "arbitrary"),
        name="dispatch",
    )(starts, h, slot, aff)


def _ffn_body(st_ref, xs_ref, wg_in, wu_in, wd_in, y_ref, *rest, nblk, layer):
    e, bi, i = pl.program_id(0), pl.program_id(1), pl.program_id(2)
    tm = xs_ref.shape[2]
    d = y_ref.shape[-1]
    used = st_ref[bi, e, nblk]
    base = i * tm
    if layer is None:
        wg_ref, wu_ref, wd_ref = wg_in, wu_in, wd_in
    else:
        wg_ref, wu_ref, wd_ref, bufs, sems = rest[0], rest[1], rest[2], rest[3:6], rest[6]

        def fetch(expert, slot):
            return [pltpu.make_async_copy(w.at[layer, expert], buf.at[slot], sems.at[slot, k])
                    for k, (w, buf) in enumerate(zip((wg_in, wu_in, wd_in), bufs))]

        @pl.when((bi == 0) & (i == 0))
        def _():
            slot = e % 2

            @pl.when(e == 0)
            def _():
                for c in fetch(0, 0):
                    c.start()

            for c in fetch(e, slot):
                c.wait()

            @pl.when(e + 1 < pl.num_programs(0))
            def _():
                for c in fetch(e + 1, 1 - slot):
                    c.start()

            for out, buf in zip((wg_ref, wu_ref, wd_ref), bufs):
                out[0] = buf[slot].astype(BF16)

    def run(n):
        valid = (base + lax.broadcasted_iota(I32, (n, 1), 0)) < used
        xa = xs_ref[0, 0, 0:n, :]
        x = jnp.where(valid, xa[:, 0:d], 0.0).astype(BF16)
        gate = jnp.where(valid, xa[:, d:d + 1], 0.0)
        a = jnp.dot(x, wg_ref[0], preferred_element_type=F32)
        u = jnp.dot(x, wu_ref[0], preferred_element_type=F32)
        mid = (a * jax.nn.sigmoid(a) * u).astype(BF16)
        y_ref[0, 0, 0:n, :] = jnp.dot(mid, wd_ref[0], preferred_element_type=F32) * gate
        if n < tm:
            y_ref[0, 0, n:tm, :] = jnp.zeros((tm - n, d), F32)

    half = tm // 2 if tm % (4 * SUBLANES) == 0 else 0
    left = used - base

    @pl.when(left > half)
    def _():
        run(tm)

    if half:
        @pl.when((left > 0) & (left <= half))
        def _():
            run(half)

    @pl.when(left <= 0)
    def _():
        y_ref[0, 0] = jnp.zeros(y_ref.shape[2:], F32)


def _ffn(starts, xs, wg, wu, wd, rows, tm, nblk, layer=None):
    b, e = xs.shape[:2]
    d, f = wg.shape[-2:]
    wspec = lambda r, c: pl.BlockSpec((1, r, c), lambda ei, bi, i, st: (ei, 0, 0))
    out_specs = [pl.BlockSpec((1, 1, tm, d), lambda ei, bi, i, st: (bi, ei, i, 0))]
    out_shape = [jax.ShapeDtypeStruct((b, e, rows, d), F32)]
    scratch = []
    if layer is None:
        w_specs = [wspec(d, f), wspec(d, f), wspec(f, d)]
    else:
        w_specs = [pl.BlockSpec(memory_space=pl.ANY)] * 3
        out_specs += [wspec(d, f), wspec(d, f), wspec(f, d)]
        out_shape += [jax.ShapeDtypeStruct(w.shape[1:], BF16) for w in (wg, wu, wd)]
        scratch = [pltpu.VMEM((2,) + w.shape[2:], F32) for w in (wg, wu, wd)] + [pltpu.SemaphoreType.DMA((2, 3))]
    return pl.pallas_call(
        functools.partial(_ffn_body, nblk=nblk, layer=layer),
        grid_spec=pltpu.PrefetchScalarGridSpec(
            num_scalar_prefetch=1, grid=(e, b, rows // tm),
            in_specs=[pl.BlockSpec((1, 1, tm, d + LANES), lambda ei, bi, i, st: (bi, ei, i, 0))] + w_specs,
            out_specs=out_specs, scratch_shapes=scratch),
        out_shape=out_shape,
        compiler_params=_params("arbitrary", "arbitrary", "arbitrary"),
        name="expert_ffn",
    )(starts, xs, wg, wu, wd)


def _combine_body(st_ref, slot_ref, x_ref, gt_ref, gf_ref, y_ref, o_ref, ybuf, sems, *, win, rows, final_norm):
    bi, j = pl.program_id(0), pl.program_id(1)
    nj = pl.num_programs(1)
    d = x_ref.shape[-1]
    step = bi * nj + j
    cur = step % 2
    st = [st_ref[bi, e, j] for e in range(N_EXPERTS)]
    width = st_ref[bi, 0, j + 1] - st[0]
    for e in range(1, N_EXPERTS):
        width = jnp.maximum(width, st_ref[bi, e, j + 1] - st[e])
    rounds = (width + win - 1) // win

    def window(b_, j_, r):
        lo = [st_ref[b_, e, j_] + r * win for e in range(N_EXPERTS)]
        return lo, [jnp.minimum(lo[e], rows - win) for e in range(N_EXPERTS)]

    def copies(b_, first, buf):
        return [pltpu.make_async_copy(y_ref.at[b_, e, pl.ds(pl.multiple_of(first[e], SUBLANES), win), :],
                                      ybuf.at[buf, pl.ds(e * win, win), :], sems.at[buf, e])
                for e in range(N_EXPERTS)]

    def scatter(lo, first, buf, acc):
        hot = _onehot_t(slot_ref, first, win)
        hot = [hot[e] & (slot_ref[0, e:e + 1, :] >= lo[e]) for e in range(N_EXPERTS)]
        p = jnp.concatenate(hot, axis=0).astype(BF16)
        y = ybuf[buf]
        y_hi = y.astype(BF16)
        y_lo = (y - y_hi.astype(F32)).astype(BF16)
        return (acc + lax.dot_general(p, y_hi, TN_DIMS, preferred_element_type=F32)
                + lax.dot_general(p, y_lo, TN_DIMS, preferred_element_type=F32))

    lo0, first0 = window(bi, j, 0)

    @pl.when(step == 0)
    def _():
        for c in copies(bi, first0, cur):
            c.start()

    @pl.when(step < pl.num_programs(0) * nj - 1)
    def _():
        wrap = j == nj - 1
        b_next, j_next = jnp.where(wrap, bi + 1, bi), jnp.where(wrap, 0, j + 1)
        for c in copies(b_next, window(b_next, j_next, 0)[1], 1 - cur):
            c.start()

    for c in copies(bi, first0, cur):
        c.wait()
    acc = scatter(lo0, first0, cur, jnp.zeros((TOK, d), F32))

    def later_round(r, acc):
        lo, first = window(bi, j, r)
        for c in copies(bi, first, 2):
            c.start()
        for c in copies(bi, first, 2):
            c.wait()
        return scatter(lo, first, 2, acc)

    acc = lax.fori_loop(1, rounds, later_round, acc)
    out = x_ref[0] + gt_ref[0] * acc
    if final_norm:
        ms = jnp.mean(out * out, axis=-1, keepdims=True)
        out = out * lax.rsqrt(ms + EPS) * gf_ref[...]
    o_ref[0] = out


def _combine(starts, slot, x, gate, g_final, y, win, final_norm):
    b, s, d = x.shape
    e = N_EXPERTS
    rows = y.shape[2]
    return pl.pallas_call(
        functools.partial(_combine_body, win=win, rows=rows, final_norm=final_norm),
        grid_spec=pltpu.PrefetchScalarGridSpec(
            num_scalar_prefetch=1, grid=(b, s // TOK),
            in_specs=[pl.BlockSpec((1, e, TOK), lambda bi, j, st: (bi, 0, j)),
                      pl.BlockSpec((1, TOK, d), lambda bi, j, st: (bi, j, 0)),
                      pl.BlockSpec((1, 1, d), lambda bi, j, st: (bi, 0, 0)),
                      pl.BlockSpec((1, d), lambda bi, j, st: (0, 0)),
                      pl.BlockSpec(memory_space=pl.ANY)],
            out_specs=pl.BlockSpec((1, TOK, d), lambda bi, j, st: (bi, j, 0)),
            scratch_shapes=[pltpu.VMEM((3, e * win, d), F32), pltpu.SemaphoreType.DMA((3, e))]),
        out_shape=jax.ShapeDtypeStruct((b, s, d), F32),
        compiler_params=_params("arbitrary", "arbitrary"),
        name="combine",
    )(starts, slot, x, gate, g_final, y)


def _moe(x, h, logits_t, gate, g_final, weights, final_norm, layer=None):
    wg, wu, wd = weights
    n = x.shape[1]
    nblk = n // TOK
    cap = CAPACITY_FACTOR * n // N_EXPERTS
    win = min(64, cap)
    rows = cap + SUBLANES * nblk
    tm = min(TOK, rows)
    rows = -(-rows // tm) * tm
    slot, aff, starts = _router(logits_t, cap)
    xs = _dispatch(starts, h, slot, aff, rows + win, win)
    y, *copies = _ffn(starts, xs, wg, wu, wd, rows, tm, nblk, layer)
    return _combine(starts, slot, x, gate, g_final, y, win, final_norm), copies


def _rope_tables(n_tokens, rot_dim):
    rows = n_tokens // GRID_W
    row = jnp.repeat(jnp.arange(rows), GRID_W)
    col = jnp.tile(jnp.arange(GRID_W), rows)
    quarter = rot_dim // 4
    inv_freq = ROPE_THETA ** (-jnp.arange(quarter, dtype=F32) / quarter)
    ang = jnp.concatenate([inv_freq[:, None] * row[None, :], inv_freq[:, None] * col[None, :]], axis=0)
    return jnp.cos(ang), jnp.sin(ang)


def _identity_tables(n_tokens, rot_dim):
    return jnp.ones((rot_dim // 2, n_tokens), F32), jnp.zeros((rot_dim // 2, n_tokens), F32)


def kernel(x, c, ctx, c_ctx, w_mod, b_mod, g_attn, g_ffn, w_in, a_sink, b_q_norm, b_w_uq, b_kv_norm, b_w_ukv, c_lambda, c_subln, d_q_norm, d_k_norm, w_out, w_router, w_gate, w_up, w_down, g_final):
    b, s, d = x.shape
    depth = w_mod.shape[0]
    n_ctx = ctx.shape[1]
    rope64, rope32 = _rope_tables(s, HEAD_DIM), _rope_tables(s, C_QK)
    id64, id32 = _identity_tables(n_ctx, HEAD_DIM), _identity_tables(n_ctx, C_QK)

    cond = jnp.zeros((SUBLANES, d), F32).at[:b].set(c).at[b].set(c_ctx)
    mod = _adaln(cond, w_mod, b_mod)

    col = lambda v: v.reshape(-1, 1)
    xl, xc = x, ctx
    for l in range(depth):
        last = l == depth - 1
        lam_init = 0.8 - 0.6 * math.exp(-0.3 * l)
        m6 = mod[l].reshape(SUBLANES, 6, d)
        lat = [m6[:b, k][:, None, :] for k in range(6)]
        cx = [jnp.broadcast_to(m6[b, k][None, None, :], (b, 1, d)) for k in range(6)]
        w_in_t = w_in[l].T.astype(BF16)
        wuq_t = b_w_uq[l].T.astype(BF16)
        wukv_t = b_w_ukv[l].T.astype(BF16)
        w_out_b = w_out[l].astype(BF16)
        w_router_t = w_router[l].T
        g_a, g_f = g_attn[l][None, :], g_ffn[l][None, :]
        small = (col(b_q_norm[l]), wuq_t, col(b_kv_norm[l]), wukv_t, col(d_q_norm[l]), col(d_k_norm[l]))

        pl_ = _proj(xl, g_a, lat[0], lat[1], w_in_t, rope64, rope32, *small)
        pc_ = _proj(xc, g_a, cx[0], cx[1], w_in_t, id64, id32, *small)
        qa, ka, va, qb, kb, vb, qc, kc, vc, qd, kd, vd = pl_
        qa_c, ka_c, va_c, qb_c, kb_c, vb_c, qc_c, kc_c, vc_c, qd_c, kd_c, vd_c = pc_

        zero = lambda h: 0
        oa = _window_attn(qa, ka_c, va_c, ka, va, a_sink[l])
        ob = _dense_attn(qb, kb_c, vb_c, kb, vb, lambda h: h, lambda h: h)
        oc = _dense_attn(qc, kc_c, vc_c, kc, vc, lambda h: h // 4, lambda h: h // 2)
        od = _dense_attn(qd, kd_c, vd_c, kd, vd, zero, lambda h: h // 2)
        outproj = functools.partial(_outproj, lam_p=c_lambda[l], subln=col(c_subln[l]), w_out=w_out_b,
                                    g_ffn=g_f, w_router_t=w_router_t, lam_init=lam_init)
        xl, hl, lg = outproj(oa, ob, oc, od, x=xl, gate=lat[2], shift=lat[3], scale=lat[4])
        xl, experts_bf16 = _moe(xl, hl, lg, lat[5], g_final[None, :], (w_gate, w_up, w_down), final_norm=last, layer=l)
        if not last:
            oa = _dense_attn(qa_c, ka_c, va_c, None, None, zero, lambda h: h // 2, sink=a_sink[l])
            ob = _dense_attn(qb_c, kb_c, vb_c, None, None, lambda h: h, lambda h: h)
            oc = _dense_attn(qc_c, kc_c, vc_c, None, None, lambda h: h // 4, lambda h: h // 2)
            od = _dense_attn(qd_c, kd_c, vd_c, None, None, zero, lambda h: h // 2)
            xc, hc, lgc = outproj(oa, ob, oc, od, x=xc, gate=cx[2], shift=cx[3], scale=cx[4])
            xc, _ = _moe(xc, hc, lgc, cx[5], g_final[None, :], experts_bf16, final_norm=False)
    return xl
```

```python
import functools
import math

import jax
import jax.numpy as jnp
from jax import lax
from jax.experimental import pallas as pl
from jax.experimental.pallas import tpu as pltpu

F32, BF16, I32 = jnp.float32, jnp.bfloat16, jnp.int32
HIGHEST = lax.Precision.HIGHEST

SUBLANES = 8
LANES = 128

GRID_W = 64
HEAD_DIM = 64
WINDOW = 128
EPS = 1e-6
ROPE_THETA = 10000.0
NEG_INF = -1e30
A_HEADS, A_KV = 4, 2
B_HEADS, B_Q_RANK, B_KV_RANK, B_NOPE, B_ROPE, B_V = 4, 192, 128, 64, 32, 64
C_HEADS, C_QK, C_V = 4, 32, 64
D_HEADS, D_KV = 4, 2
N_EXPERTS = 16
CAPACITY_FACTOR = 2

A_COLS = A_HEADS * HEAD_DIM + 2 * A_KV * HEAD_DIM
B_COLS = B_Q_RANK + B_KV_RANK + B_ROPE
C_COLS = 4 * C_HEADS * C_QK + C_HEADS * C_V
D_COLS = D_HEADS * HEAD_DIM + 2 * D_KV * HEAD_DIM
A_OFF, B_OFF, C_OFF, D_OFF = 0, A_COLS, A_COLS + B_COLS, A_COLS + B_COLS + C_COLS

BISECT_STEPS = 152
TOK = 256
QPAD = 128
VROWS = HEAD_DIM + 16
TQ = 1024
ATTN_UNROLL = 8
SLOT_WINDOW = 48
LOG2E = math.log2(math.e)

NT_DIMS = (((1,), (1,)), ((), ()))
TN_DIMS = (((0,), (0,)), ((), ()))


def _params(*sem):
    return pltpu.CompilerParams(dimension_semantics=sem, vmem_limit_bytes=56 * 1024 * 1024)


def _adaln_body(c_ref, w_ref, b_ref, o_ref):
    c = c_ref[...]
    s = c * jax.nn.sigmoid(c)
    o_ref[0] = jnp.dot(s, w_ref[0], precision=HIGHEST, preferred_element_type=F32) + b_ref[0]


def _adaln(cond, w_mod, b_mod):
    depth, d, n = w_mod.shape
    tn = n // 4
    return pl.pallas_call(
        _adaln_body,
        grid=(depth, n // tn),
        in_specs=[pl.BlockSpec((SUBLANES, d), lambda l, j: (0, 0)),
                  pl.BlockSpec((1, d, tn), lambda l, j: (l, 0, j)),
                  pl.BlockSpec((1, 1, tn), lambda l, j: (l, 0, j))],
        out_specs=pl.BlockSpec((1, SUBLANES, tn), lambda l, j: (l, 0, j)),
        out_shape=jax.ShapeDtypeStruct((depth, SUBLANES, n), F32),
        compiler_params=_params("arbitrary", "arbitrary"),
        name="adaln",
    )(cond, w_mod, b_mod.reshape(depth, 1, n))


def _rope_t(xt, cos, sin):
    half = xt.shape[0] // 2
    x1, x2 = xt[:half], xt[half:]
    return jnp.concatenate([x1 * cos - x2 * sin, x1 * sin + x2 * cos], axis=0)


def _rms_t(xt, g):
    ms = jnp.mean(xt * xt, axis=0, keepdims=True)
    return xt * lax.rsqrt(ms + EPS) * g


def _modulate(x, g, shift, scale):
    ms = jnp.mean(x * x, axis=-1, keepdims=True)
    return x * lax.rsqrt(ms + EPS) * g * (1.0 + scale) + shift


def _proj_body(x_ref, g_ref, sh_ref, sc_ref, w_ref, c64_ref, s64_ref, c32_ref, s32_ref,
               bqn_ref, wuq_ref, bkvn_ref, wukv_ref, dqn_ref, dkn_ref,
               qa_ref, ka_ref, va_ref, qb_ref, kb_ref, vb_ref,
               qc_ref, kc_ref, vc_ref, qd_ref, kd_ref, vd_ref):
    tm = x_ref.shape[1]
    h = _modulate(x_ref[0], g_ref[...], sh_ref[0], sc_ref[0]).astype(BF16)
    pt = lax.dot_general(w_ref[...], h, NT_DIMS, preferred_element_type=F32)
    c64, s64, c32, s32 = c64_ref[...], s64_ref[...], c32_ref[...], s32_ref[...]
    z64 = jnp.zeros((64, tm), F32)
    z32 = jnp.zeros((32, tm), F32)
    scale64 = HEAD_DIM ** -0.5 * LOG2E
    scale_b = (B_NOPE + B_ROPE) ** -0.5 * LOG2E
    scale_c = C_QK ** -0.5 * LOG2E
    ones = jnp.ones((VROWS - HEAD_DIM, TOK), BF16)

    def put_values(v_ref, hh, vt):
        for c in range(tm // TOK):
            v_ref[0, c, VROWS * hh: VROWS * hh + HEAD_DIM] = vt[:, TOK * c: TOK * c + TOK].astype(BF16)
            v_ref[0, c, VROWS * hh + HEAD_DIM: VROWS * hh + VROWS] = ones

    def gqa(off, q_ref, k_ref, v_ref, qn, kn):
        for hh in range(4):
            q = pt[off + 64 * hh: off + 64 * hh + 64]
            if qn is not None:
                q = _rms_t(q, qn)
            q = _rope_t(q, c64, s64) * scale64
            q_ref[0, hh] = (jnp.concatenate([q, z64], axis=0) if hh // 2 == 0
                            else jnp.concatenate([z64, q], axis=0)).astype(BF16)
        ks = []
        for j in range(2):
            k = pt[off + 256 + 64 * j: off + 256 + 64 * j + 64]
            if kn is not None:
                k = _rms_t(k, kn)
            ks.append(_rope_t(k, c64, s64))
        k_ref[0] = jnp.concatenate(ks, axis=0).T.astype(BF16)
        for j in range(2):
            put_values(v_ref, j, pt[off + 384 + 64 * j: off + 384 + 64 * j + 64])

    gqa(A_OFF, qa_ref, ka_ref, va_ref, None, None)
    gqa(D_OFF, qd_ref, kd_ref, vd_ref, dqn_ref[...], dkn_ref[...])

    cq = _rms_t(pt[B_OFF: B_OFF + B_Q_RANK], bqn_ref[...]).astype(BF16)
    ckv = _rms_t(pt[B_OFF + B_Q_RANK: B_OFF + B_Q_RANK + B_KV_RANK], bkvn_ref[...]).astype(BF16)
    kr = _rope_t(pt[B_OFF + B_Q_RANK + B_KV_RANK: B_OFF + B_COLS], c32, s32)
    qt = jnp.dot(wuq_ref[...], cq, preferred_element_type=F32)
    kvt = jnp.dot(wukv_ref[...], ckv, preferred_element_type=F32)
    kparts = []
    dq = B_NOPE + B_ROPE
    for hh in range(B_HEADS):
        qn_ = qt[dq * hh: dq * hh + B_NOPE]
        qr = _rope_t(qt[dq * hh + B_NOPE: dq * hh + dq], c32, s32)
        qb_ref[0, hh] = (jnp.concatenate([qn_, qr, z32], axis=0) * scale_b).astype(BF16)
        kparts.append(jnp.concatenate([kvt[128 * hh: 128 * hh + B_NOPE], kr, z32], axis=0))
        put_values(vb_ref, hh, kvt[128 * hh + B_NOPE: 128 * hh + 128])
    kb_ref[0] = jnp.concatenate(kparts, axis=0).T.astype(BF16)

    for j in range(2 * C_HEADS):
        q = _rope_t(pt[C_OFF + 32 * j: C_OFF + 32 * j + 32], c32, s32) * scale_c
        pieces = [z32, z32, z32, z32]
        pieces[j % 4] = q
        qc_ref[0, j] = jnp.concatenate(pieces, axis=0).astype(BF16)
    kc = [_rope_t(pt[C_OFF + 256 + 32 * j: C_OFF + 256 + 32 * j + 32], c32, s32) for j in range(2 * C_HEADS)]
    kc_ref[0] = jnp.concatenate(kc, axis=0).T.astype(BF16)
    for hh in range(C_HEADS):
        put_values(vc_ref, hh, pt[C_OFF + 512 + 64 * hh: C_OFF + 512 + 64 * hh + 64])


def _proj(x, g, shift, scale, w_in_t, rope64, rope32, bqn, wuq_t, bkvn, wukv_t, dqn, dkn):
    b, s, d = x.shape
    tm = min(2 * TOK, s)
    n = s // tm
    full = lambda a: pl.BlockSpec(a.shape, lambda bi, i: (0,) * a.ndim)
    q_spec = lambda nh: pl.BlockSpec((1, nh, QPAD, tm), lambda bi, i: (bi, 0, 0, i))
    k_spec = lambda w: pl.BlockSpec((1, tm, w), lambda bi, i: (bi, i, 0))
    v_spec = lambda r: pl.BlockSpec((1, tm // TOK, r, TOK), lambda bi, i: (bi, i, 0, 0))
    q_shape = lambda nh: jax.ShapeDtypeStruct((b, nh, QPAD, s), BF16)
    k_shape = lambda w: jax.ShapeDtypeStruct((b, s, w), BF16)
    v_shape = lambda r: jax.ShapeDtypeStruct((b, s // TOK, r, TOK), BF16)
    tab = lambda t: pl.BlockSpec((t.shape[0], tm), lambda bi, i: (0, i))
    c64, s64 = rope64
    c32, s32 = rope32
    return pl.pallas_call(
        _proj_body,
        grid=(b, n),
        in_specs=[pl.BlockSpec((1, tm, d), lambda bi, i: (bi, i, 0)), full(g),
                  pl.BlockSpec((1, 1, d), lambda bi, i: (bi, 0, 0)),
                  pl.BlockSpec((1, 1, d), lambda bi, i: (bi, 0, 0)),
                  full(w_in_t), tab(c64), tab(s64), tab(c32), tab(s32),
                  full(bqn), full(wuq_t), full(bkvn), full(wukv_t), full(dqn), full(dkn)],
        out_specs=[q_spec(4), k_spec(128), v_spec(2 * VROWS),
                   q_spec(4), k_spec(512), v_spec(4 * VROWS),
                   q_spec(8), k_spec(256), v_spec(4 * VROWS),
                   q_spec(4), k_spec(128), v_spec(2 * VROWS)],
        out_shape=[q_shape(4), k_shape(128), v_shape(2 * VROWS),
                   q_shape(4), k_shape(512), v_shape(4 * VROWS),
                   q_shape(8), k_shape(256), v_shape(4 * VROWS),
                   q_shape(4), k_shape(128), v_shape(2 * VROWS)],
        compiler_params=_params("arbitrary", "arbitrary"),
        name="proj",
    )(x, g, shift, scale, w_in_t, c64, s64, c32, s32, bqn, wuq_t, bkvn, wukv_t, dqn, dkn)


def _dense_body(*refs, has_lat, has_sink, n_lat):
    refs = list(refs)
    sink_ref = refs.pop(0) if has_sink else None
    q_ref, kc_ref, vc_ref = refs[:3]
    kl_ref, vl_ref = (refs[3], refs[4]) if has_lat else (None, None)
    o_ref, s_even, s_odd, acc_ref = refs[-4:]
    tq = q_ref.shape[-1]
    qt = q_ref[0, 0]

    def produce(kblk, s_ref):
        s = jnp.dot(kblk, qt, preferred_element_type=F32)
        s_ref[...] = s
        return jnp.max(s, axis=0, keepdims=True)

    def consume(s_ref, mx, vblk, m):
        m_new = mx if m is None else jnp.maximum(m, mx)
        p = jnp.exp2(s_ref[...] - m_new).astype(BF16)
        pv = jnp.dot(vblk, p, preferred_element_type=F32)
        acc_ref[...] = pv if m is None else jnp.exp2(m - m_new) * acc_ref[...] + pv
        return m_new

    def lat_keys(i):
        return kl_ref[0, pl.ds(pl.multiple_of(i * TOK, TOK), TOK), :]

    mx_c = produce(kc_ref[0], s_odd)
    if has_lat:
        mx_e = produce(lat_keys(0), s_even)
    m = consume(s_odd, mx_c, vc_ref[0, 0], None)
    if has_lat:
        def body(j, carry):
            m, mx_e = carry
            mx_o = produce(lat_keys(2 * j + 1), s_odd)
            m = consume(s_even, mx_e, vl_ref[0, 2 * j], m)
            mx_e = produce(lat_keys(jnp.minimum(2 * j + 2, n_lat - 1)), s_even)
            m = consume(s_odd, mx_o, vl_ref[0, 2 * j + 1], m)
            return m, mx_e
        m, _ = lax.fori_loop(0, n_lat // 2, body, (m, mx_e), unroll=ATTN_UNROLL)
    acc = acc_ref[...]
    num, den = acc[:HEAD_DIM], acc[HEAD_DIM:HEAD_DIM + 1]
    if has_sink:
        sk = sink_ref[pl.program_id(1)] * LOG2E
        m2 = jnp.maximum(m, sk)
        a = jnp.exp2(m - m2)
        den = den * a + jnp.exp2(sk - m2)
        num = num * a
    o_ref[0, 0] = num * (1.0 / den)


def _dense_attn(q, k_ctx, v_ctx, k_lat, v_lat, kgroup, vhead, sink=None):
    b, nh, _, s = q.shape
    tq = min(TQ, s)
    has_lat = k_lat is not None
    has_sink = sink is not None
    sc = k_ctx.shape[1]
    in_specs = [pl.BlockSpec((1, 1, QPAD, tq), lambda bi, h, i: (bi, h, 0, i)),
                pl.BlockSpec((1, sc, LANES), lambda bi, h, i: (bi, 0, kgroup(h))),
                pl.BlockSpec((1, 1, VROWS, sc), lambda bi, h, i: (bi, 0, vhead(h), 0))]
    args = [q, k_ctx, v_ctx]
    n_lat = 0
    if has_lat:
        sl = k_lat.shape[1]
        n_lat = sl // TOK
        assert n_lat % 2 == 0
        in_specs += [pl.BlockSpec((1, sl, LANES), lambda bi, h, i: (bi, 0, kgroup(h))),
                     pl.BlockSpec((1, n_lat, VROWS, TOK), lambda bi, h, i: (bi, 0, vhead(h), 0))]
        args += [k_lat, v_lat]
    if has_sink:
        in_specs = [pl.BlockSpec(memory_space=pltpu.SMEM)] + in_specs
        args = [sink] + args
    return pl.pallas_call(
        functools.partial(_dense_body, has_lat=has_lat, has_sink=has_sink, n_lat=n_lat),
        grid=(b, nh, s // tq),
        in_specs=in_specs,
        out_specs=pl.BlockSpec((1, 1, HEAD_DIM, tq), lambda bi, h, i: (bi, h, 0, i)),
        out_shape=jax.ShapeDtypeStruct((b, nh, HEAD_DIM, s), F32),
        scratch_shapes=[pltpu.VMEM((TOK, tq), F32), pltpu.VMEM((TOK, tq), F32), pltpu.VMEM((VROWS, tq), F32)],
        compiler_params=_params("arbitrary", "arbitrary", "arbitrary"),
        name="dense_attn",
    )(*args)


def _window_body(sink_ref, q_ref, kc_ref, vc_ref, kp_ref, kq_ref, kn_ref, vp_ref, vq_ref, vn_ref, o_ref,
                 s_even, s_odd, acc_ref):
    i = pl.program_id(1)
    n = pl.num_programs(1)
    nh, tq = q_ref.shape[1], q_ref.shape[-1]
    width = nh * tq
    qt = jnp.concatenate([q_ref[0, hh] for hh in range(nh)], axis=1)
    key = lax.broadcasted_iota(I32, (TOK, width), 0)
    qry = jnp.bitwise_and(lax.broadcasted_iota(I32, (TOK, width), 1), tq - 1)
    dist = key - qry

    def produce(k_ref, ok, s_ref):
        s = jnp.dot(k_ref[0], qt, preferred_element_type=F32)
        if ok is not None:
            s = jnp.where(ok, s, NEG_INF)
        s_ref[...] = s
        return jnp.max(s, axis=0, keepdims=True)

    def consume(s_ref, mx, v_ref, m):
        m_new = jnp.maximum(m, mx)
        p = jnp.exp2(s_ref[...] - m_new).astype(BF16)
        pv = [jnp.dot(v_ref[0, 0, VROWS * g: VROWS * g + VROWS, :], p[:, 2 * tq * g: 2 * tq * g + 2 * tq],
                      preferred_element_type=F32) for g in range(A_KV)]
        acc_ref[...] = jnp.exp2(m - m_new) * acc_ref[...] + jnp.concatenate(pv, axis=1)
        return m_new

    acc_ref[...] = jnp.zeros(acc_ref.shape, F32)
    mx_ctx = produce(kc_ref, None, s_even)
    mx_cur = produce(kq_ref, jnp.abs(dist) <= WINDOW, s_odd)
    m = consume(s_even, mx_ctx, vc_ref, jnp.full((1, width), NEG_INF, F32))
    mx_prev = produce(kp_ref, (dist >= TOK - WINDOW) & (i > 0), s_even)
    m = consume(s_odd, mx_cur, vq_ref, m)
    mx_next = produce(kn_ref, (dist <= WINDOW - TOK) & (i < n - 1), s_odd)
    m = consume(s_even, mx_prev, vp_ref, m)
    m = consume(s_odd, mx_next, vn_ref, m)

    sk = jnp.concatenate([jnp.full((1, tq), sink_ref[hh] * LOG2E, F32) for hh in range(nh)], axis=1)
    m2 = jnp.maximum(m, sk)
    a = jnp.exp2(m - m2)
    acc = acc_ref[...]
    out = acc[:HEAD_DIM] * a / (acc[HEAD_DIM:HEAD_DIM + 1] * a + jnp.exp2(sk - m2))
    for hh in range(nh):
        o_ref[0, hh] = out[:, tq * hh: tq * hh + tq]


def _window_attn(q, k_ctx, v_ctx, k_lat, v_lat, sink):
    b, nh, _, s = q.shape
    n = s // TOK
    sc = k_ctx.shape[1]
    prev = lambda i: jnp.maximum(i - 1, 0)
    nxt = lambda i: jnp.minimum(i + 1, n - 1)
    kspec = lambda f: pl.BlockSpec((1, TOK, LANES), lambda bi, i: (bi, f(i), 0))
    vspec = lambda f: pl.BlockSpec((1, 1, 2 * VROWS, TOK), lambda bi, i: (bi, f(i), 0, 0))
    same = lambda i: i
    return pl.pallas_call(
        _window_body,
        grid=(b, n),
        in_specs=[pl.BlockSpec(memory_space=pltpu.SMEM),
                  pl.BlockSpec((1, nh, QPAD, TOK), lambda bi, i: (bi, 0, 0, i)),
                  pl.BlockSpec((1, sc, LANES), lambda bi, i: (bi, 0, 0)),
                  pl.BlockSpec((1, 1, 2 * VROWS, sc), lambda bi, i: (bi, 0, 0, 0)),
                  kspec(prev), kspec(same), kspec(nxt), vspec(prev), vspec(same), vspec(nxt)],
        out_specs=pl.BlockSpec((1, nh, HEAD_DIM, TOK), lambda bi, i: (bi, 0, 0, i)),
        out_shape=jax.ShapeDtypeStruct((b, nh, HEAD_DIM, s), F32),
        scratch_shapes=[pltpu.VMEM((TOK, nh * TOK), F32), pltpu.VMEM((TOK, nh * TOK), F32),
                        pltpu.VMEM((VROWS, nh * TOK), F32)],
        compiler_params=_params("arbitrary", "arbitrary"),
        name="window_attn",
    )(sink, q, k_ctx, v_ctx, k_lat, k_lat, k_lat, v_lat, v_lat, v_lat)


def _outproj_body(oa_ref, ob_ref, oc_ref, od_ref, lam_ref, subln_ref, w_ref, x_ref, gt_ref,
                  g_ref, sh_ref, sc_ref, wr_ref, xo_ref, h_ref, lg_ref, *, lam_init):
    lp = lam_ref[...]
    lam = (jnp.exp(jnp.sum(lp[0:1] * lp[1:2], axis=1, keepdims=True))
           - jnp.exp(jnp.sum(lp[2:3] * lp[3:4], axis=1, keepdims=True)) + lam_init)
    parts = [oa_ref[0, hh] for hh in range(4)] + [ob_ref[0, hh] for hh in range(4)]
    for hh in range(C_HEADS):
        o = oc_ref[0, 2 * hh] - lam * oc_ref[0, 2 * hh + 1]
        parts.append(_rms_t(o, subln_ref[...]) * (1.0 - lam_init))
    parts += [od_ref[0, hh] for hh in range(4)]
    ot = jnp.concatenate(parts, axis=0).astype(BF16)
    out = lax.dot_general(ot, w_ref[...], TN_DIMS, preferred_element_type=F32)
    xn = x_ref[0] + gt_ref[0] * out
    xo_ref[0] = xn
    h = _modulate(xn, g_ref[...], sh_ref[0], sc_ref[0])
    h_ref[0] = h.astype(BF16)
    lg_ref[0] = lax.dot_general(wr_ref[...], h, NT_DIMS, precision=HIGHEST, preferred_element_type=F32)


def _outproj(oa, ob, oc, od, lam_p, subln, w_out, x, gate, g_ffn, shift, scale, w_router_t, lam_init):
    b, s, d = x.shape
    tm = min(2 * TOK, s)
    ospec = lambda nh: pl.BlockSpec((1, nh, HEAD_DIM, tm), lambda bi, i: (bi, 0, 0, i))
    full = lambda a: pl.BlockSpec(a.shape, lambda bi, i: (0,) * a.ndim)
    row = pl.BlockSpec((1, 1, d), lambda bi, i: (bi, 0, 0))
    return pl.pallas_call(
        functools.partial(_outproj_body, lam_init=lam_init),
        grid=(b, s // tm),
        in_specs=[ospec(4), ospec(4), ospec(8), ospec(4), full(lam_p), full(subln), full(w_out),
                  pl.BlockSpec((1, tm, d), lambda bi, i: (bi, i, 0)), row, full(g_ffn), row, row,
                  full(w_router_t)],
        out_specs=[pl.BlockSpec((1, tm, d), lambda bi, i: (bi, i, 0)),
                   pl.BlockSpec((1, tm, d), lambda bi, i: (bi, i, 0)),
                   pl.BlockSpec((1, N_EXPERTS, tm), lambda bi, i: (bi, 0, i))],
        out_shape=[jax.ShapeDtypeStruct((b, s, d), F32), jax.ShapeDtypeStruct((b, s, d), BF16),
                   jax.ShapeDtypeStruct((b, N_EXPERTS, s), F32)],
        compiler_params=_params("arbitrary", "arbitrary"),
        name="outproj",
    )(oa, ob, oc, od, lam_p, subln, w_out, x, gate, g_ffn, shift, scale, w_router_t)


def _router_body(lg_ref, slot_ref, aff_ref, st_ref, *, cap, nblk):
    lg = lg_ref[0]
    ex = jnp.exp(lg - jnp.max(lg, axis=0, keepdims=True))
    aff = ex / jnp.sum(ex, axis=0, keepdims=True)
    aff_ref[0] = aff

    def search(_, bounds):
        lo, hi = bounds
        mid = (lo + hi) * 0.5
        enough = jnp.sum((aff >= mid).astype(I32), axis=1, keepdims=True) >= cap
        return jnp.where(enough, mid, lo), jnp.where(enough, hi, mid)

    lo, hi = lax.fori_loop(0, BISECT_STEPS, search,
                           (jnp.zeros((N_EXPERTS, 1), F32), jnp.full((N_EXPERTS, 1), 2.0, F32)))
    gt = aff >= hi
    eq = (aff >= lo) & jnp.logical_not(gt)
    need = (cap - jnp.sum(gt.astype(I32), axis=1, keepdims=True)).astype(F32)
    tri = (lax.broadcasted_iota(I32, (TOK, TOK), 0) <= lax.broadcasted_iota(I32, (TOK, TOK), 1)).astype(BF16)
    lane = lax.broadcasted_iota(I32, (N_EXPERTS, LANES), 1)
    eq_seen = jnp.zeros((N_EXPERTS, 1), F32)
    base = jnp.zeros((N_EXPERTS, 1), I32)
    starts = jnp.zeros((N_EXPERTS, LANES), I32)
    for j in range(nblk):
        cols = slice(TOK * j, TOK * j + TOK)
        eq_c = eq[:, cols]
        eq_cum = jnp.dot(eq_c.astype(BF16), tri, preferred_element_type=F32) + eq_seen
        eq_seen = eq_seen + jnp.sum(eq_c.astype(F32), axis=1, keepdims=True)
        sel = gt[:, cols] | (eq_c & (eq_cum <= need))
        cum = jnp.dot(sel.astype(BF16), tri, preferred_element_type=F32).astype(I32)
        slot_ref[0, :, cols] = jnp.where(sel, base + cum - 1, -1)
        starts = jnp.where(lane == j, base, starts)
        cnt = jnp.sum(sel.astype(I32), axis=1, keepdims=True)
        base = base + jnp.bitwise_and(cnt + (SUBLANES - 1), -SUBLANES)
    st_ref[0] = jnp.where(lane == nblk, base, starts)


def _router(logits_t, cap):
    b, e, s = logits_t.shape
    nblk = s // TOK
    return pl.pallas_call(
        functools.partial(_router_body, cap=cap, nblk=nblk),
        grid=(b,),
        in_specs=[pl.BlockSpec((1, e, s), lambda bi: (bi, 0, 0))],
        out_specs=[pl.BlockSpec((1, e, s), lambda bi: (bi, 0, 0)),
                   pl.BlockSpec((1, e, s), lambda bi: (bi, 0, 0)),
                   pl.BlockSpec((1, e, LANES), lambda bi: (bi, 0, 0))],
        out_shape=[jax.ShapeDtypeStruct((b, e, s), I32), jax.ShapeDtypeStruct((b, e, s), F32),
                   jax.ShapeDtypeStruct((b, e, LANES), I32)],
        compiler_params=_params("arbitrary"),
        name="router",
    )(logits_t)


def _onehot_t(slot_ref, first, win):
    rows = lax.broadcasted_iota(I32, (win, TOK), 0)
    return [rows == (slot_ref[0, e:e + 1, :] - first[e]) for e in range(N_EXPERTS)]


def _dispatch_body(st_ref, h_ref, slot_ref, aff_ref, xs_ref, stage, sems, count, *, win):
    bi, j = pl.program_id(0), pl.program_id(1)
    d = h_ref.shape[-1]

    @pl.when((bi == 0) & (j == 0))
    def _():
        count[0] = 0

    rows_alloc = xs_ref.shape[2]
    last_block = j == pl.num_programs(1) - 1
    st = [st_ref[bi, e, j] for e in range(N_EXPERTS)]
    ends = [jnp.where(last_block, rows_alloc, st_ref[bi, e, j + 1]) for e in range(N_EXPERTS)]
    width = ends[0] - st[0]
    for e in range(1, N_EXPERTS):
        width = jnp.maximum(width, ends[e] - st[e])
    rounds = jnp.maximum((width + win - 1) // win, 1)

    def copies(buf, first):
        return [pltpu.make_async_copy(stage.at[buf, pl.ds(e * win, win), :],
                                      xs_ref.at[bi, e, pl.ds(pl.multiple_of(first[e], SUBLANES), win), :],
                                      sems.at[e]) for e in range(N_EXPERTS)]

    def one_round(r, carry):
        k = count[0]
        buf = k % 2
        first = [jnp.minimum(st[e] + r * win, rows_alloc - win) for e in range(N_EXPERTS)]
        hot = _onehot_t(slot_ref, first, win)
        p = jnp.concatenate(hot, axis=0).astype(BF16)
        stage[buf, :, 0:d] = jnp.dot(p, h_ref[0], preferred_element_type=F32)
        gates = [jnp.sum(jnp.where(hot[e], aff_ref[0, e:e + 1, :], 0.0), axis=1, keepdims=True)
                 for e in range(N_EXPERTS)]
        stage[buf, :, d:d + LANES] = jnp.broadcast_to(jnp.concatenate(gates, axis=0), (N_EXPERTS * win, LANES))

        @pl.when(k > 0)
        def _():
            for c in copies(1 - buf, st):
                c.wait()

        for c in copies(buf, first):
            c.start()
        count[0] = k + 1
        return carry

    lax.fori_loop(0, rounds, one_round, 0)

    @pl.when((bi == pl.num_programs(0) - 1) & (j == pl.num_programs(1) - 1))
    def _():
        for c in copies(0, st):
            c.wait()


def _dispatch(starts, h, slot, aff, rows_alloc, win):
    b, s, d = h.shape
    nblk = s // TOK
    e = N_EXPERTS
    return pl.pallas_call(
        functools.partial(_dispatch_body, win=win),
        grid_spec=pltpu.PrefetchScalarGridSpec(
            num_scalar_prefetch=1, grid=(b, nblk),
            in_specs=[pl.BlockSpec((1, TOK, d), lambda bi, j, st: (bi, j, 0)),
                      pl.BlockSpec((1, e, TOK), lambda bi, j, st: (bi, 0, j)),
                      pl.BlockSpec((1, e, TOK), lambda bi, j, st: (bi, 0, j))],
            out_specs=pl.BlockSpec(memory_space=pl.ANY),
            scratch_shapes=[pltpu.VMEM((2, e * win, d + LANES), F32),
                            pltpu.SemaphoreType.DMA((e,)),
                            pltpu.SMEM((1,), I32)]),
        out_shape=jax.ShapeDtypeStruct((b, e, rows_alloc, d + LANES), F32),
        compiler_params=_params("arbitrary", "arbitrary"),
        name="dispatch",
    )(starts, h, slot, aff)


def _ffn_body(st_ref, xs_ref, wg_in, wu_in, wd_in, y_ref, *rest, nblk, layer, emit_copies):
    e, bi, i = pl.program_id(0), pl.program_id(1), pl.program_id(2)
    tm = xs_ref.shape[2]
    d = y_ref.shape[-1]
    used = st_ref[bi, e, nblk]
    base = i * tm
    if layer is None:
        wg_ref, wu_ref, wd_ref = wg_in, wu_in, wd_in
    else:
        if emit_copies:
            wg_ref, wu_ref, wd_ref, bufs, sems = rest[0], rest[1], rest[2], rest[3:6], rest[6]
        else:
            bufs, sems, (wg_ref, wu_ref, wd_ref) = rest[0:3], rest[3], rest[4:7]

        def fetch(expert, slot):
            return [pltpu.make_async_copy(w.at[layer, expert], buf.at[slot], sems.at[slot, k])
                    for k, (w, buf) in enumerate(zip((wg_in, wu_in, wd_in), bufs))]

        @pl.when((bi == 0) & (i == 0))
        def _():
            slot = e % 2

            @pl.when(e == 0)
            def _():
                for c in fetch(0, 0):
                    c.start()

            for c in fetch(e, slot):
                c.wait()

            @pl.when(e + 1 < pl.num_programs(0))
            def _():
                for c in fetch(e + 1, 1 - slot):
                    c.start()

            for out, buf in zip((wg_ref, wu_ref, wd_ref), bufs):
                out[0] = buf[slot].astype(BF16)

    def run(n):
        valid = (base + lax.broadcasted_iota(I32, (n, 1), 0)) < used
        xa = xs_ref[0, 0, 0:n, :]
        x = jnp.where(valid, xa[:, 0:d], 0.0).astype(BF16)
        gate = jnp.where(valid, xa[:, d:d + 1], 0.0)
        a = jnp.dot(x, wg_ref[0], preferred_element_type=F32)
        u = jnp.dot(x, wu_ref[0], preferred_element_type=F32)
        mid = (a * jax.nn.sigmoid(a) * u).astype(BF16)
        y_ref[0, 0, 0:n, :] = jnp.dot(mid, wd_ref[0], preferred_element_type=F32) * gate
        if n < tm:
            y_ref[0, 0, n:tm, :] = jnp.zeros((tm - n, d), F32)

    half = tm // 2 if tm % (4 * SUBLANES) == 0 else 0
    left = used - base

    @pl.when(left > half)
    def _():
        run(tm)

    if half:
        @pl.when((left > 0) & (left <= half))
        def _():
            run(half)

    @pl.when(left <= 0)
    def _():
        y_ref[0, 0] = jnp.zeros(y_ref.shape[2:], F32)


def _ffn(starts, xs, wg, wu, wd, rows, tm, nblk, layer=None, emit_copies=False):
    b, e = xs.shape[:2]
    d, f = wg.shape[-2:]
    wspec = lambda r, c: pl.BlockSpec((1, r, c), lambda ei, bi, i, st: (ei, 0, 0))
    out_specs = [pl.BlockSpec((1, 1, tm, d), lambda ei, bi, i, st: (bi, ei, i, 0))]
    out_shape = [jax.ShapeDtypeStruct((b, e, rows, d), F32)]
    scratch = []
    if layer is None:
        w_specs = [wspec(d, f), wspec(d, f), wspec(f, d)]
    else:
        w_specs = [pl.BlockSpec(memory_space=pl.ANY)] * 3
        scratch = [pltpu.VMEM((2,) + w.shape[2:], F32) for w in (wg, wu, wd)] + [pltpu.SemaphoreType.DMA((2, 3))]
        if emit_copies:
            out_specs += [wspec(d, f), wspec(d, f), wspec(f, d)]
            out_shape += [jax.ShapeDtypeStruct(w.shape[1:], BF16) for w in (wg, wu, wd)]
        else:
            scratch += [pltpu.VMEM((1,) + w.shape[2:], BF16) for w in (wg, wu, wd)]
    return pl.pallas_call(
        functools.partial(_ffn_body, nblk=nblk, layer=layer, emit_copies=emit_copies),
        grid_spec=pltpu.PrefetchScalarGridSpec(
            num_scalar_prefetch=1, grid=(e, b, rows // tm),
            in_specs=[pl.BlockSpec((1, 1, tm, d + LANES), lambda ei, bi, i, st: (bi, ei, i, 0))] + w_specs,
            out_specs=out_specs, scratch_shapes=scratch),
        out_shape=out_shape,
        compiler_params=_params("arbitrary", "arbitrary", "arbitrary"),
        name="expert_ffn",
    )(starts, xs, wg, wu, wd)


def _combine_body(st_ref, slot_ref, x_ref, gt_ref, gf_ref, y_ref, o_ref, ybuf, sems, *, win, rows, final_norm):
    bi, j = pl.program_id(0), pl.program_id(1)
    nj = pl.num_programs(1)
    d = x_ref.shape[-1]
    step = bi * nj + j
    cur = step % 2
    st = [st_ref[bi, e, j] for e in range(N_EXPERTS)]
    width = st_ref[bi, 0, j + 1] - st[0]
    for e in range(1, N_EXPERTS):
        width = jnp.maximum(width, st_ref[bi, e, j + 1] - st[e])
    rounds = (width + win - 1) // win

    def window(b_, j_, r):
        lo = [st_ref[b_, e, j_] + r * win for e in range(N_EXPERTS)]
        return lo, [jnp.minimum(lo[e], rows - win) for e in range(N_EXPERTS)]

    def copies(b_, first, buf):
        return [pltpu.make_async_copy(y_ref.at[b_, e, pl.ds(pl.multiple_of(first[e], SUBLANES), win), :],
                                      ybuf.at[buf, pl.ds(e * win, win), :], sems.at[buf, e])
                for e in range(N_EXPERTS)]

    def scatter(lo, first, buf, acc):
        hot = _onehot_t(slot_ref, first, win)
        hot = [hot[e] & (slot_ref[0, e:e + 1, :] >= lo[e]) for e in range(N_EXPERTS)]
        p = jnp.concatenate(hot, axis=0).astype(BF16)
        return acc + lax.dot_general(p, ybuf[buf].astype(BF16), TN_DIMS, preferred_element_type=F32)

    lo0, first0 = window(bi, j, 0)

    @pl.when(step == 0)
    def _():
        for c in copies(bi, first0, cur):
            c.start()

    @pl.when(step < pl.num_programs(0) * nj - 1)
    def _():
        wrap = j == nj - 1
        b_next, j_next = jnp.where(wrap, bi + 1, bi), jnp.where(wrap, 0, j + 1)
        for c in copies(b_next, window(b_next, j_next, 0)[1], 1 - cur):
            c.start()

    for c in copies(bi, first0, cur):
        c.wait()
    acc = scatter(lo0, first0, cur, jnp.zeros((TOK, d), F32))

    def later_round(r, acc):
        lo, first = window(bi, j, r)
        for c in copies(bi, first, 2):
            c.start()
        for c in copies(bi, first, 2):
            c.wait()
        return scatter(lo, first, 2, acc)

    acc = lax.fori_loop(1, rounds, later_round, acc)
    out = x_ref[0] + gt_ref[0] * acc
    if final_norm:
        ms = jnp.mean(out * out, axis=-1, keepdims=True)
        out = out * lax.rsqrt(ms + EPS) * gf_ref[...]
    o_ref[0] = out


def _combine(starts, slot, x, gate, g_final, y, win, final_norm):
    b, s, d = x.shape
    e = N_EXPERTS
    rows = y.shape[2]
    return pl.pallas_call(
        functools.partial(_combine_body, win=win, rows=rows, final_norm=final_norm),
        grid_spec=pltpu.PrefetchScalarGridSpec(
            num_scalar_prefetch=1, grid=(b, s // TOK),
            in_specs=[pl.BlockSpec((1, e, TOK), lambda bi, j, st: (bi, 0, j)),
                      pl.BlockSpec((1, TOK, d), lambda bi, j, st: (bi, j, 0)),
                      pl.BlockSpec((1, 1, d), lambda bi, j, st: (bi, 0, 0)),
                      pl.BlockSpec((1, d), lambda bi, j, st: (0, 0)),
                      pl.BlockSpec(memory_space=pl.ANY)],
            out_specs=pl.BlockSpec((1, TOK, d), lambda bi, j, st: (bi, j, 0)),
            scratch_shapes=[pltpu.VMEM((3, e * win, d), F32), pltpu.SemaphoreType.DMA((3, e))]),
        out_shape=jax.ShapeDtypeStruct((b, s, d), F32),
        compiler_params=_params("arbitrary", "arbitrary"),
        name="combine",
    )(starts, slot, x, gate, g_final, y)


def _moe(x, h, logits_t, gate, g_final, weights, final_norm, layer=None, emit_copies=False):
    wg, wu, wd = weights
    n = x.shape[1]
    nblk = n // TOK
    cap = CAPACITY_FACTOR * n // N_EXPERTS
    win = min(SLOT_WINDOW, cap)
    rows = cap + SUBLANES * nblk
    tm = min(TOK, rows)
    rows = -(-rows // tm) * tm
    slot, aff, starts = _router(logits_t, cap)
    xs = _dispatch(starts, h, slot, aff, rows + win, win)
    y, *copies = _ffn(starts, xs, wg, wu, wd, rows, tm, nblk, layer, emit_copies)
    return _combine(starts, slot, x, gate, g_final, y, win, final_norm), copies


def _rope_tables(n_tokens, rot_dim):
    rows = n_tokens // GRID_W
    row = jnp.repeat(jnp.arange(rows), GRID_W)
    col = jnp.tile(jnp.arange(GRID_W), rows)
    quarter = rot_dim // 4
    inv_freq = ROPE_THETA ** (-jnp.arange(quarter, dtype=F32) / quarter)
    ang = jnp.concatenate([inv_freq[:, None] * row[None, :], inv_freq[:, None] * col[None, :]], axis=0)
    return jnp.cos(ang), jnp.sin(ang)


def _identity_tables(n_tokens, rot_dim):
    return jnp.ones((rot_dim // 2, n_tokens), F32), jnp.zeros((rot_dim // 2, n_tokens), F32)


def kernel(x, c, ctx, c_ctx, w_mod, b_mod, g_attn, g_ffn, w_in, a_sink, b_q_norm, b_w_uq, b_kv_norm, b_w_ukv, c_lambda, c_subln, d_q_norm, d_k_norm, w_out, w_router, w_gate, w_up, w_down, g_final):
    b, s, d = x.shape
    depth = w_mod.shape[0]
    n_ctx = ctx.shape[1]
    rope64, rope32 = _rope_tables(s, HEAD_DIM), _rope_tables(s, C_QK)
    id64, id32 = _identity_tables(n_ctx, HEAD_DIM), _identity_tables(n_ctx, C_QK)

    cond = jnp.zeros((SUBLANES, d), F32).at[:b].set(c).at[b].set(c_ctx)
    mod = _adaln(cond, w_mod, b_mod)

    col = lambda v: v.reshape(-1, 1)
    xl, xc = x, ctx
    for l in range(depth):
        last = l == depth - 1
        lam_init = 0.8 - 0.6 * math.exp(-0.3 * l)
        m6 = mod[l].reshape(SUBLANES, 6, d)
        lat = [m6[:b, k][:, None, :] for k in range(6)]
        cx = [jnp.broadcast_to(m6[b, k][None, None, :], (b, 1, d)) for k in range(6)]
        w_in_t = w_in[l].T.astype(BF16)
        wuq_t = b_w_uq[l].T.astype(BF16)
        wukv_t = b_w_ukv[l].T.astype(BF16)
        w_out_b = w_out[l].astype(BF16)
        w_router_t = w_router[l].T
        g_a, g_f = g_attn[l][None, :], g_ffn[l][None, :]
        small = (col(b_q_norm[l]), wuq_t, col(b_kv_norm[l]), wukv_t, col(d_q_norm[l]), col(d_k_norm[l]))

        pl_ = _proj(xl, g_a, lat[0], lat[1], w_in_t, rope64, rope32, *small)
        pc_ = _proj(xc, g_a, cx[0], cx[1], w_in_t, id64, id32, *small)
        qa, ka, va, qb, kb, vb, qc, kc, vc, qd, kd, vd = pl_
        qa_c, ka_c, va_c, qb_c, kb_c, vb_c, qc_c, kc_c, vc_c, qd_c, kd_c, vd_c = pc_

        zero = lambda h: 0
        oa = _window_attn(qa, ka_c, va_c, ka, va, a_sink[l])
        ob = _dense_attn(qb, kb_c, vb_c, kb, vb, lambda h: h, lambda h: h)
        oc = _dense_attn(qc, kc_c, vc_c, kc, vc, lambda h: h // 4, lambda h: h // 2)
        od = _dense_attn(qd, kd_c, vd_c, kd, vd, zero, lambda h: h // 2)
        outproj = functools.partial(_outproj, lam_p=c_lambda[l], subln=col(c_subln[l]), w_out=w_out_b,
                                    g_ffn=g_f, w_router_t=w_router_t, lam_init=lam_init)
        xl, hl, lg = outproj(oa, ob, oc, od, x=xl, gate=lat[2], shift=lat[3], scale=lat[4])
        xl, experts_bf16 = _moe(xl, hl, lg, lat[5], g_final[None, :], (w_gate, w_up, w_down), final_norm=last, layer=l, emit_copies=not last)
        if not last:
            oa = _dense_attn(qa_c, ka_c, va_c, None, None, zero, lambda h: h // 2, sink=a_sink[l])
            ob = _dense_attn(qb_c, kb_c, vb_c, None, None, lambda h: h, lambda h: h)
            oc = _dense_attn(qc_c, kc_c, vc_c, None, None, lambda h: h // 4, lambda h: h // 2)
            od = _dense_attn(qd_c, kd_c, vd_c, None, None, zero, lambda h: h // 2)
            xc, hc, lgc = outproj(oa, ob, oc, od, x=xc, gate=cx[2], shift=cx[3], scale=cx[4])
            xc, _ = _moe(xc, hc, lgc, cx[5], g_final[None, :], experts_bf16, final_norm=False)
    return xl
```

```python
import functools
import math

import jax
import jax.numpy as jnp
from jax import lax
from jax.experimental import pallas as pl
from jax.experimental.pallas import tpu as pltpu

F32, BF16, I32 = jnp.float32, jnp.bfloat16, jnp.int32
HIGHEST = lax.Precision.HIGHEST

SUBLANES = 8
LANES = 128

GRID_W = 64
HEAD_DIM = 64
WINDOW = 128
EPS = 1e-6
ROPE_THETA = 10000.0
NEG_INF = -1e30
A_HEADS, A_KV = 4, 2
B_HEADS, B_Q_RANK, B_KV_RANK, B_NOPE, B_ROPE, B_V = 4, 192, 128, 64, 32, 64
C_HEADS, C_QK, C_V = 4, 32, 64
D_HEADS, D_KV = 4, 2
N_EXPERTS = 16
CAPACITY_FACTOR = 2

A_COLS = A_HEADS * HEAD_DIM + 2 * A_KV * HEAD_DIM
B_COLS = B_Q_RANK + B_KV_RANK + B_ROPE
C_COLS = 4 * C_HEADS * C_QK + C_HEADS * C_V
D_COLS = D_HEADS * HEAD_DIM + 2 * D_KV * HEAD_DIM
A_OFF, B_OFF, C_OFF, D_OFF = 0, A_COLS, A_COLS + B_COLS, A_COLS + B_COLS + C_COLS

BISECT_STEPS = 152
TOK = 256
QPAD = 128
VROWS = HEAD_DIM + 16
TQ = 1024
ATTN_UNROLL = 8
SLOT_WINDOW = 48
LOG2E = math.log2(math.e)

NT_DIMS = (((1,), (1,)), ((), ()))
TN_DIMS = (((0,), (0,)), ((), ()))


def _params(*sem):
    return pltpu.CompilerParams(dimension_semantics=sem, vmem_limit_bytes=56 * 1024 * 1024)


def _adaln_body(c_ref, w_ref, b_ref, o_ref):
    c = c_ref[...]
    s = c * jax.nn.sigmoid(c)
    o_ref[0] = jnp.dot(s, w_ref[0], precision=HIGHEST, preferred_element_type=F32) + b_ref[0]


def _adaln(cond, w_mod, b_mod):
    depth, d, n = w_mod.shape
    tn = n // 4
    return pl.pallas_call(
        _adaln_body,
        grid=(depth, n // tn),
        in_specs=[pl.BlockSpec((SUBLANES, d), lambda l, j: (0, 0)),
                  pl.BlockSpec((1, d, tn), lambda l, j: (l, 0, j)),
                  pl.BlockSpec((1, 1, tn), lambda l, j: (l, 0, j))],
        out_specs=pl.BlockSpec((1, SUBLANES, tn), lambda l, j: (l, 0, j)),
        out_shape=jax.ShapeDtypeStruct((depth, SUBLANES, n), F32),
        compiler_params=_params("arbitrary", "arbitrary"),
        name="adaln",
    )(cond, w_mod, b_mod.reshape(depth, 1, n))


def _rope_t(xt, cos, sin):
    half = xt.shape[0] // 2
    x1, x2 = xt[:half], xt[half:]
    return jnp.concatenate([x1 * cos - x2 * sin, x1 * sin + x2 * cos], axis=0)


def _rms_t(xt, g):
    ms = jnp.mean(xt * xt, axis=0, keepdims=True)
    return xt * lax.rsqrt(ms + EPS) * g


def _modulate(x, g, shift, scale):
    ms = jnp.mean(x * x, axis=-1, keepdims=True)
    return x * lax.rsqrt(ms + EPS) * g * (1.0 + scale) + shift


def _proj_body(x_ref, g_ref, sh_ref, sc_ref, w_ref, c64_ref, s64_ref, c32_ref, s32_ref,
               bqn_ref, wuq_ref, bkvn_ref, wukv_ref, dqn_ref, dkn_ref,
               qa_ref, ka_ref, va_ref, qb_ref, kb_ref, vb_ref,
               qc_ref, kc_ref, vc_ref, qd_ref, kd_ref, vd_ref):
    tm = x_ref.shape[1]
    h = _modulate(x_ref[0], g_ref[...], sh_ref[0], sc_ref[0]).astype(BF16)
    pt = lax.dot_general(w_ref[...], h, NT_DIMS, preferred_element_type=F32)
    c64, s64, c32, s32 = c64_ref[...], s64_ref[...], c32_ref[...], s32_ref[...]
    z64 = jnp.zeros((64, tm), F32)
    z32 = jnp.zeros((32, tm), F32)
    scale64 = HEAD_DIM ** -0.5 * LOG2E
    scale_b = (B_NOPE + B_ROPE) ** -0.5 * LOG2E
    scale_c = C_QK ** -0.5 * LOG2E
    ones = jnp.ones((VROWS - HEAD_DIM, TOK), BF16)

    def put_values(v_ref, hh, vt):
        for c in range(tm // TOK):
            v_ref[0, c, VROWS * hh: VROWS * hh + HEAD_DIM] = vt[:, TOK * c: TOK * c + TOK].astype(BF16)
            v_ref[0, c, VROWS * hh + HEAD_DIM: VROWS * hh + VROWS] = ones

    def gqa(off, q_ref, k_ref, v_ref, qn, kn):
        for hh in range(4):
            q = pt[off + 64 * hh: off + 64 * hh + 64]
            if qn is not None:
                q = _rms_t(q, qn)
            q = _rope_t(q, c64, s64) * scale64
            q_ref[0, hh] = (jnp.concatenate([q, z64], axis=0) if hh // 2 == 0
                            else jnp.concatenate([z64, q], axis=0)).astype(BF16)
        ks = []
        for j in range(2):
            k = pt[off + 256 + 64 * j: off + 256 + 64 * j + 64]
            if kn is not None:
                k = _rms_t(k, kn)
            ks.append(_rope_t(k, c64, s64))
        k_ref[0] = jnp.concatenate(ks, axis=0).T.astype(BF16)
        for j in range(2):
            put_values(v_ref, j, pt[off + 384 + 64 * j: off + 384 + 64 * j + 64])

    gqa(A_OFF, qa_ref, ka_ref, va_ref, None, None)
    gqa(D_OFF, qd_ref, kd_ref, vd_ref, dqn_ref[...], dkn_ref[...])

    cq = _rms_t(pt[B_OFF: B_OFF + B_Q_RANK], bqn_ref[...]).astype(BF16)
    ckv = _rms_t(pt[B_OFF + B_Q_RANK: B_OFF + B_Q_RANK + B_KV_RANK], bkvn_ref[...]).astype(BF16)
    kr = _rope_t(pt[B_OFF + B_Q_RANK + B_KV_RANK: B_OFF + B_COLS], c32, s32)
    qt = jnp.dot(wuq_ref[...], cq, preferred_element_type=F32)
    kvt = jnp.dot(wukv_ref[...], ckv, preferred_element_type=F32)
    kparts = []
    dq = B_NOPE + B_ROPE
    for hh in range(B_HEADS):
        qn_ = qt[dq * hh: dq * hh + B_NOPE]
        qr = _rope_t(qt[dq * hh + B_NOPE: dq * hh + dq], c32, s32)
        qb_ref[0, hh] = (jnp.concatenate([qn_, qr, z32], axis=0) * scale_b).astype(BF16)
        kparts.append(jnp.concatenate([kvt[128 * hh: 128 * hh + B_NOPE], kr, z32], axis=0))
        put_values(vb_ref, hh, kvt[128 * hh + B_NOPE: 128 * hh + 128])
    kb_ref[0] = jnp.concatenate(kparts, axis=0).T.astype(BF16)

    for j in range(2 * C_HEADS):
        q = _rope_t(pt[C_OFF + 32 * j: C_OFF + 32 * j + 32], c32, s32) * scale_c
        pieces = [z32, z32, z32, z32]
        pieces[j % 4] = q
        qc_ref[0, j] = jnp.concatenate(pieces, axis=0).astype(BF16)
    kc = [_rope_t(pt[C_OFF + 256 + 32 * j: C_OFF + 256 + 32 * j + 32], c32, s32) for j in range(2 * C_HEADS)]
    kc_ref[0] = jnp.concatenate(kc, axis=0).T.astype(BF16)
    for hh in range(C_HEADS):
        put_values(vc_ref, hh, pt[C_OFF + 512 + 64 * hh: C_OFF + 512 + 64 * hh + 64])


def _proj(x, g, shift, scale, w_in_t, rope64, rope32, bqn, wuq_t, bkvn, wukv_t, dqn, dkn):
    b, s, d = x.shape
    tm = min(2 * TOK, s)
    n = s // tm
    full = lambda a: pl.BlockSpec(a.shape, lambda bi, i: (0,) * a.ndim)
    q_spec = lambda nh: pl.BlockSpec((1, nh, QPAD, tm), lambda bi, i: (bi, 0, 0, i))
    k_spec = lambda w: pl.BlockSpec((1, tm, w), lambda bi, i: (bi, i, 0))
    v_spec = lambda r: pl.BlockSpec((1, tm // TOK, r, TOK), lambda bi, i: (bi, i, 0, 0))
    q_shape = lambda nh: jax.ShapeDtypeStruct((b, nh, QPAD, s), BF16)
    k_shape = lambda w: jax.ShapeDtypeStruct((b, s, w), BF16)
    v_shape = lambda r: jax.ShapeDtypeStruct((b, s // TOK, r, TOK), BF16)
    tab = lambda t: pl.BlockSpec((t.shape[0], tm), lambda bi, i: (0, i))
    c64, s64 = rope64
    c32, s32 = rope32
    return pl.pallas_call(
        _proj_body,
        grid=(b, n),
        in_specs=[pl.BlockSpec((1, tm, d), lambda bi, i: (bi, i, 0)), full(g),
                  pl.BlockSpec((1, 1, d), lambda bi, i: (bi, 0, 0)),
                  pl.BlockSpec((1, 1, d), lambda bi, i: (bi, 0, 0)),
                  full(w_in_t), tab(c64), tab(s64), tab(c32), tab(s32),
                  full(bqn), full(wuq_t), full(bkvn), full(wukv_t), full(dqn), full(dkn)],
        out_specs=[q_spec(4), k_spec(128), v_spec(2 * VROWS),
                   q_spec(4), k_spec(512), v_spec(4 * VROWS),
                   q_spec(8), k_spec(256), v_spec(4 * VROWS),
                   q_spec(4), k_spec(128), v_spec(2 * VROWS)],
        out_shape=[q_shape(4), k_shape(128), v_shape(2 * VROWS),
                   q_shape(4), k_shape(512), v_shape(4 * VROWS),
                   q_shape(8), k_shape(256), v_shape(4 * VROWS),
                   q_shape(4), k_shape(128), v_shape(2 * VROWS)],
        compiler_params=_params("arbitrary", "arbitrary"),
        name="proj",
    )(x, g, shift, scale, w_in_t, c64, s64, c32, s32, bqn, wuq_t, bkvn, wukv_t, dqn, dkn)


def _dense_body(*refs, has_lat, has_sink, n_lat):
    refs = list(refs)
    sink_ref = refs.pop(0) if has_sink else None
    q_ref, kc_ref, vc_ref = refs[:3]
    kl_ref, vl_ref = (refs[3], refs[4]) if has_lat else (None, None)
    o_ref, s_even, s_odd, acc_ref = refs[-4:]
    tq = q_ref.shape[-1]
    qt = q_ref[0, 0]

    def produce(kblk, s_ref):
        s = jnp.dot(kblk, qt, preferred_element_type=F32)
        s_ref[...] = s
        return jnp.max(s, axis=0, keepdims=True)

    def consume(s_ref, mx, vblk, m):
        m_new = mx if m is None else jnp.maximum(m, mx)
        p = jnp.exp2(s_ref[...] - m_new).astype(BF16)
        pv = jnp.dot(vblk, p, preferred_element_type=F32)
        acc_ref[...] = pv if m is None else jnp.exp2(m - m_new) * acc_ref[...] + pv
        return m_new

    def lat_keys(i):
        return kl_ref[0, pl.ds(pl.multiple_of(i * TOK, TOK), TOK), :]

    mx_c = produce(kc_ref[0], s_odd)
    if has_lat:
        mx_e = produce(lat_keys(0), s_even)
    m = consume(s_odd, mx_c, vc_ref[0, 0], None)
    if has_lat:
        def body(j, carry):
            m, mx_e = carry
            mx_o = produce(lat_keys(2 * j + 1), s_odd)
            m = consume(s_even, mx_e, vl_ref[0, 2 * j], m)
            mx_e = produce(lat_keys(jnp.minimum(2 * j + 2, n_lat - 1)), s_even)
            m = consume(s_odd, mx_o, vl_ref[0, 2 * j + 1], m)
            return m, mx_e
        m, _ = lax.fori_loop(0, n_lat // 2, body, (m, mx_e), unroll=ATTN_UNROLL)
    acc = acc_ref[...]
    num, den = acc[:HEAD_DIM], acc[HEAD_DIM:HEAD_DIM + 1]
    if has_sink:
        sk = sink_ref[pl.program_id(1)] * LOG2E
        m2 = jnp.maximum(m, sk)
        a = jnp.exp2(m - m2)
        den = den * a + jnp.exp2(sk - m2)
        num = num * a
    o_ref[0, 0] = num * (1.0 / den)


def _dense_attn(q, k_ctx, v_ctx, k_lat, v_lat, kgroup, vhead, sink=None):
    b, nh, _, s = q.shape
    tq = min(TQ, s)
    has_lat = k_lat is not None
    has_sink = sink is not None
    sc = k_ctx.shape[1]
    in_specs = [pl.BlockSpec((1, 1, QPAD, tq), lambda bi, h, i: (bi, h, 0, i)),
                pl.BlockSpec((1, sc, LANES), lambda bi, h, i: (bi, 0, kgroup(h))),
                pl.BlockSpec((1, 1, VROWS, sc), lambda bi, h, i: (bi, 0, vhead(h), 0))]
    args = [q, k_ctx, v_ctx]
    n_lat = 0
    if has_lat:
        sl = k_lat.shape[1]
        n_lat = sl // TOK
        assert n_lat % 2 == 0
        in_specs += [pl.BlockSpec((1, sl, LANES), lambda bi, h, i: (bi, 0, kgroup(h))),
                     pl.BlockSpec((1, n_lat, VROWS, TOK), lambda bi, h, i: (bi, 0, vhead(h), 0))]
        args += [k_lat, v_lat]
    if has_sink:
        in_specs = [pl.BlockSpec(memory_space=pltpu.SMEM)] + in_specs
        args = [sink] + args
    return pl.pallas_call(
        functools.partial(_dense_body, has_lat=has_lat, has_sink=has_sink, n_lat=n_lat),
        grid=(b, nh, s // tq),
        in_specs=in_specs,
        out_specs=pl.BlockSpec((1, 1, HEAD_DIM, tq), lambda bi, h, i: (bi, h, 0, i)),
        out_shape=jax.ShapeDtypeStruct((b, nh, HEAD_DIM, s), F32),
        scratch_shapes=[pltpu.VMEM((TOK, tq), F32), pltpu.VMEM((TOK, tq), F32), pltpu.VMEM((VROWS, tq), F32)],
        compiler_params=_params("arbitrary", "arbitrary", "arbitrary"),
        name="dense_attn",
    )(*args)


def _window_body(sink_ref, q_ref, kc_ref, vc_ref, kp_ref, kq_ref, kn_ref, vp_ref, vq_ref, vn_ref, o_ref,
                 s_even, s_odd, acc_ref):
    i = pl.program_id(1)
    n = pl.num_programs(1)
    nh, tq = q_ref.shape[1], q_ref.shape[-1]
    width = nh * tq
    qt = jnp.concatenate([q_ref[0, hh] for hh in range(nh)], axis=1)
    key = lax.broadcasted_iota(I32, (TOK, width), 0)
    qry = jnp.bitwise_and(lax.broadcasted_iota(I32, (TOK, width), 1), tq - 1)
    dist = key - qry

    def produce(k_ref, ok, s_ref):
        s = jnp.dot(k_ref[0], qt, preferred_element_type=F32)
        if ok is not None:
            s = jnp.where(ok, s, NEG_INF)
        s_ref[...] = s
        return jnp.max(s, axis=0, keepdims=True)

    def consume(s_ref, mx, v_ref, m):
        m_new = jnp.maximum(m, mx)
        p = jnp.exp2(s_ref[...] - m_new).astype(BF16)
        pv = [jnp.dot(v_ref[0, 0, VROWS * g: VROWS * g + VROWS, :], p[:, 2 * tq * g: 2 * tq * g + 2 * tq],
                      preferred_element_type=F32) for g in range(A_KV)]
        acc_ref[...] = jnp.exp2(m - m_new) * acc_ref[...] + jnp.concatenate(pv, axis=1)
        return m_new

    acc_ref[...] = jnp.zeros(acc_ref.shape, F32)
    mx_ctx = produce(kc_ref, None, s_even)
    mx_cur = produce(kq_ref, jnp.abs(dist) <= WINDOW, s_odd)
    m = consume(s_even, mx_ctx, vc_ref, jnp.full((1, width), NEG_INF, F32))
    mx_prev = produce(kp_ref, (dist >= TOK - WINDOW) & (i > 0), s_even)
    m = consume(s_odd, mx_cur, vq_ref, m)
    mx_next = produce(kn_ref, (dist <= WINDOW - TOK) & (i < n - 1), s_odd)
    m = consume(s_even, mx_prev, vp_ref, m)
    m = consume(s_odd, mx_next, vn_ref, m)

    sk = jnp.concatenate([jnp.full((1, tq), sink_ref[hh] * LOG2E, F32) for hh in range(nh)], axis=1)
    m2 = jnp.maximum(m, sk)
    a = jnp.exp2(m - m2)
    acc = acc_ref[...]
    out = acc[:HEAD_DIM] * a / (acc[HEAD_DIM:HEAD_DIM + 1] * a + jnp.exp2(sk - m2))
    for hh in range(nh):
        o_ref[0, hh] = out[:, tq * hh: tq * hh + tq]


def _window_attn(q, k_ctx, v_ctx, k_lat, v_lat, sink):
    b, nh, _, s = q.shape
    n = s // TOK
    sc = k_ctx.shape[1]
    prev = lambda i: jnp.maximum(i - 1, 0)
    nxt = lambda i: jnp.minimum(i + 1, n - 1)
    kspec = lambda f: pl.BlockSpec((1, TOK, LANES), lambda bi, i: (bi, f(i), 0))
    vspec = lambda f: pl.BlockSpec((1, 1, 2 * VROWS, TOK), lambda bi, i: (bi, f(i), 0, 0))
    same = lambda i: i
    return pl.pallas_call(
        _window_body,
        grid=(b, n),
        in_specs=[pl.BlockSpec(memory_space=pltpu.SMEM),
                  pl.BlockSpec((1, nh, QPAD, TOK), lambda bi, i: (bi, 0, 0, i)),
                  pl.BlockSpec((1, sc, LANES), lambda bi, i: (bi, 0, 0)),
                  pl.BlockSpec((1, 1, 2 * VROWS, sc), lambda bi, i: (bi, 0, 0, 0)),
                  kspec(prev), kspec(same), kspec(nxt), vspec(prev), vspec(same), vspec(nxt)],
        out_specs=pl.BlockSpec((1, nh, HEAD_DIM, TOK), lambda bi, i: (bi, 0, 0, i)),
        out_shape=jax.ShapeDtypeStruct((b, nh, HEAD_DIM, s), F32),
        scratch_shapes=[pltpu.VMEM((TOK, nh * TOK), F32), pltpu.VMEM((TOK, nh * TOK), F32),
                        pltpu.VMEM((VROWS, nh * TOK), F32)],
        compiler_params=_params("arbitrary", "arbitrary"),
        name="window_attn",
    )(sink, q, k_ctx, v_ctx, k_lat, k_lat, k_lat, v_lat, v_lat, v_lat)


def _outproj_body(oa_ref, ob_ref, oc_ref, od_ref, lam_ref, subln_ref, w_ref, x_ref, gt_ref,
                  g_ref, sh_ref, sc_ref, wr_ref, xo_ref, h_ref, lg_ref, *, lam_init):
    lp = lam_ref[...]
    lam = (jnp.exp(jnp.sum(lp[0:1] * lp[1:2], axis=1, keepdims=True))
           - jnp.exp(jnp.sum(lp[2:3] * lp[3:4], axis=1, keepdims=True)) + lam_init)
    parts = [oa_ref[0, hh] for hh in range(4)] + [ob_ref[0, hh] for hh in range(4)]
    for hh in range(C_HEADS):
        o = oc_ref[0, 2 * hh] - lam * oc_ref[0, 2 * hh + 1]
        parts.append(_rms_t(o, subln_ref[...]) * (1.0 - lam_init))
    parts += [od_ref[0, hh] for hh in range(4)]
    ot = jnp.concatenate(parts, axis=0).astype(BF16)
    out = lax.dot_general(ot, w_ref[...], TN_DIMS, preferred_element_type=F32)
    xn = x_ref[0] + gt_ref[0] * out
    xo_ref[0] = xn
    h = _modulate(xn, g_ref[...], sh_ref[0], sc_ref[0])
    h_hi = h.astype(BF16)
    h_ref[0] = h_hi
    h_lo = (h - h_hi.astype(F32)).astype(BF16)
    both = lax.dot_general(wr_ref[...], h_hi, NT_DIMS, preferred_element_type=F32)
    low = lax.dot_general(wr_ref[0:N_EXPERTS], h_lo, NT_DIMS, preferred_element_type=F32)
    lg_ref[0] = both[:N_EXPERTS] + both[N_EXPERTS:] + low


def _outproj(oa, ob, oc, od, lam_p, subln, w_out, x, gate, g_ffn, shift, scale, w_router_t, lam_init):
    b, s, d = x.shape
    tm = min(2 * TOK, s)
    ospec = lambda nh: pl.BlockSpec((1, nh, HEAD_DIM, tm), lambda bi, i: (bi, 0, 0, i))
    full = lambda a: pl.BlockSpec(a.shape, lambda bi, i: (0,) * a.ndim)
    row = pl.BlockSpec((1, 1, d), lambda bi, i: (bi, 0, 0))
    return pl.pallas_call(
        functools.partial(_outproj_body, lam_init=lam_init),
        grid=(b, s // tm),
        in_specs=[ospec(4), ospec(4), ospec(8), ospec(4), full(lam_p), full(subln), full(w_out),
                  pl.BlockSpec((1, tm, d), lambda bi, i: (bi, i, 0)), row, full(g_ffn), row, row,
                  full(w_router_t)],
        out_specs=[pl.BlockSpec((1, tm, d), lambda bi, i: (bi, i, 0)),
                   pl.BlockSpec((1, tm, d), lambda bi, i: (bi, i, 0)),
                   pl.BlockSpec((1, N_EXPERTS, tm), lambda bi, i: (bi, 0, i))],
        out_shape=[jax.ShapeDtypeStruct((b, s, d), F32), jax.ShapeDtypeStruct((b, s, d), BF16),
                   jax.ShapeDtypeStruct((b, N_EXPERTS, s), F32)],
        compiler_params=_params("arbitrary", "arbitrary"),
        name="outproj",
    )(oa, ob, oc, od, lam_p, subln, w_out, x, gate, g_ffn, shift, scale, w_router_t)


def _router_body(lg_ref, slot_ref, aff_ref, st_ref, *, cap, nblk):
    lg = lg_ref[0]
    ex = jnp.exp(lg - jnp.max(lg, axis=0, keepdims=True))
    aff = ex / jnp.sum(ex, axis=0, keepdims=True)
    aff_ref[0] = aff

    def search(_, bounds):
        lo, hi = bounds
        mid = (lo + hi) * 0.5
        enough = jnp.sum((aff >= mid).astype(I32), axis=1, keepdims=True) >= cap
        return jnp.where(enough, mid, lo), jnp.where(enough, hi, mid)

    lo, hi = lax.fori_loop(0, BISECT_STEPS, search,
                           (jnp.zeros((N_EXPERTS, 1), F32), jnp.full((N_EXPERTS, 1), 2.0, F32)))
    gt = aff >= hi
    eq = (aff >= lo) & jnp.logical_not(gt)
    need = (cap - jnp.sum(gt.astype(I32), axis=1, keepdims=True)).astype(F32)
    tri = (lax.broadcasted_iota(I32, (TOK, TOK), 0) <= lax.broadcasted_iota(I32, (TOK, TOK), 1)).astype(BF16)
    lane = lax.broadcasted_iota(I32, (N_EXPERTS, LANES), 1)
    eq_seen = jnp.zeros((N_EXPERTS, 1), F32)
    base = jnp.zeros((N_EXPERTS, 1), I32)
    starts = jnp.zeros((N_EXPERTS, LANES), I32)
    for j in range(nblk):
        cols = slice(TOK * j, TOK * j + TOK)
        eq_c = eq[:, cols]
        eq_cum = jnp.dot(eq_c.astype(BF16), tri, preferred_element_type=F32) + eq_seen
        eq_seen = eq_seen + jnp.sum(eq_c.astype(F32), axis=1, keepdims=True)
        sel = gt[:, cols] | (eq_c & (eq_cum <= need))
        cum = jnp.dot(sel.astype(BF16), tri, preferred_element_type=F32).astype(I32)
        slot_ref[0, :, cols] = jnp.where(sel, base + cum - 1, -1)
        starts = jnp.where(lane == j, base, starts)
        cnt = jnp.sum(sel.astype(I32), axis=1, keepdims=True)
        base = base + jnp.bitwise_and(cnt + (SUBLANES - 1), -SUBLANES)
    st_ref[0] = jnp.where(lane == nblk, base, starts)


def _router(logits_t, cap):
    b, e, s = logits_t.shape
    nblk = s // TOK
    return pl.pallas_call(
        functools.partial(_router_body, cap=cap, nblk=nblk),
        grid=(b,),
        in_specs=[pl.BlockSpec((1, e, s), lambda bi: (bi, 0, 0))],
        out_specs=[pl.BlockSpec((1, e, s), lambda bi: (bi, 0, 0)),
                   pl.BlockSpec((1, e, s), lambda bi: (bi, 0, 0)),
                   pl.BlockSpec((1, e, LANES), lambda bi: (bi, 0, 0))],
        out_shape=[jax.ShapeDtypeStruct((b, e, s), I32), jax.ShapeDtypeStruct((b, e, s), F32),
                   jax.ShapeDtypeStruct((b, e, LANES), I32)],
        compiler_params=_params("arbitrary"),
        name="router",
    )(logits_t)


def _onehot_t(slot_ref, first, win):
    rows = lax.broadcasted_iota(I32, (win, TOK), 0)
    return [rows == (slot_ref[0, e:e + 1, :] - first[e]) for e in range(N_EXPERTS)]


def _dispatch_body(st_ref, h_ref, slot_ref, aff_ref, xs_ref, stage, sems, count, *, win):
    bi, j = pl.program_id(0), pl.program_id(1)
    d = h_ref.shape[-1]

    @pl.when((bi == 0) & (j == 0))
    def _():
        count[0] = 0

    rows_alloc = xs_ref.shape[2]
    last_block = j == pl.num_programs(1) - 1
    st = [st_ref[bi, e, j] for e in range(N_EXPERTS)]
    ends = [jnp.where(last_block, rows_alloc, st_ref[bi, e, j + 1]) for e in range(N_EXPERTS)]
    width = ends[0] - st[0]
    for e in range(1, N_EXPERTS):
        width = jnp.maximum(width, ends[e] - st[e])
    rounds = jnp.maximum((width + win - 1) // win, 1)

    def copies(buf, first):
        return [pltpu.make_async_copy(stage.at[buf, pl.ds(e * win, win), :],
                                      xs_ref.at[bi, e, pl.ds(pl.multiple_of(first[e], SUBLANES), win), :],
                                      sems.at[e]) for e in range(N_EXPERTS)]

    def one_round(r, carry):
        k = count[0]
        buf = k % 2
        first = [jnp.minimum(st[e] + r * win, rows_alloc - win) for e in range(N_EXPERTS)]
        hot = _onehot_t(slot_ref, first, win)
        p = jnp.concatenate(hot, axis=0).astype(BF16)
        stage[buf, :, 0:d] = jnp.dot(p, h_ref[0], preferred_element_type=F32)
        gates = [jnp.sum(jnp.where(hot[e], aff_ref[0, e:e + 1, :], 0.0), axis=1, keepdims=True)
                 for e in range(N_EXPERTS)]
        stage[buf, :, d:d + LANES] = jnp.broadcast_to(jnp.concatenate(gates, axis=0), (N_EXPERTS * win, LANES))

        @pl.when(k == 0)
        def _():
            for c in copies(buf, first):
                c.start()

        @pl.when(k > 0)
        def _():
            for done, c in zip(copies(1 - buf, st), copies(buf, first)):
                done.wait()
                c.start()

        count[0] = k + 1
        return carry

    lax.fori_loop(0, rounds, one_round, 0)

    @pl.when((bi == pl.num_programs(0) - 1) & (j == pl.num_programs(1) - 1))
    def _():
        for c in copies(0, st):
            c.wait()


def _dispatch(starts, h, slot, aff, rows_alloc, win):
    b, s, d = h.shape
    nblk = s // TOK
    e = N_EXPERTS
    return pl.pallas_call(
        functools.partial(_dispatch_body, win=win),
        grid_spec=pltpu.PrefetchScalarGridSpec(
            num_scalar_prefetch=1, grid=(b, nblk),
            in_specs=[pl.BlockSpec((1, TOK, d), lambda bi, j, st: (bi, j, 0)),
                      pl.BlockSpec((1, e, TOK), lambda bi, j, st: (bi, 0, j)),
                      pl.BlockSpec((1, e, TOK), lambda bi, j, st: (bi, 0, j))],
            out_specs=pl.BlockSpec(memory_space=pl.ANY),
            scratch_shapes=[pltpu.VMEM((2, e * win, d + LANES), F32),
                            pltpu.SemaphoreType.DMA((e,)),
                            pltpu.SMEM((1,), I32)]),
        out_shape=jax.ShapeDtypeStruct((b, e, rows_alloc, d + LANES), F32),
        compiler_params=_params("arbitrary", "arbitrary"),
        name="dispatch",
    )(starts, h, slot, aff)


def _ffn_body(*refs, layer, sets):
    n = len(sets)
    st_refs, xs_refs = refs[:n], refs[n:2 * n]
    wg_in, wu_in, wd_in = refs[2 * n: 2 * n + 3]
    y_refs = refs[2 * n + 3: 3 * n + 3]
    bufs, sems = refs[3 * n + 3: 3 * n + 6], refs[3 * n + 6]
    wg_ref, wu_ref, wd_ref = refs[3 * n + 7: 3 * n + 10]
    e, bi, i = pl.program_id(0), pl.program_id(1), pl.program_id(2)
    d = y_refs[0].shape[-1]

    def fetch(expert, slot):
        return [pltpu.make_async_copy(w.at[layer, expert], buf.at[slot], sems.at[slot, k])
                for k, (w, buf) in enumerate(zip((wg_in, wu_in, wd_in), bufs))]

    @pl.when((bi == 0) & (i == 0))
    def _():
        slot = e % 2

        @pl.when(e == 0)
        def _():
            for c in fetch(0, 0):
                c.start()

        for c in fetch(e, slot):
            c.wait()

        @pl.when(e + 1 < pl.num_programs(0))
        def _():
            for c in fetch(e + 1, 1 - slot):
                c.start()

        for out, buf in zip((wg_ref, wu_ref, wd_ref), bufs):
            out[...] = buf[slot].astype(BF16)

    for (first, count, nblk), st_ref, xs_ref, y_ref in zip(sets, st_refs, xs_refs, y_refs):
        tm = xs_ref.shape[2]
        used = st_ref[bi, e, nblk]
        base = (i - first) * tm
        mine = (i >= first) & (i < first + count)

        def run(rows, xs_ref=xs_ref, y_ref=y_ref, tm=tm, used=used, base=base):
            valid = (base + lax.broadcasted_iota(I32, (rows, 1), 0)) < used
            xa = xs_ref[0, 0, 0:rows, :]
            x = jnp.where(valid, xa[:, 0:d], 0.0).astype(BF16)
            gate = jnp.where(valid, xa[:, d:d + 1], 0.0)
            a = jnp.dot(x, wg_ref[...], preferred_element_type=F32)
            u = jnp.dot(x, wu_ref[...], preferred_element_type=F32)
            mid = (a * jax.nn.sigmoid(a) * u).astype(BF16)
            y_ref[0, 0, 0:rows, :] = jnp.dot(mid, wd_ref[...], preferred_element_type=F32) * gate
            if rows < tm:
                y_ref[0, 0, rows:tm, :] = jnp.zeros((tm - rows, d), F32)

        half = tm // 2 if tm % (4 * SUBLANES) == 0 else 0
        left = used - base
        pl.when(mine & (left > half))(functools.partial(run, tm))
        if half:
            pl.when(mine & (left > 0) & (left <= half))(functools.partial(run, half))

        @pl.when(mine & (left <= 0))
        def _(y_ref=y_ref):
            y_ref[0, 0] = jnp.zeros(y_ref.shape[2:], F32)


def _ffn(routed, wg, wu, wd, layer):
    b, e = routed[0]["xs"].shape[:2]
    d, f = wg.shape[-2:]
    sets, first = [], 0
    for r in routed:
        count = r["rows"] // r["tm"]
        sets.append((first, count, r["nblk"]))
        first += count

    def tile_spec(r, first_tile, count, width):
        def index(ei, bi, i, *st):
            return bi, ei, jnp.clip(i - first_tile, 0, count - 1), 0
        return pl.BlockSpec((1, 1, r["tm"], width), index)

    in_specs = [tile_spec(r, s[0], s[1], d + LANES) for r, s in zip(routed, sets)]
    out_specs = [tile_spec(r, s[0], s[1], d) for r, s in zip(routed, sets)]
    out_shape = [jax.ShapeDtypeStruct((b, e, r["rows"], d), F32) for r in routed]
    scratch = ([pltpu.VMEM((2,) + w.shape[2:], F32) for w in (wg, wu, wd)] + [pltpu.SemaphoreType.DMA((2, 3))]
               + [pltpu.VMEM(w.shape[2:], BF16) for w in (wg, wu, wd)])
    return pl.pallas_call(
        functools.partial(_ffn_body, layer=layer, sets=tuple(sets)),
        grid_spec=pltpu.PrefetchScalarGridSpec(
            num_scalar_prefetch=len(routed), grid=(e, b, first),
            in_specs=in_specs + [pl.BlockSpec(memory_space=pl.ANY)] * 3,
            out_specs=out_specs, scratch_shapes=scratch),
        out_shape=out_shape,
        compiler_params=_params("arbitrary", "arbitrary", "arbitrary"),
        name="expert_ffn",
    )(*[r["starts"] for r in routed], *[r["xs"] for r in routed], wg, wu, wd)


def _combine_body(st_ref, slot_ref, x_ref, gt_ref, gf_ref, y_ref, o_ref, ybuf, sems, *, win, rows, final_norm):
    bi, j = pl.program_id(0), pl.program_id(1)
    nj = pl.num_programs(1)
    d = x_ref.shape[-1]
    step = bi * nj + j
    cur = step % 2
    st = [st_ref[bi, e, j] for e in range(N_EXPERTS)]
    width = st_ref[bi, 0, j + 1] - st[0]
    for e in range(1, N_EXPERTS):
        width = jnp.maximum(width, st_ref[bi, e, j + 1] - st[e])
    rounds = (width + win - 1) // win

    def window(b_, j_, r):
        lo = [st_ref[b_, e, j_] + r * win for e in range(N_EXPERTS)]
        return lo, [jnp.minimum(lo[e], rows - win) for e in range(N_EXPERTS)]

    def copies(b_, first, buf):
        return [pltpu.make_async_copy(y_ref.at[b_, e, pl.ds(pl.multiple_of(first[e], SUBLANES), win), :],
                                      ybuf.at[buf, pl.ds(e * win, win), :], sems.at[buf, e])
                for e in range(N_EXPERTS)]

    def scatter(lo, first, buf, acc):
        hot = _onehot_t(slot_ref, first, win)
        hot = [hot[e] & (slot_ref[0, e:e + 1, :] >= lo[e]) for e in range(N_EXPERTS)]
        p = jnp.concatenate(hot, axis=0).astype(BF16)
        return acc + lax.dot_general(p, ybuf[buf].astype(BF16), TN_DIMS, preferred_element_type=F32)

    lo0, first0 = window(bi, j, 0)

    @pl.when(step == 0)
    def _():
        for c in copies(bi, first0, cur):
            c.start()

    @pl.when(step < pl.num_programs(0) * nj - 1)
    def _():
        wrap = j == nj - 1
        b_next, j_next = jnp.where(wrap, bi + 1, bi), jnp.where(wrap, 0, j + 1)
        for c in copies(b_next, window(b_next, j_next, 0)[1], 1 - cur):
            c.start()

    for c in copies(bi, first0, cur):
        c.wait()
    acc = scatter(lo0, first0, cur, jnp.zeros((TOK, d), F32))

    def later_round(r, acc):
        lo, first = window(bi, j, r)
        for c in copies(bi, first, 2):
            c.start()
        for c in copies(bi, first, 2):
            c.wait()
        return scatter(lo, first, 2, acc)

    acc = lax.fori_loop(1, rounds, later_round, acc)
    out = x_ref[0] + gt_ref[0] * acc
    if final_norm:
        ms = jnp.mean(out * out, axis=-1, keepdims=True)
        out = out * lax.rsqrt(ms + EPS) * gf_ref[...]
    o_ref[0] = out


def _combine(starts, slot, x, gate, g_final, y, win, final_norm):
    b, s, d = x.shape
    e = N_EXPERTS
    rows = y.shape[2]
    return pl.pallas_call(
        functools.partial(_combine_body, win=win, rows=rows, final_norm=final_norm),
        grid_spec=pltpu.PrefetchScalarGridSpec(
            num_scalar_prefetch=1, grid=(b, s // TOK),
            in_specs=[pl.BlockSpec((1, e, TOK), lambda bi, j, st: (bi, 0, j)),
                      pl.BlockSpec((1, TOK, d), lambda bi, j, st: (bi, j, 0)),
                      pl.BlockSpec((1, 1, d), lambda bi, j, st: (bi, 0, 0)),
                      pl.BlockSpec((1, d), lambda bi, j, st: (0, 0)),
                      pl.BlockSpec(memory_space=pl.ANY)],
            out_specs=pl.BlockSpec((1, TOK, d), lambda bi, j, st: (bi, j, 0)),
            scratch_shapes=[pltpu.VMEM((3, e * win, d), F32), pltpu.SemaphoreType.DMA((3, e))]),
        out_shape=jax.ShapeDtypeStruct((b, s, d), F32),
        compiler_params=_params("arbitrary", "arbitrary"),
        name="combine",
    )(starts, slot, x, gate, g_final, y)


def _route(h, logits_t):
    n = h.shape[1]
    nblk = n // TOK
    cap = CAPACITY_FACTOR * n // N_EXPERTS
    win = min(SLOT_WINDOW, cap)
    rows = cap + SUBLANES * nblk
    tm = min(TOK, rows)
    rows = -(-rows // tm) * tm
    slot, aff, starts = _router(logits_t, cap)
    xs = _dispatch(starts, h, slot, aff, rows + win, win)
    return dict(slot=slot, starts=starts, xs=xs, win=win, rows=rows, tm=tm, nblk=nblk)


def _rope_tables(n_tokens, rot_dim):
    rows = n_tokens // GRID_W
    row = jnp.repeat(jnp.arange(rows), GRID_W)
    col = jnp.tile(jnp.arange(GRID_W), rows)
    quarter = rot_dim // 4
    inv_freq = ROPE_THETA ** (-jnp.arange(quarter, dtype=F32) / quarter)
    ang = jnp.concatenate([inv_freq[:, None] * row[None, :], inv_freq[:, None] * col[None, :]], axis=0)
    return jnp.cos(ang), jnp.sin(ang)


def _identity_tables(n_tokens, rot_dim):
    return jnp.ones((rot_dim // 2, n_tokens), F32), jnp.zeros((rot_dim // 2, n_tokens), F32)


def kernel(x, c, ctx, c_ctx, w_mod, b_mod, g_attn, g_ffn, w_in, a_sink, b_q_norm, b_w_uq, b_kv_norm, b_w_ukv, c_lambda, c_subln, d_q_norm, d_k_norm, w_out, w_router, w_gate, w_up, w_down, g_final):
    b, s, d = x.shape
    depth = w_mod.shape[0]
    n_ctx = ctx.shape[1]
    rope64, rope32 = _rope_tables(s, HEAD_DIM), _rope_tables(s, C_QK)
    id64, id32 = _identity_tables(n_ctx, HEAD_DIM), _identity_tables(n_ctx, C_QK)

    cond = jnp.zeros((SUBLANES, d), F32).at[:b].set(c).at[b].set(c_ctx)
    mod = _adaln(cond, w_mod, b_mod)

    col = lambda v: v.reshape(-1, 1)
    xl, xc = x, ctx
    for l in range(depth):
        last = l == depth - 1
        lam_init = 0.8 - 0.6 * math.exp(-0.3 * l)
        m6 = mod[l].reshape(SUBLANES, 6, d)
        lat = [m6[:b, k][:, None, :] for k in range(6)]
        cx = [jnp.broadcast_to(m6[b, k][None, None, :], (b, 1, d)) for k in range(6)]
        w_in_t = w_in[l].T.astype(BF16)
        wuq_t = b_w_uq[l].T.astype(BF16)
        wukv_t = b_w_ukv[l].T.astype(BF16)
        w_out_b = w_out[l].astype(BF16)
        w_router_t = w_router[l].T
        w_router_hi = w_router_t.astype(BF16)
        w_router_t = jnp.concatenate([w_router_hi, (w_router_t - w_router_hi.astype(F32)).astype(BF16)], axis=0)
        g_a, g_f = g_attn[l][None, :], g_ffn[l][None, :]
        small = (col(b_q_norm[l]), wuq_t, col(b_kv_norm[l]), wukv_t, col(d_q_norm[l]), col(d_k_norm[l]))

        pl_ = _proj(xl, g_a, lat[0], lat[1], w_in_t, rope64, rope32, *small)
        pc_ = _proj(xc, g_a, cx[0], cx[1], w_in_t, id64, id32, *small)
        qa, ka, va, qb, kb, vb, qc, kc, vc, qd, kd, vd = pl_
        qa_c, ka_c, va_c, qb_c, kb_c, vb_c, qc_c, kc_c, vc_c, qd_c, kd_c, vd_c = pc_

        zero = lambda h: 0
        oa = _window_attn(qa, ka_c, va_c, ka, va, a_sink[l])
        ob = _dense_attn(qb, kb_c, vb_c, kb, vb, lambda h: h, lambda h: h)
        oc = _dense_attn(qc, kc_c, vc_c, kc, vc, lambda h: h // 4, lambda h: h // 2)
        od = _dense_attn(qd, kd_c, vd_c, kd, vd, zero, lambda h: h // 2)
        outproj = functools.partial(_outproj, lam_p=c_lambda[l], subln=col(c_subln[l]), w_out=w_out_b,
                                    g_ffn=g_f, w_router_t=w_router_t, lam_init=lam_init)
        xl, hl, lg = outproj(oa, ob, oc, od, x=xl, gate=lat[2], shift=lat[3], scale=lat[4])
        routed = [_route(hl, lg)]
        if not last:
            oa = _dense_attn(qa_c, ka_c, va_c, None, None, zero, lambda h: h // 2, sink=a_sink[l])
            ob = _dense_attn(qb_c, kb_c, vb_c, None, None, lambda h: h, lambda h: h)
            oc = _dense_attn(qc_c, kc_c, vc_c, None, None, lambda h: h // 4, lambda h: h // 2)
            od = _dense_attn(qd_c, kd_c, vd_c, None, None, zero, lambda h: h // 2)
            xc, hc, lgc = outproj(oa, ob, oc, od, x=xc, gate=cx[2], shift=cx[3], scale=cx[4])
            routed.append(_route(hc, lgc))
        ys = _ffn(routed, w_gate, w_up, w_down, l)
        combine = lambda r, y, x_, gate, norm: _combine(r["starts"], r["slot"], x_, gate, g_final[None, :], y,
                                                        r["win"], norm)
        xl = combine(routed[0], ys[0], xl, lat[5], last)
        if not last:
            xc = combine(routed[1], ys[1], xc, cx[5], False)
    return xl
```

```python
import functools
import math

import jax
import jax.numpy as jnp
from jax import lax
from jax.experimental import pallas as pl
from jax.experimental.pallas import tpu as pltpu

F32, BF16, I32 = jnp.float32, jnp.bfloat16, jnp.int32
HIGHEST = lax.Precision.HIGHEST

SUBLANES = 8
LANES = 128
BF16_ROWS = 16

GRID_W = 64
HEAD_DIM = 64
WINDOW = 128
EPS = 1e-6
ROPE_THETA = 10000.0
NEG_INF = -1e30
A_HEADS, A_KV = 4, 2
B_HEADS, B_Q_RANK, B_KV_RANK, B_NOPE, B_ROPE, B_V = 4, 192, 128, 64, 32, 64
C_HEADS, C_QK, C_V = 4, 32, 64
D_HEADS, D_KV = 4, 2
N_EXPERTS = 16
CAPACITY_FACTOR = 2

A_COLS = A_HEADS * HEAD_DIM + 2 * A_KV * HEAD_DIM
B_COLS = B_Q_RANK + B_KV_RANK + B_ROPE
C_COLS = 4 * C_HEADS * C_QK + C_HEADS * C_V
D_COLS = D_HEADS * HEAD_DIM + 2 * D_KV * HEAD_DIM
A_OFF, B_OFF, C_OFF, D_OFF = 0, A_COLS, A_COLS + B_COLS, A_COLS + B_COLS + C_COLS

BISECT_STEPS = 152
TOK = 256
QPAD = 128
VROWS = HEAD_DIM + 16
TQ = 1024
ATTN_UNROLL = 8
WEIGHT_PIECES = 2
SLOT_WINDOW = 48
LOG2E = math.log2(math.e)

NT_DIMS = (((1,), (1,)), ((), ()))
TN_DIMS = (((0,), (0,)), ((), ()))


def _params(*sem):
    return pltpu.CompilerParams(dimension_semantics=sem, vmem_limit_bytes=56 * 1024 * 1024)


def _adaln_body(c_ref, w_ref, b_ref, o_ref):
    c = c_ref[...]
    s = c * jax.nn.sigmoid(c)
    o_ref[0] = jnp.dot(s, w_ref[0], precision=HIGHEST, preferred_element_type=F32) + b_ref[0]


def _adaln(cond, w_mod, b_mod):
    depth, d, n = w_mod.shape
    tn = n // 4
    return pl.pallas_call(
        _adaln_body,
        grid=(depth, n // tn),
        in_specs=[pl.BlockSpec((SUBLANES, d), lambda l, j: (0, 0)),
                  pl.BlockSpec((1, d, tn), lambda l, j: (l, 0, j)),
                  pl.BlockSpec((1, 1, tn), lambda l, j: (l, 0, j))],
        out_specs=pl.BlockSpec((1, SUBLANES, tn), lambda l, j: (l, 0, j)),
        out_shape=jax.ShapeDtypeStruct((depth, SUBLANES, n), F32),
        compiler_params=_params("arbitrary", "arbitrary"),
        name="adaln",
    )(cond, w_mod, b_mod.reshape(depth, 1, n))


def _rope_t(xt, cos, sin):
    half = xt.shape[0] // 2
    x1, x2 = xt[:half], xt[half:]
    return jnp.concatenate([x1 * cos - x2 * sin, x1 * sin + x2 * cos], axis=0)


def _rms_t(xt, g):
    ms = jnp.mean(xt * xt, axis=0, keepdims=True)
    return xt * lax.rsqrt(ms + EPS) * g


def _modulate(x, g, shift, scale):
    ms = jnp.mean(x * x, axis=-1, keepdims=True)
    return x * lax.rsqrt(ms + EPS) * g * (1.0 + scale) + shift


def _proj_body(x_ref, g_ref, sh_ref, sc_ref, w_ref, c64_ref, s64_ref, c32_ref, s32_ref,
               bqn_ref, wuq_ref, bkvn_ref, wukv_ref, dqn_ref, dkn_ref,
               qa_ref, ka_ref, va_ref, qb_ref, kb_ref, vb_ref,
               qc_ref, kc_ref, vc_ref, qd_ref, kd_ref, vd_ref):
    tm = x_ref.shape[1]
    h = _modulate(x_ref[0], g_ref[...], sh_ref[0], sc_ref[0]).astype(BF16)
    pt = lax.dot_general(w_ref[...], h, NT_DIMS, preferred_element_type=F32)
    c64, s64, c32, s32 = c64_ref[...], s64_ref[...], c32_ref[...], s32_ref[...]
    z64 = jnp.zeros((64, tm), F32)
    z32 = jnp.zeros((32, tm), F32)
    scale64 = HEAD_DIM ** -0.5 * LOG2E
    scale_b = (B_NOPE + B_ROPE) ** -0.5 * LOG2E
    scale_c = C_QK ** -0.5 * LOG2E
    ones = jnp.ones((VROWS - HEAD_DIM, TOK), BF16)

    def put_values(v_ref, hh, vt):
        for c in range(tm // TOK):
            v_ref[0, c, VROWS * hh: VROWS * hh + HEAD_DIM] = vt[:, TOK * c: TOK * c + TOK].astype(BF16)
            v_ref[0, c, VROWS * hh + HEAD_DIM: VROWS * hh + VROWS] = ones

    def gqa(off, q_ref, k_ref, v_ref, qn, kn):
        for hh in range(4):
            q = pt[off + 64 * hh: off + 64 * hh + 64]
            if qn is not None:
                q = _rms_t(q, qn)
            q = _rope_t(q, c64, s64) * scale64
            q_ref[0, hh] = (jnp.concatenate([q, z64], axis=0) if hh // 2 == 0
                            else jnp.concatenate([z64, q], axis=0)).astype(BF16)
        ks = []
        for j in range(2):
            k = pt[off + 256 + 64 * j: off + 256 + 64 * j + 64]
            if kn is not None:
                k = _rms_t(k, kn)
            ks.append(_rope_t(k, c64, s64))
        k_ref[0] = jnp.concatenate(ks, axis=0).T.astype(BF16)
        for j in range(2):
            put_values(v_ref, j, pt[off + 384 + 64 * j: off + 384 + 64 * j + 64])

    gqa(A_OFF, qa_ref, ka_ref, va_ref, None, None)
    gqa(D_OFF, qd_ref, kd_ref, vd_ref, dqn_ref[...], dkn_ref[...])

    cq = _rms_t(pt[B_OFF: B_OFF + B_Q_RANK], bqn_ref[...]).astype(BF16)
    ckv = _rms_t(pt[B_OFF + B_Q_RANK: B_OFF + B_Q_RANK + B_KV_RANK], bkvn_ref[...]).astype(BF16)
    kr = _rope_t(pt[B_OFF + B_Q_RANK + B_KV_RANK: B_OFF + B_COLS], c32, s32)
    qt = jnp.dot(wuq_ref[...], cq, preferred_element_type=F32)
    kvt = jnp.dot(wukv_ref[...], ckv, preferred_element_type=F32)
    kparts = []
    dq = B_NOPE + B_ROPE
    for hh in range(B_HEADS):
        qn_ = qt[dq * hh: dq * hh + B_NOPE]
        qr = _rope_t(qt[dq * hh + B_NOPE: dq * hh + dq], c32, s32)
        qb_ref[0, hh] = (jnp.concatenate([qn_, qr, z32], axis=0) * scale_b).astype(BF16)
        kparts.append(jnp.concatenate([kvt[128 * hh: 128 * hh + B_NOPE], kr, z32], axis=0))
        put_values(vb_ref, hh, kvt[128 * hh + B_NOPE: 128 * hh + 128])
    kb_ref[0] = jnp.concatenate(kparts, axis=0).T.astype(BF16)

    for j in range(2 * C_HEADS):
        q = _rope_t(pt[C_OFF + 32 * j: C_OFF + 32 * j + 32], c32, s32) * scale_c
        pieces = [z32, z32, z32, z32]
        pieces[j % 4] = q
        qc_ref[0, j] = jnp.concatenate(pieces, axis=0).astype(BF16)
    kc = [_rope_t(pt[C_OFF + 256 + 32 * j: C_OFF + 256 + 32 * j + 32], c32, s32) for j in range(2 * C_HEADS)]
    kc_ref[0] = jnp.concatenate(kc, axis=0).T.astype(BF16)
    for hh in range(C_HEADS):
        put_values(vc_ref, hh, pt[C_OFF + 512 + 64 * hh: C_OFF + 512 + 64 * hh + 64])


def _proj(x, g, shift, scale, w_in_t, rope64, rope32, bqn, wuq_t, bkvn, wukv_t, dqn, dkn):
    b, s, d = x.shape
    tm = min(2 * TOK, s)
    n = s // tm
    full = lambda a: pl.BlockSpec(a.shape, lambda bi, i: (0,) * a.ndim)
    q_spec = lambda nh: pl.BlockSpec((1, nh, QPAD, tm), lambda bi, i: (bi, 0, 0, i))
    k_spec = lambda w: pl.BlockSpec((1, tm, w), lambda bi, i: (bi, i, 0))
    v_spec = lambda r: pl.BlockSpec((1, tm // TOK, r, TOK), lambda bi, i: (bi, i, 0, 0))
    q_shape = lambda nh: jax.ShapeDtypeStruct((b, nh, QPAD, s), BF16)
    k_shape = lambda w: jax.ShapeDtypeStruct((b, s, w), BF16)
    v_shape = lambda r: jax.ShapeDtypeStruct((b, s // TOK, r, TOK), BF16)
    tab = lambda t: pl.BlockSpec((t.shape[0], tm), lambda bi, i: (0, i))
    c64, s64 = rope64
    c32, s32 = rope32
    return pl.pallas_call(
        _proj_body,
        grid=(b, n),
        in_specs=[pl.BlockSpec((1, tm, d), lambda bi, i: (bi, i, 0)), full(g),
                  pl.BlockSpec((1, 1, d), lambda bi, i: (bi, 0, 0)),
                  pl.BlockSpec((1, 1, d), lambda bi, i: (bi, 0, 0)),
                  full(w_in_t), tab(c64), tab(s64), tab(c32), tab(s32),
                  full(bqn), full(wuq_t), full(bkvn), full(wukv_t), full(dqn), full(dkn)],
        out_specs=[q_spec(4), k_spec(128), v_spec(2 * VROWS),
                   q_spec(4), k_spec(512), v_spec(4 * VROWS),
                   q_spec(8), k_spec(256), v_spec(4 * VROWS),
                   q_spec(4), k_spec(128), v_spec(2 * VROWS)],
        out_shape=[q_shape(4), k_shape(128), v_shape(2 * VROWS),
                   q_shape(4), k_shape(512), v_shape(4 * VROWS),
                   q_shape(8), k_shape(256), v_shape(4 * VROWS),
                   q_shape(4), k_shape(128), v_shape(2 * VROWS)],
        compiler_params=_params("arbitrary", "arbitrary"),
        name="proj",
    )(x, g, shift, scale, w_in_t, c64, s64, c32, s32, bqn, wuq_t, bkvn, wukv_t, dqn, dkn)


def _dense_body(*refs, has_lat, has_sink, n_lat):
    refs = list(refs)
    sink_ref = refs.pop(0) if has_sink else None
    q_ref, kc_ref, vc_ref = refs[:3]
    kl_ref, vl_ref = (refs[3], refs[4]) if has_lat else (None, None)
    o_ref, s_even, s_odd, acc_ref = refs[-4:]
    tq = q_ref.shape[-1]
    qt = q_ref[0, 0]

    def produce(kblk, s_ref):
        s = jnp.dot(kblk, qt, preferred_element_type=F32)
        s_ref[...] = s
        return jnp.max(s, axis=0, keepdims=True)

    def consume(s_ref, mx, vblk, m):
        m_new = mx if m is None else jnp.maximum(m, mx)
        p = jnp.exp2(s_ref[...] - m_new).astype(BF16)
        pv = jnp.dot(vblk, p, preferred_element_type=F32)
        acc_ref[...] = pv if m is None else jnp.exp2(m - m_new) * acc_ref[...] + pv
        return m_new

    def lat_keys(i):
        return kl_ref[0, pl.ds(pl.multiple_of(i * TOK, TOK), TOK), :]

    mx_c = produce(kc_ref[0], s_odd)
    if has_lat:
        mx_e = produce(lat_keys(0), s_even)
    m = consume(s_odd, mx_c, vc_ref[0, 0], None)
    if has_lat:
        def body(j, carry):
            m, mx_e = carry
            mx_o = produce(lat_keys(2 * j + 1), s_odd)
            m = consume(s_even, mx_e, vl_ref[0, 2 * j], m)
            mx_e = produce(lat_keys(jnp.minimum(2 * j + 2, n_lat - 1)), s_even)
            m = consume(s_odd, mx_o, vl_ref[0, 2 * j + 1], m)
            return m, mx_e
        m, _ = lax.fori_loop(0, n_lat // 2, body, (m, mx_e), unroll=ATTN_UNROLL)
    acc = acc_ref[...]
    num, den = acc[:HEAD_DIM], acc[HEAD_DIM:HEAD_DIM + 1]
    if has_sink:
        sk = sink_ref[pl.program_id(1)] * LOG2E
        m2 = jnp.maximum(m, sk)
        a = jnp.exp2(m - m2)
        den = den * a + jnp.exp2(sk - m2)
        num = num * a
    o_ref[0, 0] = (num * (1.0 / den)).astype(o_ref.dtype)


def _dense_attn(q, k_ctx, v_ctx, k_lat, v_lat, kgroup, vhead, sink=None, out_dtype=BF16):
    b, nh, _, s = q.shape
    tq = min(TQ, s)
    has_lat = k_lat is not None
    has_sink = sink is not None
    sc = k_ctx.shape[1]
    in_specs = [pl.BlockSpec((1, 1, QPAD, tq), lambda bi, h, i: (bi, h, 0, i)),
                pl.BlockSpec((1, sc, LANES), lambda bi, h, i: (bi, 0, kgroup(h))),
                pl.BlockSpec((1, 1, VROWS, sc), lambda bi, h, i: (bi, 0, vhead(h), 0))]
    args = [q, k_ctx, v_ctx]
    n_lat = 0
    if has_lat:
        sl = k_lat.shape[1]
        n_lat = sl // TOK
        assert n_lat % 2 == 0
        in_specs += [pl.BlockSpec((1, sl, LANES), lambda bi, h, i: (bi, 0, kgroup(h))),
                     pl.BlockSpec((1, n_lat, VROWS, TOK), lambda bi, h, i: (bi, 0, vhead(h), 0))]
        args += [k_lat, v_lat]
    if has_sink:
        in_specs = [pl.BlockSpec(memory_space=pltpu.SMEM)] + in_specs
        args = [sink] + args
    return pl.pallas_call(
        functools.partial(_dense_body, has_lat=has_lat, has_sink=has_sink, n_lat=n_lat),
        grid=(b, nh, s // tq),
        in_specs=in_specs,
        out_specs=pl.BlockSpec((1, 1, HEAD_DIM, tq), lambda bi, h, i: (bi, h, 0, i)),
        out_shape=jax.ShapeDtypeStruct((b, nh, HEAD_DIM, s), out_dtype),
        scratch_shapes=[pltpu.VMEM((TOK, tq), F32), pltpu.VMEM((TOK, tq), F32), pltpu.VMEM((VROWS, tq), F32)],
        compiler_params=_params("arbitrary", "arbitrary", "arbitrary"),
        name="dense_attn",
    )(*args)


def _window_body(sink_ref, q_ref, kc_ref, vc_ref, kp_ref, kq_ref, kn_ref, vp_ref, vq_ref, vn_ref, o_ref,
                 s_even, s_odd, acc_ref):
    i = pl.program_id(1)
    n = pl.num_programs(1)
    nh, tq = q_ref.shape[1], q_ref.shape[-1]
    width = nh * tq
    qt = jnp.concatenate([q_ref[0, hh] for hh in range(nh)], axis=1)
    key = lax.broadcasted_iota(I32, (TOK, width), 0)
    qry = jnp.bitwise_and(lax.broadcasted_iota(I32, (TOK, width), 1), tq - 1)
    dist = key - qry

    def produce(k_ref, ok, s_ref):
        s = jnp.dot(k_ref[0], qt, preferred_element_type=F32)
        if ok is not None:
            s = jnp.where(ok, s, NEG_INF)
        s_ref[...] = s
        return jnp.max(s, axis=0, keepdims=True)

    def consume(s_ref, mx, v_ref, m):
        m_new = jnp.maximum(m, mx)
        p = jnp.exp2(s_ref[...] - m_new).astype(BF16)
        pv = [jnp.dot(v_ref[0, 0, VROWS * g: VROWS * g + VROWS, :], p[:, 2 * tq * g: 2 * tq * g + 2 * tq],
                      preferred_element_type=F32) for g in range(A_KV)]
        acc_ref[...] = jnp.exp2(m - m_new) * acc_ref[...] + jnp.concatenate(pv, axis=1)
        return m_new

    acc_ref[...] = jnp.zeros(acc_ref.shape, F32)
    mx_ctx = produce(kc_ref, None, s_even)
    mx_cur = produce(kq_ref, jnp.abs(dist) <= WINDOW, s_odd)
    m = consume(s_even, mx_ctx, vc_ref, jnp.full((1, width), NEG_INF, F32))
    mx_prev = produce(kp_ref, (dist >= TOK - WINDOW) & (i > 0), s_even)
    m = consume(s_odd, mx_cur, vq_ref, m)
    mx_next = produce(kn_ref, (dist <= WINDOW - TOK) & (i < n - 1), s_odd)
    m = consume(s_even, mx_prev, vp_ref, m)
    m = consume(s_odd, mx_next, vn_ref, m)

    sk = jnp.concatenate([jnp.full((1, tq), sink_ref[hh] * LOG2E, F32) for hh in range(nh)], axis=1)
    m2 = jnp.maximum(m, sk)
    a = jnp.exp2(m - m2)
    acc = acc_ref[...]
    out = acc[:HEAD_DIM] * a / (acc[HEAD_DIM:HEAD_DIM + 1] * a + jnp.exp2(sk - m2))
    for hh in range(nh):
        o_ref[0, hh] = out[:, tq * hh: tq * hh + tq].astype(o_ref.dtype)


def _window_attn(q, k_ctx, v_ctx, k_lat, v_lat, sink):
    b, nh, _, s = q.shape
    n = s // TOK
    sc = k_ctx.shape[1]
    prev = lambda i: jnp.maximum(i - 1, 0)
    nxt = lambda i: jnp.minimum(i + 1, n - 1)
    kspec = lambda f: pl.BlockSpec((1, TOK, LANES), lambda bi, i: (bi, f(i), 0))
    vspec = lambda f: pl.BlockSpec((1, 1, 2 * VROWS, TOK), lambda bi, i: (bi, f(i), 0, 0))
    same = lambda i: i
    return pl.pallas_call(
        _window_body,
        grid=(b, n),
        in_specs=[pl.BlockSpec(memory_space=pltpu.SMEM),
                  pl.BlockSpec((1, nh, QPAD, TOK), lambda bi, i: (bi, 0, 0, i)),
                  pl.BlockSpec((1, sc, LANES), lambda bi, i: (bi, 0, 0)),
                  pl.BlockSpec((1, 1, 2 * VROWS, sc), lambda bi, i: (bi, 0, 0, 0)),
                  kspec(prev), kspec(same), kspec(nxt), vspec(prev), vspec(same), vspec(nxt)],
        out_specs=pl.BlockSpec((1, nh, HEAD_DIM, TOK), lambda bi, i: (bi, 0, 0, i)),
        out_shape=jax.ShapeDtypeStruct((b, nh, HEAD_DIM, s), BF16),
        scratch_shapes=[pltpu.VMEM((TOK, nh * TOK), F32), pltpu.VMEM((TOK, nh * TOK), F32),
                        pltpu.VMEM((VROWS, nh * TOK), F32)],
        compiler_params=_params("arbitrary", "arbitrary"),
        name="window_attn",
    )(sink, q, k_ctx, v_ctx, k_lat, k_lat, k_lat, v_lat, v_lat, v_lat)


def _outproj_body(oa_ref, ob_ref, oc_ref, od_ref, lam_ref, subln_ref, w_ref, x_ref, gt_ref,
                  g_ref, sh_ref, sc_ref, wr_ref, xo_ref, h_ref, lg_ref, *, lam_init):
    lp = lam_ref[...]
    lam = (jnp.exp(jnp.sum(lp[0:1] * lp[1:2], axis=1, keepdims=True))
           - jnp.exp(jnp.sum(lp[2:3] * lp[3:4], axis=1, keepdims=True)) + lam_init)
    parts = [oa_ref[0, hh] for hh in range(4)] + [ob_ref[0, hh] for hh in range(4)]
    for hh in range(C_HEADS):
        o = oc_ref[0, 2 * hh] - lam * oc_ref[0, 2 * hh + 1]
        parts.append((_rms_t(o, subln_ref[...]) * (1.0 - lam_init)).astype(BF16))
    parts += [od_ref[0, hh] for hh in range(4)]
    ot = jnp.concatenate(parts, axis=0)
    out = lax.dot_general(ot, w_ref[...], TN_DIMS, preferred_element_type=F32)
    xn = x_ref[0] + gt_ref[0] * out
    xo_ref[0] = xn
    h = _modulate(xn, g_ref[...], sh_ref[0], sc_ref[0])
    h_hi = h.astype(BF16)
    h_ref[0] = h_hi
    h_lo = (h - h_hi.astype(F32)).astype(BF16)
    both = lax.dot_general(wr_ref[...], h_hi, NT_DIMS, preferred_element_type=F32)
    low = lax.dot_general(wr_ref[0:N_EXPERTS], h_lo, NT_DIMS, preferred_element_type=F32)
    lg_ref[0] = both[:N_EXPERTS] + both[N_EXPERTS:] + low


def _outproj(oa, ob, oc, od, lam_p, subln, w_out, x, gate, g_ffn, shift, scale, w_router_t, lam_init):
    b, s, d = x.shape
    tm = min(2 * TOK, s)
    ospec = lambda nh: pl.BlockSpec((1, nh, HEAD_DIM, tm), lambda bi, i: (bi, 0, 0, i))
    full = lambda a: pl.BlockSpec(a.shape, lambda bi, i: (0,) * a.ndim)
    row = pl.BlockSpec((1, 1, d), lambda bi, i: (bi, 0, 0))
    return pl.pallas_call(
        functools.partial(_outproj_body, lam_init=lam_init),
        grid=(b, s // tm),
        in_specs=[ospec(4), ospec(4), ospec(8), ospec(4), full(lam_p), full(subln), full(w_out),
                  pl.BlockSpec((1, tm, d), lambda bi, i: (bi, i, 0)), row, full(g_ffn), row, row,
                  full(w_router_t)],
        out_specs=[pl.BlockSpec((1, tm, d), lambda bi, i: (bi, i, 0)),
                   pl.BlockSpec((1, tm, d), lambda bi, i: (bi, i, 0)),
                   pl.BlockSpec((1, N_EXPERTS, tm), lambda bi, i: (bi, 0, i))],
        out_shape=[jax.ShapeDtypeStruct((b, s, d), F32), jax.ShapeDtypeStruct((b, s, d), BF16),
                   jax.ShapeDtypeStruct((b, N_EXPERTS, s), F32)],
        compiler_params=_params("arbitrary", "arbitrary"),
        name="outproj",
    )(oa, ob, oc, od, lam_p, subln, w_out, x, gate, g_ffn, shift, scale, w_router_t)


def _router_body(lg_ref, slot_ref, aff_ref, st_ref, *, cap, nblk):
    lg = lg_ref[0]
    ex = jnp.exp(lg - jnp.max(lg, axis=0, keepdims=True))
    aff = ex / jnp.sum(ex, axis=0, keepdims=True)
    aff_ref[0] = aff

    def search(_, bounds):
        lo, hi = bounds
        mid = (lo + hi) * 0.5
        enough = jnp.sum((aff >= mid).astype(I32), axis=1, keepdims=True) >= cap
        return jnp.where(enough, mid, lo), jnp.where(enough, hi, mid)

    lo, hi = lax.fori_loop(0, BISECT_STEPS, search,
                           (jnp.zeros((N_EXPERTS, 1), F32), jnp.full((N_EXPERTS, 1), 2.0, F32)))
    gt = aff >= hi
    eq = (aff >= lo) & jnp.logical_not(gt)
    need = (cap - jnp.sum(gt.astype(I32), axis=1, keepdims=True)).astype(F32)
    tri = (lax.broadcasted_iota(I32, (TOK, TOK), 0) <= lax.broadcasted_iota(I32, (TOK, TOK), 1)).astype(BF16)
    lane = lax.broadcasted_iota(I32, (N_EXPERTS, LANES), 1)
    eq_seen = jnp.zeros((N_EXPERTS, 1), F32)
    base = jnp.zeros((N_EXPERTS, 1), I32)
    starts = jnp.zeros((N_EXPERTS, LANES), I32)
    for j in range(nblk):
        cols = slice(TOK * j, TOK * j + TOK)
        eq_c = eq[:, cols]
        eq_cum = jnp.dot(eq_c.astype(BF16), tri, preferred_element_type=F32) + eq_seen
        eq_seen = eq_seen + jnp.sum(eq_c.astype(F32), axis=1, keepdims=True)
        sel = gt[:, cols] | (eq_c & (eq_cum <= need))
        cum = jnp.dot(sel.astype(BF16), tri, preferred_element_type=F32).astype(I32)
        slot_ref[0, :, cols] = jnp.where(sel, base + cum - 1, -1)
        starts = jnp.where(lane == j, base, starts)
        cnt = jnp.sum(sel.astype(I32), axis=1, keepdims=True)
        base = base + jnp.bitwise_and(cnt + (SUBLANES - 1), -SUBLANES)
    st_ref[0] = jnp.where(lane == nblk, base, starts)


def _router(logits_t, cap):
    b, e, s = logits_t.shape
    nblk = s // TOK
    return pl.pallas_call(
        functools.partial(_router_body, cap=cap, nblk=nblk),
        grid=(b,),
        in_specs=[pl.BlockSpec((1, e, s), lambda bi: (bi, 0, 0))],
        out_specs=[pl.BlockSpec((1, e, s), lambda bi: (bi, 0, 0)),
                   pl.BlockSpec((1, e, s), lambda bi: (bi, 0, 0)),
                   pl.BlockSpec((1, e, LANES), lambda bi: (bi, 0, 0))],
        out_shape=[jax.ShapeDtypeStruct((b, e, s), I32), jax.ShapeDtypeStruct((b, e, s), F32),
                   jax.ShapeDtypeStruct((b, e, LANES), I32)],
        compiler_params=_params("arbitrary"),
        name="router",
    )(logits_t)


def _onehot_t(slot_ref, first, win):
    rows = lax.broadcasted_iota(I32, (win, TOK), 0)
    return [rows == (slot_ref[0, e:e + 1, :] - first[e]) for e in range(N_EXPERTS)]


def _dispatch_body(st_ref, h_ref, slot_ref, aff_ref, xs_ref, stage, sems, count, *, win):
    bi, j = pl.program_id(0), pl.program_id(1)
    d = h_ref.shape[-1]

    @pl.when((bi == 0) & (j == 0))
    def _():
        count[0] = 0

    rows_alloc = xs_ref.shape[2]
    last_block = j == pl.num_programs(1) - 1
    st = [st_ref[bi, e, j] for e in range(N_EXPERTS)]
    ends = [jnp.where(last_block, rows_alloc, st_ref[bi, e, j + 1]) for e in range(N_EXPERTS)]
    width = ends[0] - st[0]
    for e in range(1, N_EXPERTS):
        width = jnp.maximum(width, ends[e] - st[e])
    rounds = jnp.maximum((width + win - 1) // win, 1)

    def copies(buf, first):
        return [pltpu.make_async_copy(stage.at[buf, pl.ds(e * win, win), :],
                                      xs_ref.at[bi, e, pl.ds(pl.multiple_of(first[e], SUBLANES), win), :],
                                      sems.at[e]) for e in range(N_EXPERTS)]

    def one_round(r, carry):
        k = count[0]
        buf = k % 2
        first = [jnp.minimum(st[e] + r * win, rows_alloc - win) for e in range(N_EXPERTS)]
        hot = _onehot_t(slot_ref, first, win)
        p = jnp.concatenate(hot, axis=0).astype(BF16)
        stage[buf, :, 0:d] = jnp.dot(p, h_ref[0], preferred_element_type=F32)
        gates = [jnp.sum(jnp.where(hot[e], aff_ref[0, e:e + 1, :], 0.0), axis=1, keepdims=True)
                 for e in range(N_EXPERTS)]
        stage[buf, :, d:d + LANES] = jnp.broadcast_to(jnp.concatenate(gates, axis=0), (N_EXPERTS * win, LANES))

        @pl.when(k == 0)
        def _():
            for c in copies(buf, first):
                c.start()

        @pl.when(k > 0)
        def _():
            for done, c in zip(copies(1 - buf, st), copies(buf, first)):
                done.wait()
                c.start()

        count[0] = k + 1
        return carry

    lax.fori_loop(0, rounds, one_round, 0)

    @pl.when((bi == pl.num_programs(0) - 1) & (j == pl.num_programs(1) - 1))
    def _():
        for c in copies(0, st):
            c.wait()


def _dispatch(starts, h, slot, aff, rows_alloc, win):
    b, s, d = h.shape
    nblk = s // TOK
    e = N_EXPERTS
    return pl.pallas_call(
        functools.partial(_dispatch_body, win=win),
        grid_spec=pltpu.PrefetchScalarGridSpec(
            num_scalar_prefetch=1, grid=(b, nblk),
            in_specs=[pl.BlockSpec((1, TOK, d), lambda bi, j, st: (bi, j, 0)),
                      pl.BlockSpec((1, e, TOK), lambda bi, j, st: (bi, 0, j)),
                      pl.BlockSpec((1, e, TOK), lambda bi, j, st: (bi, 0, j))],
            out_specs=pl.BlockSpec(memory_space=pl.ANY),
            scratch_shapes=[pltpu.VMEM((2, e * win, d + LANES), F32),
                            pltpu.SemaphoreType.DMA((e,)),
                            pltpu.SMEM((1,), I32)]),
        out_shape=jax.ShapeDtypeStruct((b, e, rows_alloc, d + LANES), F32),
        compiler_params=_params("arbitrary", "arbitrary"),
        name="dispatch",
    )(starts, h, slot, aff)


def _ffn_body(*refs, layer, sets, steps):
    n = len(sets)
    st_refs, xs_refs = refs[:n], refs[n:2 * n]
    wg_in, wu_in, wd_in = refs[2 * n: 2 * n + 3]
    y_refs = refs[2 * n + 3: 3 * n + 3]
    bufs, sems = refs[3 * n + 3: 3 * n + 6], refs[3 * n + 6]
    wg_ref, wu_ref, wd_ref = refs[3 * n + 7: 3 * n + 10]
    e, bi, i = pl.program_id(0), pl.program_id(1), pl.program_id(2)
    d = y_refs[0].shape[-1]

    def fetch(expert, slot, piece):
        k, p = divmod(piece, WEIGHT_PIECES)
        w, buf = (wg_in, wu_in, wd_in)[k], bufs[k]
        rows = w.shape[2] // WEIGHT_PIECES
        return pltpu.make_async_copy(w.at[layer, expert, pl.ds(p * rows, rows), :],
                                     buf.at[slot, pl.ds(p * rows, rows), :], sems.at[slot, piece])

    pieces = range(3 * WEIGHT_PIECES)
    step = bi * pl.num_programs(2) + i
    slot = e % 2

    @pl.when(step == 0)
    def _():
        @pl.when(e == 0)
        def _():
            for piece in pieces:
                fetch(0, 0, piece).start()

        for piece in pieces:
            fetch(e, slot, piece).wait()
        for out, buf in zip((wg_ref, wu_ref, wd_ref), bufs):
            out[...] = buf[slot].astype(BF16)

    for piece in pieces:
        @pl.when((step == min(piece, steps - 1)) & (e + 1 < pl.num_programs(0)))
        def _(piece=piece):
            fetch(e + 1, 1 - slot, piece).start()

    for (first, count, nblk), st_ref, xs_ref, y_ref in zip(sets, st_refs, xs_refs, y_refs):
        tm = xs_ref.shape[2]
        used = st_ref[bi, e, nblk]
        base = (i - first) * tm
        mine = (i >= first) & (i < first + count)

        def run(rows, xs_ref=xs_ref, y_ref=y_ref, tm=tm, used=used, base=base):
            valid = (base + lax.broadcasted_iota(I32, (rows, 1), 0)) < used
            xa = xs_ref[0, 0, 0:rows, :]
            x = jnp.where(valid, xa[:, 0:d], 0.0).astype(BF16)
            gate = jnp.where(valid, xa[:, d:d + 1], 0.0)
            a = jnp.dot(x, wg_ref[...], preferred_element_type=F32)
            u = jnp.dot(x, wu_ref[...], preferred_element_type=F32)
            mid = (a * jax.nn.sigmoid(a) * u).astype(BF16)
            y_ref[0, 0, 0:rows, :] = (jnp.dot(mid, wd_ref[...], preferred_element_type=F32) * gate).astype(BF16)
            if rows < tm:
                y_ref[0, 0, rows:tm, :] = jnp.zeros((tm - rows, d), BF16)

        half = tm // 2 if tm % (4 * SUBLANES) == 0 else 0
        left = used - base
        pl.when(mine & (left > half))(functools.partial(run, tm))
        if half:
            pl.when(mine & (left > 0) & (left <= half))(functools.partial(run, half))

        @pl.when(mine & (left <= 0))
        def _(y_ref=y_ref):
            y_ref[0, 0] = jnp.zeros(y_ref.shape[2:], BF16)


def _ffn(routed, wg, wu, wd, layer):
    b, e = routed[0]["xs"].shape[:2]
    d, f = wg.shape[-2:]
    sets, first = [], 0
    for r in routed:
        count = r["rows"] // r["tm"]
        sets.append((first, count, r["nblk"]))
        first += count

    def tile_spec(r, first_tile, count, width):
        def index(ei, bi, i, *st):
            return bi, ei, jnp.clip(i - first_tile, 0, count - 1), 0
        return pl.BlockSpec((1, 1, r["tm"], width), index)

    in_specs = [tile_spec(r, s[0], s[1], d + LANES) for r, s in zip(routed, sets)]
    out_specs = [tile_spec(r, s[0], s[1], d) for r, s in zip(routed, sets)]
    out_shape = [jax.ShapeDtypeStruct((b, e, r["rows"], d), BF16) for r in routed]
    scratch = ([pltpu.VMEM((2,) + w.shape[2:], F32) for w in (wg, wu, wd)]
               + [pltpu.SemaphoreType.DMA((2, 3 * WEIGHT_PIECES))]
               + [pltpu.VMEM(w.shape[2:], BF16) for w in (wg, wu, wd)])
    return pl.pallas_call(
        functools.partial(_ffn_body, layer=layer, sets=tuple(sets), steps=b * first),
        grid_spec=pltpu.PrefetchScalarGridSpec(
            num_scalar_prefetch=len(routed), grid=(e, b, first),
            in_specs=in_specs + [pl.BlockSpec(memory_space=pl.ANY)] * 3,
            out_specs=out_specs, scratch_shapes=scratch),
        out_shape=out_shape,
        compiler_params=_params("arbitrary", "arbitrary", "arbitrary"),
        name="expert_ffn",
    )(*[r["starts"] for r in routed], *[r["xs"] for r in routed], wg, wu, wd)


def _combine_body(st_ref, slot_ref, x_ref, gt_ref, gf_ref, y_ref, o_ref, ybuf, sems, *, win, rows, final_norm):
    bi, j = pl.program_id(0), pl.program_id(1)
    nj = pl.num_programs(1)
    d = x_ref.shape[-1]
    step = bi * nj + j
    cur = step % 2
    st = [st_ref[bi, e, j] for e in range(N_EXPERTS)]
    width = st_ref[bi, 0, j + 1] - st[0]
    for e in range(1, N_EXPERTS):
        width = jnp.maximum(width, st_ref[bi, e, j + 1] - st[e])
    rounds = (width + win - 1) // win

    wrows = win + BF16_ROWS

    def window(b_, j_, r):
        lo = [st_ref[b_, e, j_] + r * win for e in range(N_EXPERTS)]
        first = [jnp.minimum(jnp.bitwise_and(lo[e], -BF16_ROWS), rows - wrows) for e in range(N_EXPERTS)]
        return lo, first

    def copies(b_, first, buf):
        return [pltpu.make_async_copy(y_ref.at[b_, e, pl.ds(pl.multiple_of(first[e], BF16_ROWS), wrows), :],
                                      ybuf.at[buf, pl.ds(e * wrows, wrows), :], sems.at[buf, e])
                for e in range(N_EXPERTS)]

    def scatter(lo, first, buf, acc):
        hot = _onehot_t(slot_ref, first, wrows)
        hot = [hot[e] & (slot_ref[0, e:e + 1, :] >= lo[e]) & (slot_ref[0, e:e + 1, :] < lo[e] + win)
               for e in range(N_EXPERTS)]
        p = jnp.concatenate(hot, axis=0).astype(BF16)
        return acc + lax.dot_general(p, ybuf[buf], TN_DIMS, preferred_element_type=F32)

    lo0, first0 = window(bi, j, 0)

    @pl.when(step == 0)
    def _():
        for c in copies(bi, first0, cur):
            c.start()

    @pl.when(step < pl.num_programs(0) * nj - 1)
    def _():
        wrap = j == nj - 1
        b_next, j_next = jnp.where(wrap, bi + 1, bi), jnp.where(wrap, 0, j + 1)
        for c in copies(b_next, window(b_next, j_next, 0)[1], 1 - cur):
            c.start()

    for c in copies(bi, first0, cur):
        c.wait()
    acc = scatter(lo0, first0, cur, jnp.zeros((TOK, d), F32))

    def later_round(r, acc):
        lo, first = window(bi, j, r)
        for c in copies(bi, first, 2):
            c.start()
        for c in copies(bi, first, 2):
            c.wait()
        return scatter(lo, first, 2, acc)

    acc = lax.fori_loop(1, rounds, later_round, acc)
    out = x_ref[0] + gt_ref[0] * acc
    if final_norm:
        ms = jnp.mean(out * out, axis=-1, keepdims=True)
        out = out * lax.rsqrt(ms + EPS) * gf_ref[...]
    o_ref[0] = out


def _combine(starts, slot, x, gate, g_final, y, win, final_norm):
    b, s, d = x.shape
    e = N_EXPERTS
    rows = y.shape[2]
    return pl.pallas_call(
        functools.partial(_combine_body, win=win, rows=rows, final_norm=final_norm),
        grid_spec=pltpu.PrefetchScalarGridSpec(
            num_scalar_prefetch=1, grid=(b, s // TOK),
            in_specs=[pl.BlockSpec((1, e, TOK), lambda bi, j, st: (bi, 0, j)),
                      pl.BlockSpec((1, TOK, d), lambda bi, j, st: (bi, j, 0)),
                      pl.BlockSpec((1, 1, d), lambda bi, j, st: (bi, 0, 0)),
                      pl.BlockSpec((1, d), lambda bi, j, st: (0, 0)),
                      pl.BlockSpec(memory_space=pl.ANY)],
            out_specs=pl.BlockSpec((1, TOK, d), lambda bi, j, st: (bi, j, 0)),
            scratch_shapes=[pltpu.VMEM((3, e * (win + BF16_ROWS), d), BF16), pltpu.SemaphoreType.DMA((3, e))]),
        out_shape=jax.ShapeDtypeStruct((b, s, d), F32),
        compiler_params=_params("arbitrary", "arbitrary"),
        name="combine",
    )(starts, slot, x, gate, g_final, y)


def _route(h, logits_t):
    n = h.shape[1]
    nblk = n // TOK
    cap = CAPACITY_FACTOR * n // N_EXPERTS
    win = min(SLOT_WINDOW, cap)
    rows = cap + SUBLANES * nblk
    rows = -(-rows // BF16_ROWS) * BF16_ROWS
    tm = min(TOK, rows)
    rows = -(-rows // tm) * tm
    slot, aff, starts = _router(logits_t, cap)
    xs = _dispatch(starts, h, slot, aff, rows + win, win)
    return dict(slot=slot, starts=starts, xs=xs, win=win, rows=rows, tm=tm, nblk=nblk)


def _rope_tables(n_tokens, rot_dim):
    rows = n_tokens // GRID_W
    row = jnp.repeat(jnp.arange(rows), GRID_W)
    col = jnp.tile(jnp.arange(GRID_W), rows)
    quarter = rot_dim // 4
    inv_freq = ROPE_THETA ** (-jnp.arange(quarter, dtype=F32) / quarter)
    ang = jnp.concatenate([inv_freq[:, None] * row[None, :], inv_freq[:, None] * col[None, :]], axis=0)
    return jnp.cos(ang), jnp.sin(ang)


def _identity_tables(n_tokens, rot_dim):
    return jnp.ones((rot_dim // 2, n_tokens), F32), jnp.zeros((rot_dim // 2, n_tokens), F32)


def kernel(x, c, ctx, c_ctx, w_mod, b_mod, g_attn, g_ffn, w_in, a_sink, b_q_norm, b_w_uq, b_kv_norm, b_w_ukv, c_lambda, c_subln, d_q_norm, d_k_norm, w_out, w_router, w_gate, w_up, w_down, g_final):
    b, s, d = x.shape
    depth = w_mod.shape[0]
    n_ctx = ctx.shape[1]
    rope64, rope32 = _rope_tables(s, HEAD_DIM), _rope_tables(s, C_QK)
    id64, id32 = _identity_tables(n_ctx, HEAD_DIM), _identity_tables(n_ctx, C_QK)

    cond = jnp.zeros((SUBLANES, d), F32).at[:b].set(c).at[b].set(c_ctx)
    mod = _adaln(cond, w_mod, b_mod)

    col = lambda v: v.reshape(-1, 1)
    xl, xc = x, ctx
    for l in range(depth):
        last = l == depth - 1
        lam_init = 0.8 - 0.6 * math.exp(-0.3 * l)
        m6 = mod[l].reshape(SUBLANES, 6, d)
        lat = [m6[:b, k][:, None, :] for k in range(6)]
        cx = [jnp.broadcast_to(m6[b, k][None, None, :], (b, 1, d)) for k in range(6)]
        w_in_t = w_in[l].T.astype(BF16)
        wuq_t = b_w_uq[l].T.astype(BF16)
        wukv_t = b_w_ukv[l].T.astype(BF16)
        w_out_b = w_out[l].astype(BF16)
        w_router_t = w_router[l].T
        w_router_hi = w_router_t.astype(BF16)
        w_router_t = jnp.concatenate([w_router_hi, (w_router_t - w_router_hi.astype(F32)).astype(BF16)], axis=0)
        g_a, g_f = g_attn[l][None, :], g_ffn[l][None, :]
        small = (col(b_q_norm[l]), wuq_t, col(b_kv_norm[l]), wukv_t, col(d_q_norm[l]), col(d_k_norm[l]))

        pl_ = _proj(xl, g_a, lat[0], lat[1], w_in_t, rope64, rope32, *small)
        pc_ = _proj(xc, g_a, cx[0], cx[1], w_in_t, id64, id32, *small)
        qa, ka, va, qb, kb, vb, qc, kc, vc, qd, kd, vd = pl_
        qa_c, ka_c, va_c, qb_c, kb_c, vb_c, qc_c, kc_c, vc_c, qd_c, kd_c, vd_c = pc_

        zero = lambda h: 0
        oa = _window_attn(qa, ka_c, va_c, ka, va, a_sink[l])
        ob = _dense_attn(qb, kb_c, vb_c, kb, vb, lambda h: h, lambda h: h)
        oc = _dense_attn(qc, kc_c, vc_c, kc, vc, lambda h: h // 4, lambda h: h // 2, out_dtype=F32)
        od = _dense_attn(qd, kd_c, vd_c, kd, vd, zero, lambda h: h // 2)
        outproj = functools.partial(_outproj, lam_p=c_lambda[l], subln=col(c_subln[l]), w_out=w_out_b,
                                    g_ffn=g_f, w_router_t=w_router_t, lam_init=lam_init)
        xl, hl, lg = outproj(oa, ob, oc, od, x=xl, gate=lat[2], shift=lat[3], scale=lat[4])
        routed = [_route(hl, lg)]
        if not last:
            oa = _dense_attn(qa_c, ka_c, va_c, None, None, zero, lambda h: h // 2, sink=a_sink[l])
            ob = _dense_attn(qb_c, kb_c, vb_c, None, None, lambda h: h, lambda h: h)
            oc = _dense_attn(qc_c, kc_c, vc_c, None, None, lambda h: h // 4, lambda h: h // 2, out_dtype=F32)
            od = _dense_attn(qd_c, kd_c, vd_c, None, None, zero, lambda h: h // 2)
            xc, hc, lgc = outproj(oa, ob, oc, od, x=xc, gate=cx[2], shift=cx[3], scale=cx[4])
            routed.append(_route(hc, lgc))
        ys = _ffn(routed, w_gate, w_up, w_down, l)
        combine = lambda r, y, x_, gate, norm: _combine(r["starts"], r["slot"], x_, gate, g_final[None, :], y,
                                                        r["win"], norm)
        xl = combine(routed[0], ys[0], xl, lat[5], last)
        if not last:
            xc = combine(routed[1], ys[1], xc, cx[5], False)
    return xl
```

```python
import functools
import math

import jax
import jax.numpy as jnp
from jax import lax
from jax.experimental import pallas as pl
from jax.experimental.pallas import tpu as pltpu

F32, BF16, I32 = jnp.float32, jnp.bfloat16, jnp.int32
HIGHEST = lax.Precision.HIGHEST

SUBLANES = 8
LANES = 128
BF16_ROWS = 16

GRID_W = 64
HEAD_DIM = 64
WINDOW = 128
EPS = 1e-6
ROPE_THETA = 10000.0
NEG_INF = -1e30
A_HEADS, A_KV = 4, 2
B_HEADS, B_Q_RANK, B_KV_RANK, B_NOPE, B_ROPE, B_V = 4, 192, 128, 64, 32, 64
C_HEADS, C_QK, C_V = 4, 32, 64
D_HEADS, D_KV = 4, 2
N_EXPERTS = 16
CAPACITY_FACTOR = 2

A_COLS = A_HEADS * HEAD_DIM + 2 * A_KV * HEAD_DIM
B_COLS = B_Q_RANK + B_KV_RANK + B_ROPE
C_COLS = 4 * C_HEADS * C_QK + C_HEADS * C_V
D_COLS = D_HEADS * HEAD_DIM + 2 * D_KV * HEAD_DIM
A_OFF, B_OFF, C_OFF, D_OFF = 0, A_COLS, A_COLS + B_COLS, A_COLS + B_COLS + C_COLS

BISECT_STEPS = 152
TOK = 256
QPAD = 128
VROWS = HEAD_DIM + 16
TQ = 1024
ATTN_UNROLL = 8
KEY_CHUNK = 256
WINDOW_CHUNKS = (0, -1, 1)
FFN_TILE = 640
FFN_STEP = 128
WEIGHT_PIECES = 2
SLOT_WINDOW = 48
LOG2E = math.log2(math.e)

NT_DIMS = (((1,), (1,)), ((), ()))
TN_DIMS = (((0,), (0,)), ((), ()))


def _params(*sem):
    return pltpu.CompilerParams(dimension_semantics=sem, vmem_limit_bytes=56 * 1024 * 1024)


def _adaln_body(c_ref, w_ref, b_ref, o_ref):
    c = c_ref[...]
    s = c * jax.nn.sigmoid(c)
    o_ref[0] = jnp.dot(s, w_ref[0], precision=HIGHEST, preferred_element_type=F32) + b_ref[0]


def _adaln(cond, w_mod, b_mod):
    depth, d, n = w_mod.shape
    tn = n // 4
    return pl.pallas_call(
        _adaln_body,
        grid=(depth, n // tn),
        in_specs=[pl.BlockSpec((SUBLANES, d), lambda l, j: (0, 0)),
                  pl.BlockSpec((1, d, tn), lambda l, j: (l, 0, j)),
                  pl.BlockSpec((1, 1, tn), lambda l, j: (l, 0, j))],
        out_specs=pl.BlockSpec((1, SUBLANES, tn), lambda l, j: (l, 0, j)),
        out_shape=jax.ShapeDtypeStruct((depth, SUBLANES, n), F32),
        compiler_params=_params("arbitrary", "arbitrary"),
        name="adaln",
    )(cond, w_mod, b_mod.reshape(depth, 1, n))


def _rope_t(xt, cos, sin):
    half = xt.shape[0] // 2
    x1, x2 = xt[:half], xt[half:]
    return jnp.concatenate([x1 * cos - x2 * sin, x1 * sin + x2 * cos], axis=0)


def _rms_t(xt, g):
    ms = jnp.mean(xt * xt, axis=0, keepdims=True)
    return xt * lax.rsqrt(ms + EPS) * g


def _modulate(x, g, shift, scale):
    ms = jnp.mean(x * x, axis=-1, keepdims=True)
    return x * lax.rsqrt(ms + EPS) * g * (1.0 + scale) + shift


def _proj_body(x_ref, g_ref, sh_ref, sc_ref, w_ref, c64_ref, s64_ref, c32_ref, s32_ref,
               bqn_ref, wuq_ref, bkvn_ref, wukv_ref, dqn_ref, dkn_ref,
               qa_ref, ka_ref, va_ref, qb_ref, kb_ref, vb_ref,
               qc_ref, kc_ref, vc_ref, qd_ref, kd_ref, vd_ref):
    tm = x_ref.shape[1]
    h = _modulate(x_ref[0], g_ref[...], sh_ref[0], sc_ref[0]).astype(BF16)
    pt = lax.dot_general(w_ref[...], h, NT_DIMS, preferred_element_type=F32)
    c64, s64, c32, s32 = c64_ref[...], s64_ref[...], c32_ref[...], s32_ref[...]
    z64 = jnp.zeros((64, tm), F32)
    z32 = jnp.zeros((32, tm), F32)
    scale64 = HEAD_DIM ** -0.5 * LOG2E
    scale_b = (B_NOPE + B_ROPE) ** -0.5 * LOG2E
    scale_c = C_QK ** -0.5 * LOG2E
    def put_values(v_ref, hh, vt):
        chunk = v_ref.shape[-1]
        for c in range(tm // chunk):
            v_ref[0, c, VROWS * hh: VROWS * hh + HEAD_DIM] = vt[:, chunk * c: chunk * c + chunk].astype(BF16)
            v_ref[0, c, VROWS * hh + HEAD_DIM: VROWS * hh + VROWS] = jnp.ones((VROWS - HEAD_DIM, chunk), BF16)

    def gqa(off, q_ref, k_ref, v_ref, qn, kn):
        for hh in range(4):
            q = pt[off + 64 * hh: off + 64 * hh + 64]
            if qn is not None:
                q = _rms_t(q, qn)
            q = _rope_t(q, c64, s64) * scale64
            q_ref[0, hh] = (jnp.concatenate([q, z64], axis=0) if hh // 2 == 0
                            else jnp.concatenate([z64, q], axis=0)).astype(BF16)
        ks = []
        for j in range(2):
            k = pt[off + 256 + 64 * j: off + 256 + 64 * j + 64]
            if kn is not None:
                k = _rms_t(k, kn)
            ks.append(_rope_t(k, c64, s64))
        k_ref[0] = jnp.concatenate(ks, axis=0).T.astype(BF16)
        for j in range(2):
            put_values(v_ref, j, pt[off + 384 + 64 * j: off + 384 + 64 * j + 64])

    gqa(A_OFF, qa_ref, ka_ref, va_ref, None, None)
    gqa(D_OFF, qd_ref, kd_ref, vd_ref, dqn_ref[...], dkn_ref[...])

    cq = _rms_t(pt[B_OFF: B_OFF + B_Q_RANK], bqn_ref[...]).astype(BF16)
    ckv = _rms_t(pt[B_OFF + B_Q_RANK: B_OFF + B_Q_RANK + B_KV_RANK], bkvn_ref[...]).astype(BF16)
    kr = _rope_t(pt[B_OFF + B_Q_RANK + B_KV_RANK: B_OFF + B_COLS], c32, s32)
    qt = jnp.dot(wuq_ref[...], cq, preferred_element_type=F32)
    kvt = jnp.dot(wukv_ref[...], ckv, preferred_element_type=F32)
    kparts = []
    dq = B_NOPE + B_ROPE
    for hh in range(B_HEADS):
        qn_ = qt[dq * hh: dq * hh + B_NOPE]
        qr = _rope_t(qt[dq * hh + B_NOPE: dq * hh + dq], c32, s32)
        qb_ref[0, hh] = (jnp.concatenate([qn_, qr, z32], axis=0) * scale_b).astype(BF16)
        kparts.append(jnp.concatenate([kvt[128 * hh: 128 * hh + B_NOPE], kr, z32], axis=0))
        put_values(vb_ref, hh, kvt[128 * hh + B_NOPE: 128 * hh + 128])
    kb_ref[0] = jnp.concatenate(kparts, axis=0).T.astype(BF16)

    for j in range(2 * C_HEADS):
        q = _rope_t(pt[C_OFF + 32 * j: C_OFF + 32 * j + 32], c32, s32) * scale_c
        pieces = [z32, z32, z32, z32]
        pieces[j % 4] = q
        qc_ref[0, j] = jnp.concatenate(pieces, axis=0).astype(BF16)
    kc = [_rope_t(pt[C_OFF + 256 + 32 * j: C_OFF + 256 + 32 * j + 32], c32, s32) for j in range(2 * C_HEADS)]
    kc_ref[0] = jnp.concatenate(kc, axis=0).T.astype(BF16)
    for hh in range(C_HEADS):
        put_values(vc_ref, hh, pt[C_OFF + 512 + 64 * hh: C_OFF + 512 + 64 * hh + 64])


def _proj(x, g, shift, scale, w_in_t, rope64, rope32, bqn, wuq_t, bkvn, wukv_t, dqn, dkn):
    b, s, d = x.shape
    tm = min(2 * TOK, s)
    n = s // tm
    full = lambda a: pl.BlockSpec(a.shape, lambda bi, i: (0,) * a.ndim)
    q_spec = lambda nh: pl.BlockSpec((1, nh, QPAD, tm), lambda bi, i: (bi, 0, 0, i))
    k_spec = lambda w: pl.BlockSpec((1, tm, w), lambda bi, i: (bi, i, 0))
    kch = min(KEY_CHUNK, tm)
    v_spec = lambda r, c=kch: pl.BlockSpec((1, tm // c, r, c), lambda bi, i: (bi, i, 0, 0))
    q_shape = lambda nh: jax.ShapeDtypeStruct((b, nh, QPAD, s), BF16)
    k_shape = lambda w: jax.ShapeDtypeStruct((b, s, w), BF16)
    v_shape = lambda r, c=kch: jax.ShapeDtypeStruct((b, s // c, r, c), BF16)
    tab = lambda t: pl.BlockSpec((t.shape[0], tm), lambda bi, i: (0, i))
    c64, s64 = rope64
    c32, s32 = rope32
    return pl.pallas_call(
        _proj_body,
        grid=(b, n),
        in_specs=[pl.BlockSpec((1, tm, d), lambda bi, i: (bi, i, 0)), full(g),
                  pl.BlockSpec((1, 1, d), lambda bi, i: (bi, 0, 0)),
                  pl.BlockSpec((1, 1, d), lambda bi, i: (bi, 0, 0)),
                  full(w_in_t), tab(c64), tab(s64), tab(c32), tab(s32),
                  full(bqn), full(wuq_t), full(bkvn), full(wukv_t), full(dqn), full(dkn)],
        out_specs=[q_spec(4), k_spec(128), v_spec(2 * VROWS, TOK),
                   q_spec(4), k_spec(512), v_spec(4 * VROWS),
                   q_spec(8), k_spec(256), v_spec(4 * VROWS),
                   q_spec(4), k_spec(128), v_spec(2 * VROWS)],
        out_shape=[q_shape(4), k_shape(128), v_shape(2 * VROWS, TOK),
                   q_shape(4), k_shape(512), v_shape(4 * VROWS),
                   q_shape(8), k_shape(256), v_shape(4 * VROWS),
                   q_shape(4), k_shape(128), v_shape(2 * VROWS)],
        compiler_params=_params("arbitrary", "arbitrary"),
        name="proj",
    )(x, g, shift, scale, w_in_t, c64, s64, c32, s32, bqn, wuq_t, bkvn, wukv_t, dqn, dkn)


def _dense_body(*refs, has_lat, has_sink, n_lat):
    refs = list(refs)
    sink_ref = refs.pop(0) if has_sink else None
    q_ref, kc_ref, vc_ref = refs[:3]
    kl_ref, vl_ref = (refs[3], refs[4]) if has_lat else (None, None)
    o_ref, s_even, s_odd, acc_ref = refs[-4:]
    tq = q_ref.shape[-1]
    qt = q_ref[0, 0]

    def produce(kblk, s_ref):
        s = jnp.dot(kblk, qt, preferred_element_type=F32)
        s_ref[0:kblk.shape[0], :] = s
        return jnp.max(s, axis=0, keepdims=True)

    def consume(s_ref, mx, vblk, m):
        m_new = mx if m is None else jnp.maximum(m, mx)
        p = jnp.exp2(s_ref[0:vblk.shape[1], :] - m_new).astype(BF16)
        pv = jnp.dot(vblk, p, preferred_element_type=F32)
        acc_ref[...] = pv if m is None else jnp.exp2(m - m_new) * acc_ref[...] + pv
        return m_new

    kch = s_even.shape[0]

    def lat_keys(i):
        return kl_ref[0, pl.ds(pl.multiple_of(i * kch, kch), kch), :]

    mx_c = produce(kc_ref[0], s_odd)
    if has_lat:
        mx_e = produce(lat_keys(0), s_even)
    m = consume(s_odd, mx_c, vc_ref[0, 0], None)
    if has_lat:
        def body(j, carry):
            m, mx_e = carry
            mx_o = produce(lat_keys(2 * j + 1), s_odd)
            m = consume(s_even, mx_e, vl_ref[0, 2 * j], m)
            mx_e = produce(lat_keys(jnp.minimum(2 * j + 2, n_lat - 1)), s_even)
            m = consume(s_odd, mx_o, vl_ref[0, 2 * j + 1], m)
            return m, mx_e
        m, _ = lax.fori_loop(0, n_lat // 2, body, (m, mx_e), unroll=ATTN_UNROLL)
    acc = acc_ref[...]
    num, den = acc[:HEAD_DIM], acc[HEAD_DIM:HEAD_DIM + 1]
    if has_sink:
        sk = sink_ref[pl.program_id(1)] * LOG2E
        m2 = jnp.maximum(m, sk)
        a = jnp.exp2(m - m2)
        den = den * a + jnp.exp2(sk - m2)
        num = num * a
    o_ref[0, 0] = (num * (1.0 / den)).astype(o_ref.dtype)


def _dense_attn(q, k_ctx, v_ctx, k_lat, v_lat, kgroup, vhead, sink=None, out_dtype=BF16):
    b, nh, _, s = q.shape
    tq = min(TQ, s)
    has_lat = k_lat is not None
    has_sink = sink is not None
    sc = k_ctx.shape[1]
    in_specs = [pl.BlockSpec((1, 1, QPAD, tq), lambda bi, h, i: (bi, h, 0, i)),
                pl.BlockSpec((1, sc, LANES), lambda bi, h, i: (bi, 0, kgroup(h))),
                pl.BlockSpec((1, 1, VROWS, sc), lambda bi, h, i: (bi, 0, vhead(h), 0))]
    args = [q, k_ctx, v_ctx]
    n_lat, kch = 0, sc
    if has_lat:
        sl = k_lat.shape[1]
        kch = v_lat.shape[-1]
        n_lat = sl // kch
        assert n_lat % 2 == 0 and kch >= sc
        in_specs += [pl.BlockSpec((1, sl, LANES), lambda bi, h, i: (bi, 0, kgroup(h))),
                     pl.BlockSpec((1, n_lat, VROWS, kch), lambda bi, h, i: (bi, 0, vhead(h), 0))]
        args += [k_lat, v_lat]
    if has_sink:
        in_specs = [pl.BlockSpec(memory_space=pltpu.SMEM)] + in_specs
        args = [sink] + args
    return pl.pallas_call(
        functools.partial(_dense_body, has_lat=has_lat, has_sink=has_sink, n_lat=n_lat),
        grid=(b, nh, s // tq),
        in_specs=in_specs,
        out_specs=pl.BlockSpec((1, 1, HEAD_DIM, tq), lambda bi, h, i: (bi, h, 0, i)),
        out_shape=jax.ShapeDtypeStruct((b, nh, HEAD_DIM, s), out_dtype),
        scratch_shapes=[pltpu.VMEM((kch, tq), F32), pltpu.VMEM((kch, tq), F32), pltpu.VMEM((VROWS, tq), F32)],
        compiler_params=_params("arbitrary", "arbitrary", "arbitrary"),
        name="dense_attn",
    )(*args)


def _window_body(sink_ref, q_ref, kc_ref, vc_ref, *rest, n_chunks):
    nw = len(WINDOW_CHUNKS)
    k_refs, v_refs = rest[:nw], rest[nw:2 * nw]
    o_ref, s_even, s_odd, acc_ref = rest[2 * nw:]
    i = pl.program_id(1)
    nh, tq = q_ref.shape[1], q_ref.shape[-1]
    width = nh * tq
    qt = jnp.concatenate([q_ref[0, hh] for hh in range(nh)], axis=1)
    dist = lax.broadcasted_iota(I32, (TOK, tq), 0) - lax.broadcasted_iota(I32, (TOK, tq), 1)

    def in_window(rel):
        chunk = (tq // TOK) * i + rel
        return (jnp.abs(dist + (rel * TOK)) <= WINDOW) & (chunk >= 0) & (chunk < n_chunks)

    def produce(k_ref, ok, s_ref):
        s = jnp.dot(k_ref[0], qt, preferred_element_type=F32)
        if ok is not None:
            s = jnp.concatenate([jnp.where(ok, s[:, tq * hh: tq * hh + tq], NEG_INF) for hh in range(nh)], axis=1)
        s_ref[...] = s
        return jnp.max(s, axis=0, keepdims=True)

    def consume(s_ref, mx, v_ref, m):
        m_new = jnp.maximum(m, mx)
        p = jnp.exp2(s_ref[...] - m_new).astype(BF16)
        pv = [jnp.dot(v_ref[0, 0, VROWS * g: VROWS * g + VROWS, :], p[:, 2 * tq * g: 2 * tq * g + 2 * tq],
                      preferred_element_type=F32) for g in range(A_KV)]
        acc_ref[...] = jnp.exp2(m - m_new) * acc_ref[...] + jnp.concatenate(pv, axis=1)
        return m_new

    acc_ref[...] = jnp.zeros(acc_ref.shape, F32)
    bufs = (s_even, s_odd)
    mx = produce(kc_ref, None, s_even)
    m = jnp.full((1, width), NEG_INF, F32)
    prev_v = vc_ref
    for c, rel in enumerate(WINDOW_CHUNKS):
        mx_next = produce(k_refs[c], in_window(rel), bufs[(c + 1) % 2])
        m = consume(bufs[c % 2], mx, prev_v, m)
        mx, prev_v = mx_next, v_refs[c]
    m = consume(bufs[nw % 2], mx, prev_v, m)

    sk = jnp.concatenate([jnp.full((1, tq), sink_ref[hh] * LOG2E, F32) for hh in range(nh)], axis=1)
    m2 = jnp.maximum(m, sk)
    a = jnp.exp2(m - m2)
    acc = acc_ref[...]
    out = acc[:HEAD_DIM] * a / (acc[HEAD_DIM:HEAD_DIM + 1] * a + jnp.exp2(sk - m2))
    for hh in range(nh):
        o_ref[0, hh] = out[:, tq * hh: tq * hh + tq].astype(o_ref.dtype)


def _window_attn(q, k_ctx, v_ctx, k_lat, v_lat, sink):
    b, nh, _, s = q.shape
    tq = TOK
    assert s % tq == 0 and WINDOW <= TOK
    n_chunks = s // TOK
    sc = k_ctx.shape[1]
    nw = len(WINDOW_CHUNKS)
    chunk = lambda rel: (lambda i: jnp.clip((tq // TOK) * i + rel, 0, n_chunks - 1))
    kspec = lambda f: pl.BlockSpec((1, TOK, LANES), lambda bi, i: (bi, f(i), 0))
    vspec = lambda f: pl.BlockSpec((1, 1, 2 * VROWS, TOK), lambda bi, i: (bi, f(i), 0, 0))
    return pl.pallas_call(
        functools.partial(_window_body, n_chunks=n_chunks),
        grid=(b, s // tq),
        in_specs=[pl.BlockSpec(memory_space=pltpu.SMEM),
                  pl.BlockSpec((1, nh, QPAD, tq), lambda bi, i: (bi, 0, 0, i)),
                  pl.BlockSpec((1, sc, LANES), lambda bi, i: (bi, 0, 0)),
                  pl.BlockSpec((1, 1, 2 * VROWS, sc), lambda bi, i: (bi, 0, 0, 0))]
                 + [kspec(chunk(rel)) for rel in WINDOW_CHUNKS] + [vspec(chunk(rel)) for rel in WINDOW_CHUNKS],
        out_specs=pl.BlockSpec((1, nh, HEAD_DIM, tq), lambda bi, i: (bi, 0, 0, i)),
        out_shape=jax.ShapeDtypeStruct((b, nh, HEAD_DIM, s), BF16),
        scratch_shapes=[pltpu.VMEM((TOK, nh * tq), F32), pltpu.VMEM((TOK, nh * tq), F32),
                        pltpu.VMEM((VROWS, nh * tq), F32)],
        compiler_params=_params("arbitrary", "arbitrary"),
        name="window_attn",
    )(sink, q, k_ctx, v_ctx, *([k_lat] * nw), *([v_lat] * nw))


def _outproj_body(oa_ref, ob_ref, oc_ref, od_ref, lam_ref, subln_ref, w_ref, x_ref, gt_ref,
                  g_ref, sh_ref, sc_ref, wr_ref, xo_ref, h_ref, lg_ref, *, lam_init):
    lp = lam_ref[...]
    lam = (jnp.exp(jnp.sum(lp[0:1] * lp[1:2], axis=1, keepdims=True))
           - jnp.exp(jnp.sum(lp[2:3] * lp[3:4], axis=1, keepdims=True)) + lam_init)
    parts = [oa_ref[0, hh] for hh in range(4)] + [ob_ref[0, hh] for hh in range(4)]
    for hh in range(C_HEADS):
        o = oc_ref[0, 2 * hh] - lam * oc_ref[0, 2 * hh + 1]
        parts.append((_rms_t(o, subln_ref[...]) * (1.0 - lam_init)).astype(BF16))
    parts += [od_ref[0, hh] for hh in range(4)]
    ot = jnp.concatenate(parts, axis=0)
    out = lax.dot_general(ot, w_ref[...], TN_DIMS, preferred_element_type=F32)
    xn = x_ref[0] + gt_ref[0] * out
    xo_ref[0] = xn
    h = _modulate(xn, g_ref[...], sh_ref[0], sc_ref[0])
    h_hi = h.astype(BF16)
    h_ref[0] = h_hi
    h_lo = (h - h_hi.astype(F32)).astype(BF16)
    both = lax.dot_general(wr_ref[...], h_hi, NT_DIMS, preferred_element_type=F32)
    low = lax.dot_general(wr_ref[0:N_EXPERTS], h_lo, NT_DIMS, preferred_element_type=F32)
    lg_ref[0] = both[:N_EXPERTS] + both[N_EXPERTS:] + low


def _outproj(oa, ob, oc, od, lam_p, subln, w_out, x, gate, g_ffn, shift, scale, w_router_t, lam_init):
    b, s, d = x.shape
    tm = min(2 * TOK, s)
    ospec = lambda nh: pl.BlockSpec((1, nh, HEAD_DIM, tm), lambda bi, i: (bi, 0, 0, i))
    full = lambda a: pl.BlockSpec(a.shape, lambda bi, i: (0,) * a.ndim)
    row = pl.BlockSpec((1, 1, d), lambda bi, i: (bi, 0, 0))
    return pl.pallas_call(
        functools.partial(_outproj_body, lam_init=lam_init),
        grid=(b, s // tm),
        in_specs=[ospec(4), ospec(4), ospec(8), ospec(4), full(lam_p), full(subln), full(w_out),
                  pl.BlockSpec((1, tm, d), lambda bi, i: (bi, i, 0)), row, full(g_ffn), row, row,
                  full(w_router_t)],
        out_specs=[pl.BlockSpec((1, tm, d), lambda bi, i: (bi, i, 0)),
                   pl.BlockSpec((1, tm, d), lambda bi, i: (bi, i, 0)),
                   pl.BlockSpec((1, N_EXPERTS, tm), lambda bi, i: (bi, 0, i))],
        out_shape=[jax.ShapeDtypeStruct((b, s, d), F32), jax.ShapeDtypeStruct((b, s, d), BF16),
                   jax.ShapeDtypeStruct((b, N_EXPERTS, s), F32)],
        compiler_params=_params("arbitrary", "arbitrary"),
        name="outproj",
    )(oa, ob, oc, od, lam_p, subln, w_out, x, gate, g_ffn, shift, scale, w_router_t)


def _router_body(lg_ref, slot_ref, aff_ref, st_ref, *, cap, nblk):
    lg = lg_ref[0]
    ex = jnp.exp(lg - jnp.max(lg, axis=0, keepdims=True))
    aff = ex / jnp.sum(ex, axis=0, keepdims=True)
    aff_ref[0] = aff

    def search(_, bounds):
        lo, hi = bounds
        mid = (lo + hi) * 0.5
        enough = jnp.sum((aff >= mid).astype(I32), axis=1, keepdims=True) >= cap
        return jnp.where(enough, mid, lo), jnp.where(enough, hi, mid)

    lo, hi = lax.fori_loop(0, BISECT_STEPS, search,
                           (jnp.zeros((N_EXPERTS, 1), F32), jnp.full((N_EXPERTS, 1), 2.0, F32)))
    gt = aff >= hi
    eq = (aff >= lo) & jnp.logical_not(gt)
    need = (cap - jnp.sum(gt.astype(I32), axis=1, keepdims=True)).astype(F32)
    tri = (lax.broadcasted_iota(I32, (TOK, TOK), 0) <= lax.broadcasted_iota(I32, (TOK, TOK), 1)).astype(BF16)
    lane = lax.broadcasted_iota(I32, (N_EXPERTS, LANES), 1)
    eq_seen = jnp.zeros((N_EXPERTS, 1), F32)
    base = jnp.zeros((N_EXPERTS, 1), I32)
    starts = jnp.zeros((N_EXPERTS, LANES), I32)
    for j in range(nblk):
        cols = slice(TOK * j, TOK * j + TOK)
        eq_c = eq[:, cols]
        eq_cum = jnp.dot(eq_c.astype(BF16), tri, preferred_element_type=F32) + eq_seen
        eq_seen = eq_seen + jnp.sum(eq_c.astype(F32), axis=1, keepdims=True)
        sel = gt[:, cols] | (eq_c & (eq_cum <= need))
        cum = jnp.dot(sel.astype(BF16), tri, preferred_element_type=F32).astype(I32)
        slot_ref[0, :, cols] = jnp.where(sel, base + cum - 1, -1)
        starts = jnp.where(lane == j, base, starts)
        cnt = jnp.sum(sel.astype(I32), axis=1, keepdims=True)
        base = base + jnp.bitwise_and(cnt + (SUBLANES - 1), -SUBLANES)
    st_ref[0] = jnp.where(lane == nblk, base, starts)


def _router(logits_t, cap):
    b, e, s = logits_t.shape
    nblk = s // TOK
    return pl.pallas_call(
        functools.partial(_router_body, cap=cap, nblk=nblk),
        grid=(b,),
        in_specs=[pl.BlockSpec((1, e, s), lambda bi: (bi, 0, 0))],
        out_specs=[pl.BlockSpec((1, e, s), lambda bi: (bi, 0, 0)),
                   pl.BlockSpec((1, e, s), lambda bi: (bi, 0, 0)),
                   pl.BlockSpec((1, e, LANES), lambda bi: (bi, 0, 0))],
        out_shape=[jax.ShapeDtypeStruct((b, e, s), I32), jax.ShapeDtypeStruct((b, e, s), F32),
                   jax.ShapeDtypeStruct((b, e, LANES), I32)],
        compiler_params=_params("arbitrary"),
        name="router",
    )(logits_t)


def _onehot_t(slot_ref, first, win):
    rows = lax.broadcasted_iota(I32, (win, TOK), 0)
    return [rows == (slot_ref[0, e:e + 1, :] - first[e]) for e in range(N_EXPERTS)]


def _dispatch_body(st_ref, h_ref, slot_ref, aff_ref, xs_ref, stage, sems, count, *, win):
    bi, j = pl.program_id(0), pl.program_id(1)
    d = h_ref.shape[-1]

    @pl.when((bi == 0) & (j == 0))
    def _():
        count[0] = 0

    rows_alloc = xs_ref.shape[2]
    last_block = j == pl.num_programs(1) - 1
    st = [st_ref[bi, e, j] for e in range(N_EXPERTS)]
    ends = [jnp.where(last_block, rows_alloc, st_ref[bi, e, j + 1]) for e in range(N_EXPERTS)]
    width = ends[0] - st[0]
    for e in range(1, N_EXPERTS):
        width = jnp.maximum(width, ends[e] - st[e])
    rounds = jnp.maximum((width + win - 1) // win, 1)

    def copies(buf, first):
        return [pltpu.make_async_copy(stage.at[buf, pl.ds(e * win, win), :],
                                      xs_ref.at[bi, e, pl.ds(pl.multiple_of(first[e], SUBLANES), win), :],
                                      sems.at[e]) for e in range(N_EXPERTS)]

    def one_round(r, carry):
        k = count[0]
        buf = k % 2
        first = [jnp.minimum(st[e] + r * win, rows_alloc - win) for e in range(N_EXPERTS)]
        hot = _onehot_t(slot_ref, first, win)
        p = jnp.concatenate(hot, axis=0).astype(BF16)
        stage[buf, :, 0:d] = jnp.dot(p, h_ref[0], preferred_element_type=F32)
        gates = [jnp.sum(jnp.where(hot[e], aff_ref[0, e:e + 1, :], 0.0), axis=1, keepdims=True)
                 for e in range(N_EXPERTS)]
        stage[buf, :, d:d + LANES] = jnp.broadcast_to(jnp.concatenate(gates, axis=0), (N_EXPERTS * win, LANES))

        @pl.when(k == 0)
        def _():
            for c in copies(buf, first):
                c.start()

        @pl.when(k > 0)
        def _():
            for done, c in zip(copies(1 - buf, st), copies(buf, first)):
                done.wait()
                c.start()

        count[0] = k + 1
        return carry

    lax.fori_loop(0, rounds, one_round, 0)

    @pl.when((bi == pl.num_programs(0) - 1) & (j == pl.num_programs(1) - 1))
    def _():
        for c in copies(0, st):
            c.wait()


def _dispatch(starts, h, slot, aff, rows_alloc, win):
    b, s, d = h.shape
    nblk = s // TOK
    e = N_EXPERTS
    return pl.pallas_call(
        functools.partial(_dispatch_body, win=win),
        grid_spec=pltpu.PrefetchScalarGridSpec(
            num_scalar_prefetch=1, grid=(b, nblk),
            in_specs=[pl.BlockSpec((1, TOK, d), lambda bi, j, st: (bi, j, 0)),
                      pl.BlockSpec((1, e, TOK), lambda bi, j, st: (bi, 0, j)),
                      pl.BlockSpec((1, e, TOK), lambda bi, j, st: (bi, 0, j))],
            out_specs=pl.BlockSpec(memory_space=pl.ANY),
            scratch_shapes=[pltpu.VMEM((2, e * win, d + LANES), F32),
                            pltpu.SemaphoreType.DMA((e,)),
                            pltpu.SMEM((1,), I32)]),
        out_shape=jax.ShapeDtypeStruct((b, e, rows_alloc, d + LANES), F32),
        compiler_params=_params("arbitrary", "arbitrary"),
        name="dispatch",
    )(starts, h, slot, aff)


def _ffn_body(*refs, layer, sets, steps):
    n = len(sets)
    st_refs, xs_refs = refs[:n], refs[n:2 * n]
    wg_in, wu_in, wd_in = refs[2 * n: 2 * n + 3]
    y_refs = refs[2 * n + 3: 3 * n + 3]
    bufs, sems = refs[3 * n + 3: 3 * n + 6], refs[3 * n + 6]
    wg_ref, wu_ref, wd_ref = refs[3 * n + 7: 3 * n + 10]
    e, bi, i = pl.program_id(0), pl.program_id(1), pl.program_id(2)
    d = y_refs[0].shape[-1]

    def fetch(expert, slot, piece):
        k, p = divmod(piece, WEIGHT_PIECES)
        w, buf = (wg_in, wu_in, wd_in)[k], bufs[k]
        rows = w.shape[2] // WEIGHT_PIECES
        return pltpu.make_async_copy(w.at[layer, expert, pl.ds(p * rows, rows), :],
                                     buf.at[slot, pl.ds(p * rows, rows), :], sems.at[slot, piece])

    pieces = range(3 * WEIGHT_PIECES)
    step = bi * pl.num_programs(2) + i
    slot = e % 2

    @pl.when(step == 0)
    def _():
        @pl.when(e == 0)
        def _():
            for piece in pieces:
                fetch(0, 0, piece).start()

        for piece in pieces:
            fetch(e, slot, piece).wait()
        for out, buf in zip((wg_ref, wu_ref, wd_ref), bufs):
            out[...] = buf[slot].astype(BF16)

    for piece in pieces:
        @pl.when((step == min(piece, steps - 1)) & (e + 1 < pl.num_programs(0)))
        def _(piece=piece):
            fetch(e + 1, 1 - slot, piece).start()

    for (first, count, nblk), st_ref, xs_ref, y_ref in zip(sets, st_refs, xs_refs, y_refs):
        tm = xs_ref.shape[2]
        used = st_ref[bi, e, nblk]
        base = (i - first) * tm
        mine = (i >= first) & (i < first + count)

        def run(rows, xs_ref=xs_ref, y_ref=y_ref, tm=tm, used=used, base=base):
            valid = (base + lax.broadcasted_iota(I32, (rows, 1), 0)) < used
            xa = xs_ref[0, 0, 0:rows, :]
            x = jnp.where(valid, xa[:, 0:d], 0.0).astype(BF16)
            gate = jnp.where(valid, xa[:, d:d + 1], 0.0)
            a = jnp.dot(x, wg_ref[...], preferred_element_type=F32)
            u = jnp.dot(x, wu_ref[...], preferred_element_type=F32)
            mid = (a * jax.nn.sigmoid(a) * u).astype(BF16)
            y_ref[0, 0, 0:rows, :] = (jnp.dot(mid, wd_ref[...], preferred_element_type=F32) * gate).astype(BF16)
            if rows < tm:
                y_ref[0, 0, rows:tm, :] = jnp.zeros((tm - rows, d), BF16)

        piece = FFN_STEP if tm % FFN_STEP == 0 else tm
        left = used - base
        for rows in range(piece, tm + 1, piece):
            covers = (left > rows - piece) if rows == tm else (left > rows - piece) & (left <= rows)
            pl.when(mine & covers)(functools.partial(run, rows))

        @pl.when(mine & (left <= 0))
        def _(y_ref=y_ref):
            y_ref[0, 0] = jnp.zeros(y_ref.shape[2:], BF16)


def _ffn(routed, wg, wu, wd, layer):
    b, e = routed[0]["xs"].shape[:2]
    d, f = wg.shape[-2:]
    sets, first = [], 0
    for r in routed:
        count = r["rows"] // r["tm"]
        sets.append((first, count, r["nblk"]))
        first += count

    def tile_spec(r, first_tile, count, width):
        def index(ei, bi, i, *st):
            return bi, ei, jnp.clip(i - first_tile, 0, count - 1), 0
        return pl.BlockSpec((1, 1, r["tm"], width), index)

    in_specs = [tile_spec(r, s[0], s[1], d + LANES) for r, s in zip(routed, sets)]
    out_specs = [tile_spec(r, s[0], s[1], d) for r, s in zip(routed, sets)]
    out_shape = [jax.ShapeDtypeStruct((b, e, r["rows"], d), BF16) for r in routed]
    scratch = ([pltpu.VMEM((2,) + w.shape[2:], F32) for w in (wg, wu, wd)]
               + [pltpu.SemaphoreType.DMA((2, 3 * WEIGHT_PIECES))]
               + [pltpu.VMEM(w.shape[2:], BF16) for w in (wg, wu, wd)])
    return pl.pallas_call(
        functools.partial(_ffn_body, layer=layer, sets=tuple(sets), steps=b * first),
        grid_spec=pltpu.PrefetchScalarGridSpec(
            num_scalar_prefetch=len(routed), grid=(e, b, first),
            in_specs=in_specs + [pl.BlockSpec(memory_space=pl.ANY)] * 3,
            out_specs=out_specs, scratch_shapes=scratch),
        out_shape=out_shape,
        compiler_params=_params("arbitrary", "arbitrary", "arbitrary"),
        name="expert_ffn",
    )(*[r["starts"] for r in routed], *[r["xs"] for r in routed], wg, wu, wd)


def _combine_body(st_ref, slot_ref, x_ref, gt_ref, gf_ref, y_ref, o_ref, ybuf, sems, *, win, rows, final_norm):
    bi, j = pl.program_id(0), pl.program_id(1)
    nj = pl.num_programs(1)
    d = x_ref.shape[-1]
    step = bi * nj + j
    cur = step % 2
    st = [st_ref[bi, e, j] for e in range(N_EXPERTS)]
    width = st_ref[bi, 0, j + 1] - st[0]
    for e in range(1, N_EXPERTS):
        width = jnp.maximum(width, st_ref[bi, e, j + 1] - st[e])
    rounds = (width + win - 1) // win

    wrows = win + BF16_ROWS

    def window(b_, j_, r):
        lo = [st_ref[b_, e, j_] + r * win for e in range(N_EXPERTS)]
        first = [jnp.minimum(jnp.bitwise_and(lo[e], -BF16_ROWS), rows - wrows) for e in range(N_EXPERTS)]
        return lo, first

    def copies(b_, first, buf):
        return [pltpu.make_async_copy(y_ref.at[b_, e, pl.ds(pl.multiple_of(first[e], BF16_ROWS), wrows), :],
                                      ybuf.at[buf, pl.ds(e * wrows, wrows), :], sems.at[buf, e])
                for e in range(N_EXPERTS)]

    def scatter(lo, first, buf, acc):
        hot = _onehot_t(slot_ref, first, wrows)
        hot = [hot[e] & (slot_ref[0, e:e + 1, :] >= lo[e]) & (slot_ref[0, e:e + 1, :] < lo[e] + win)
               for e in range(N_EXPERTS)]
        p = jnp.concatenate(hot, axis=0).astype(BF16)
        return acc + lax.dot_general(p, ybuf[buf], TN_DIMS, preferred_element_type=F32)

    lo0, first0 = window(bi, j, 0)

    @pl.when(step == 0)
    def _():
        for c in copies(bi, first0, cur):
            c.start()

    @pl.when(step < pl.num_programs(0) * nj - 1)
    def _():
        wrap = j == nj - 1
        b_next, j_next = jnp.where(wrap, bi + 1, bi), jnp.where(wrap, 0, j + 1)
        for c in copies(b_next, window(b_next, j_next, 0)[1], 1 - cur):
            c.start()

    for c in copies(bi, first0, cur):
        c.wait()
    acc = scatter(lo0, first0, cur, jnp.zeros((TOK, d), F32))

    def later_round(r, acc):
        lo, first = window(bi, j, r)
        for c in copies(bi, first, 2):
            c.start()
        for c in copies(bi, first, 2):
            c.wait()
        return scatter(lo, first, 2, acc)

    acc = lax.fori_loop(1, rounds, later_round, acc)
    out = x_ref[0] + gt_ref[0] * acc
    if final_norm:
        ms = jnp.mean(out * out, axis=-1, keepdims=True)
        out = out * lax.rsqrt(ms + EPS) * gf_ref[...]
    o_ref[0] = out


def _combine(starts, slot, x, gate, g_final, y, win, final_norm):
    b, s, d = x.shape
    e = N_EXPERTS
    rows = y.shape[2]
    return pl.pallas_call(
        functools.partial(_combine_body, win=win, rows=rows, final_norm=final_norm),
        grid_spec=pltpu.PrefetchScalarGridSpec(
            num_scalar_prefetch=1, grid=(b, s // TOK),
            in_specs=[pl.BlockSpec((1, e, TOK), lambda bi, j, st: (bi, 0, j)),
                      pl.BlockSpec((1, TOK, d), lambda bi, j, st: (bi, j, 0)),
                      pl.BlockSpec((1, 1, d), lambda bi, j, st: (bi, 0, 0)),
                      pl.BlockSpec((1, d), lambda bi, j, st: (0, 0)),
                      pl.BlockSpec(memory_space=pl.ANY)],
            out_specs=pl.BlockSpec((1, TOK, d), lambda bi, j, st: (bi, j, 0)),
            scratch_shapes=[pltpu.VMEM((3, e * (win + BF16_ROWS), d), BF16), pltpu.SemaphoreType.DMA((3, e))]),
        out_shape=jax.ShapeDtypeStruct((b, s, d), F32),
        compiler_params=_params("arbitrary", "arbitrary"),
        name="combine",
    )(starts, slot, x, gate, g_final, y)


def _route(h, logits_t):
    n = h.shape[1]
    nblk = n // TOK
    cap = CAPACITY_FACTOR * n // N_EXPERTS
    win = min(SLOT_WINDOW, cap)
    rows = cap + SUBLANES * nblk
    rows = -(-rows // BF16_ROWS) * BF16_ROWS
    tm = min(FFN_TILE, rows)
    rows = -(-rows // tm) * tm
    slot, aff, starts = _router(logits_t, cap)
    xs = _dispatch(starts, h, slot, aff, rows + win, win)
    return dict(slot=slot, starts=starts, xs=xs, win=win, rows=rows, tm=tm, nblk=nblk)


def _rope_tables(n_tokens, rot_dim):
    rows = n_tokens // GRID_W
    row = jnp.repeat(jnp.arange(rows), GRID_W)
    col = jnp.tile(jnp.arange(GRID_W), rows)
    quarter = rot_dim // 4
    inv_freq = ROPE_THETA ** (-jnp.arange(quarter, dtype=F32) / quarter)
    ang = jnp.concatenate([inv_freq[:, None] * row[None, :], inv_freq[:, None] * col[None, :]], axis=0)
    return jnp.cos(ang), jnp.sin(ang)


def _identity_tables(n_tokens, rot_dim):
    return jnp.ones((rot_dim // 2, n_tokens), F32), jnp.zeros((rot_dim // 2, n_tokens), F32)


def kernel(x, c, ctx, c_ctx, w_mod, b_mod, g_attn, g_ffn, w_in, a_sink, b_q_norm, b_w_uq, b_kv_norm, b_w_ukv, c_lambda, c_subln, d_q_norm, d_k_norm, w_out, w_router, w_gate, w_up, w_down, g_final):
    b, s, d = x.shape
    depth = w_mod.shape[0]
    n_ctx = ctx.shape[1]
    rope64, rope32 = _rope_tables(s, HEAD_DIM), _rope_tables(s, C_QK)
    id64, id32 = _identity_tables(n_ctx, HEAD_DIM), _identity_tables(n_ctx, C_QK)

    cond = jnp.zeros((SUBLANES, d), F32).at[:b].set(c).at[b].set(c_ctx)
    mod = _adaln(cond, w_mod, b_mod)

    col = lambda v: v.reshape(-1, 1)
    xl, xc = x, ctx
    for l in range(depth):
        last = l == depth - 1
        lam_init = 0.8 - 0.6 * math.exp(-0.3 * l)
        m6 = mod[l].reshape(SUBLANES, 6, d)
        lat = [m6[:b, k][:, None, :] for k in range(6)]
        cx = [jnp.broadcast_to(m6[b, k][None, None, :], (b, 1, d)) for k in range(6)]
        w_in_t = w_in[l].T.astype(BF16)
        wuq_t = b_w_uq[l].T.astype(BF16)
        wukv_t = b_w_ukv[l].T.astype(BF16)
        w_out_b = w_out[l].astype(BF16)
        w_router_t = w_router[l].T
        w_router_hi = w_router_t.astype(BF16)
        w_router_t = jnp.concatenate([w_router_hi, (w_router_t - w_router_hi.astype(F32)).astype(BF16)], axis=0)
        g_a, g_f = g_attn[l][None, :], g_ffn[l][None, :]
        small = (col(b_q_norm[l]), wuq_t, col(b_kv_norm[l]), wukv_t, col(d_q_norm[l]), col(d_k_norm[l]))

        pl_ = _proj(xl, g_a, lat[0], lat[1], w_in_t, rope64, rope32, *small)
        pc_ = _proj(xc, g_a, cx[0], cx[1], w_in_t, id64, id32, *small)
        qa, ka, va, qb, kb, vb, qc, kc, vc, qd, kd, vd = pl_
        qa_c, ka_c, va_c, qb_c, kb_c, vb_c, qc_c, kc_c, vc_c, qd_c, kd_c, vd_c = pc_

        zero = lambda h: 0
        oa = _window_attn(qa, ka_c, va_c, ka, va, a_sink[l])
        ob = _dense_attn(qb, kb_c, vb_c, kb, vb, lambda h: h, lambda h: h)
        oc = _dense_attn(qc, kc_c, vc_c, kc, vc, lambda h: h // 4, lambda h: h // 2, out_dtype=F32)
        od = _dense_attn(qd, kd_c, vd_c, kd, vd, zero, lambda h: h // 2)
        outproj = functools.partial(_outproj, lam_p=c_lambda[l], subln=col(c_subln[l]), w_out=w_out_b,
                                    g_ffn=g_f, w_router_t=w_router_t, lam_init=lam_init)
        xl, hl, lg = outproj(oa, ob, oc, od, x=xl, gate=lat[2], shift=lat[3], scale=lat[4])
        routed = [_route(hl, lg)]
        if not last:
            oa = _dense_attn(qa_c, ka_c, va_c, None, None, zero, lambda h: h // 2, sink=a_sink[l])
            ob = _dense_attn(qb_c, kb_c, vb_c, None, None, lambda h: h, lambda h: h)
            oc = _dense_attn(qc_c, kc_c, vc_c, None, None, lambda h: h // 4, lambda h: h // 2, out_dtype=F32)
            od = _dense_attn(qd_c, kd_c, vd_c, None, None, zero, lambda h: h // 2)
            xc, hc, lgc = outproj(oa, ob, oc, od, x=xc, gate=cx[2], shift=cx[3], scale=cx[4])
            routed.append(_route(hc, lgc))
        ys = _ffn(routed, w_gate, w_up, w_down, l)
        combine = lambda r, y, x_, gate, norm: _combine(r["starts"], r["slot"], x_, gate, g_final[None, :], y,
                                                        r["win"], norm)
        xl = combine(routed[0], ys[0], xl, lat[5], last)
        if not last:
            xc = combine(routed[1], ys[1], xc, cx[5], False)
    return xl
```

```python
import functools
import math

import jax
import jax.numpy as jnp
from jax import lax
from jax.experimental import pallas as pl
from jax.experimental.pallas import tpu as pltpu

F32, BF16, I32 = jnp.float32, jnp.bfloat16, jnp.int32
HIGHEST = lax.Precision.HIGHEST

SUBLANES = 8
LANES = 128
BF16_ROWS = 16

GRID_W = 64
HEAD_DIM = 64
WINDOW = 128
EPS = 1e-6
ROPE_THETA = 10000.0
NEG_INF = -1e30
A_HEADS, A_KV = 4, 2
B_HEADS, B_Q_RANK, B_KV_RANK, B_NOPE, B_ROPE, B_V = 4, 192, 128, 64, 32, 64
C_HEADS, C_QK, C_V = 4, 32, 64
D_HEADS, D_KV = 4, 2
N_EXPERTS = 16
CAPACITY_FACTOR = 2

A_COLS = A_HEADS * HEAD_DIM + 2 * A_KV * HEAD_DIM
B_COLS = B_Q_RANK + B_KV_RANK + B_ROPE
C_COLS = 4 * C_HEADS * C_QK + C_HEADS * C_V
D_COLS = D_HEADS * HEAD_DIM + 2 * D_KV * HEAD_DIM
A_OFF, B_OFF, C_OFF, D_OFF = 0, A_COLS, A_COLS + B_COLS, A_COLS + B_COLS + C_COLS

BISECT_STEPS = 152
TOK = 256
QPAD = 128
VROWS = HEAD_DIM + 16
TQ = 1024
COL_BLOCK = 256
ATTN_UNROLL = 8
KEY_CHUNK = 256
WINDOW_CHUNKS = (0, -1, 1)
FFN_TILE = 640
FFN_STEP = 128
WEIGHT_PIECES = 2
SLOT_WINDOW = 48
LOG2E = math.log2(math.e)

NT_DIMS = (((1,), (1,)), ((), ()))
TN_DIMS = (((0,), (0,)), ((), ()))


def _params(*sem):
    return pltpu.CompilerParams(dimension_semantics=sem, vmem_limit_bytes=56 * 1024 * 1024)


def _adaln_body(c_ref, w_ref, b_ref, o_ref):
    c = c_ref[...]
    s = c * jax.nn.sigmoid(c)
    o_ref[0] = jnp.dot(s, w_ref[0], precision=HIGHEST, preferred_element_type=F32) + b_ref[0]


def _adaln(cond, w_mod, b_mod):
    depth, d, n = w_mod.shape
    tn = n // 4
    return pl.pallas_call(
        _adaln_body,
        grid=(depth, n // tn),
        in_specs=[pl.BlockSpec((SUBLANES, d), lambda l, j: (0, 0)),
                  pl.BlockSpec((1, d, tn), lambda l, j: (l, 0, j)),
                  pl.BlockSpec((1, 1, tn), lambda l, j: (l, 0, j))],
        out_specs=pl.BlockSpec((1, SUBLANES, tn), lambda l, j: (l, 0, j)),
        out_shape=jax.ShapeDtypeStruct((depth, SUBLANES, n), F32),
        compiler_params=_params("arbitrary", "arbitrary"),
        name="adaln",
    )(cond, w_mod, b_mod.reshape(depth, 1, n))


def _rope_t(xt, cos, sin):
    half = xt.shape[0] // 2
    x1, x2 = xt[:half], xt[half:]
    return jnp.concatenate([x1 * cos - x2 * sin, x1 * sin + x2 * cos], axis=0)


def _rms_t(xt, g):
    ms = jnp.mean(xt * xt, axis=0, keepdims=True)
    return xt * lax.rsqrt(ms + EPS) * g


def _modulate(x, g, shift, scale):
    ms = jnp.mean(x * x, axis=-1, keepdims=True)
    return x * lax.rsqrt(ms + EPS) * g * (1.0 + scale) + shift


def _proj_body(x_ref, g_ref, sh_ref, sc_ref, w_ref, c64_ref, s64_ref, c32_ref, s32_ref,
               bqn_ref, wuq_ref, bkvn_ref, wukv_ref, dqn_ref, dkn_ref,
               qa_ref, ka_ref, va_ref, qb_ref, kb_ref, vb_ref,
               qc_ref, kc_ref, vc_ref, qd_ref, kd_ref, vd_ref):
    tm = x_ref.shape[1]
    h = _modulate(x_ref[0], g_ref[...], sh_ref[0], sc_ref[0]).astype(BF16)
    pt = lax.dot_general(w_ref[...], h, NT_DIMS, preferred_element_type=F32)
    c64, s64, c32, s32 = c64_ref[...], s64_ref[...], c32_ref[...], s32_ref[...]
    z64 = jnp.zeros((64, tm), F32)
    z32 = jnp.zeros((32, tm), F32)
    scale64 = HEAD_DIM ** -0.5 * LOG2E
    scale_b = (B_NOPE + B_ROPE) ** -0.5 * LOG2E
    scale_c = C_QK ** -0.5 * LOG2E
    def put_values(v_ref, hh, vt):
        chunk = v_ref.shape[-1]
        for c in range(tm // chunk):
            v_ref[0, c, VROWS * hh: VROWS * hh + HEAD_DIM] = vt[:, chunk * c: chunk * c + chunk].astype(BF16)
            v_ref[0, c, VROWS * hh + HEAD_DIM: VROWS * hh + VROWS] = jnp.ones((VROWS - HEAD_DIM, chunk), BF16)

    def gqa(off, q_ref, k_ref, v_ref, qn, kn):
        for hh in range(4):
            q = pt[off + 64 * hh: off + 64 * hh + 64]
            if qn is not None:
                q = _rms_t(q, qn)
            q = _rope_t(q, c64, s64) * scale64
            q_ref[0, hh] = (jnp.concatenate([q, z64], axis=0) if hh // 2 == 0
                            else jnp.concatenate([z64, q], axis=0)).astype(BF16)
        ks = []
        for j in range(2):
            k = pt[off + 256 + 64 * j: off + 256 + 64 * j + 64]
            if kn is not None:
                k = _rms_t(k, kn)
            ks.append(_rope_t(k, c64, s64))
        k_ref[0] = jnp.concatenate(ks, axis=0).T.astype(BF16)
        for j in range(2):
            put_values(v_ref, j, pt[off + 384 + 64 * j: off + 384 + 64 * j + 64])

    gqa(A_OFF, qa_ref, ka_ref, va_ref, None, None)
    gqa(D_OFF, qd_ref, kd_ref, vd_ref, dqn_ref[...], dkn_ref[...])

    cq = _rms_t(pt[B_OFF: B_OFF + B_Q_RANK], bqn_ref[...]).astype(BF16)
    ckv = _rms_t(pt[B_OFF + B_Q_RANK: B_OFF + B_Q_RANK + B_KV_RANK], bkvn_ref[...]).astype(BF16)
    kr = _rope_t(pt[B_OFF + B_Q_RANK + B_KV_RANK: B_OFF + B_COLS], c32, s32)
    qt = jnp.dot(wuq_ref[...], cq, preferred_element_type=F32)
    kvt = jnp.dot(wukv_ref[...], ckv, preferred_element_type=F32)
    kparts = []
    dq = B_NOPE + B_ROPE
    for hh in range(B_HEADS):
        qn_ = qt[dq * hh: dq * hh + B_NOPE]
        qr = _rope_t(qt[dq * hh + B_NOPE: dq * hh + dq], c32, s32)
        qb_ref[0, hh] = (jnp.concatenate([qn_, qr, z32], axis=0) * scale_b).astype(BF16)
        kparts.append(jnp.concatenate([kvt[128 * hh: 128 * hh + B_NOPE], kr, z32], axis=0))
        put_values(vb_ref, hh, kvt[128 * hh + B_NOPE: 128 * hh + 128])
    kb_ref[0] = jnp.concatenate(kparts, axis=0).T.astype(BF16)

    for j in range(2 * C_HEADS):
        q = _rope_t(pt[C_OFF + 32 * j: C_OFF + 32 * j + 32], c32, s32) * scale_c
        pieces = [z32, z32, z32, z32]
        pieces[j % 4] = q
        qc_ref[0, j] = jnp.concatenate(pieces, axis=0).astype(BF16)
    kc = [_rope_t(pt[C_OFF + 256 + 32 * j: C_OFF + 256 + 32 * j + 32], c32, s32) for j in range(2 * C_HEADS)]
    kc_ref[0] = jnp.concatenate(kc, axis=0).T.astype(BF16)
    for hh in range(C_HEADS):
        put_values(vc_ref, hh, pt[C_OFF + 512 + 64 * hh: C_OFF + 512 + 64 * hh + 64])


def _proj(x, g, shift, scale, w_in_t, rope64, rope32, bqn, wuq_t, bkvn, wukv_t, dqn, dkn):
    b, s, d = x.shape
    tm = min(2 * TOK, s)
    n = s // tm
    full = lambda a: pl.BlockSpec(a.shape, lambda bi, i: (0,) * a.ndim)
    q_spec = lambda nh: pl.BlockSpec((1, nh, QPAD, tm), lambda bi, i: (bi, 0, 0, i))
    k_spec = lambda w: pl.BlockSpec((1, tm, w), lambda bi, i: (bi, i, 0))
    kch = min(KEY_CHUNK, tm)
    v_spec = lambda r, c=kch: pl.BlockSpec((1, tm // c, r, c), lambda bi, i: (bi, i, 0, 0))
    q_shape = lambda nh: jax.ShapeDtypeStruct((b, nh, QPAD, s), BF16)
    k_shape = lambda w: jax.ShapeDtypeStruct((b, s, w), BF16)
    v_shape = lambda r, c=kch: jax.ShapeDtypeStruct((b, s // c, r, c), BF16)
    tab = lambda t: pl.BlockSpec((t.shape[0], tm), lambda bi, i: (0, i))
    c64, s64 = rope64
    c32, s32 = rope32
    return pl.pallas_call(
        _proj_body,
        grid=(b, n),
        in_specs=[pl.BlockSpec((1, tm, d), lambda bi, i: (bi, i, 0)), full(g),
                  pl.BlockSpec((1, 1, d), lambda bi, i: (bi, 0, 0)),
                  pl.BlockSpec((1, 1, d), lambda bi, i: (bi, 0, 0)),
                  full(w_in_t), tab(c64), tab(s64), tab(c32), tab(s32),
                  full(bqn), full(wuq_t), full(bkvn), full(wukv_t), full(dqn), full(dkn)],
        out_specs=[q_spec(4), k_spec(128), v_spec(2 * VROWS, TOK),
                   q_spec(4), k_spec(512), v_spec(4 * VROWS),
                   q_spec(8), k_spec(256), v_spec(4 * VROWS),
                   q_spec(4), k_spec(128), v_spec(2 * VROWS)],
        out_shape=[q_shape(4), k_shape(128), v_shape(2 * VROWS, TOK),
                   q_shape(4), k_shape(512), v_shape(4 * VROWS),
                   q_shape(8), k_shape(256), v_shape(4 * VROWS),
                   q_shape(4), k_shape(128), v_shape(2 * VROWS)],
        compiler_params=_params("arbitrary", "arbitrary"),
        name="proj",
    )(x, g, shift, scale, w_in_t, c64, s64, c32, s32, bqn, wuq_t, bkvn, wukv_t, dqn, dkn)


def _dense_body(*refs, has_lat, has_sink, n_lat):
    refs = list(refs)
    sink_ref = refs.pop(0) if has_sink else None
    q_ref, kc_ref, vc_ref = refs[:3]
    kl_ref, vl_ref = (refs[3], refs[4]) if has_lat else (None, None)
    o_ref, s_even, s_odd, acc_ref = refs[-4:]
    tq = q_ref.shape[-1]
    qt = q_ref[0, 0]

    def produce(kblk, s_ref):
        s = jnp.dot(kblk, qt, preferred_element_type=F32)
        s_ref[0:kblk.shape[0], :] = s
        return jnp.max(s, axis=0, keepdims=True)

    def consume(s_ref, mx, vblk, m):
        m_new = mx if m is None else jnp.maximum(m, mx)
        p = jnp.exp2(s_ref[0:vblk.shape[1], :] - m_new).astype(BF16)
        pv = jnp.dot(vblk, p, preferred_element_type=F32)
        acc_ref[...] = pv if m is None else jnp.exp2(m - m_new) * acc_ref[...] + pv
        return m_new

    kch = s_even.shape[0]

    def lat_keys(i):
        return kl_ref[0, pl.ds(pl.multiple_of(i * kch, kch), kch), :]

    mx_c = produce(kc_ref[0], s_odd)
    if has_lat:
        mx_e = produce(lat_keys(0), s_even)
    m = consume(s_odd, mx_c, vc_ref[0, 0], None)
    if has_lat:
        ncol = tq // COL_BLOCK

        def overlap(k_next, s_next, s_cur, mx_cur, v_cur, m):
            m_out, mx_out = [], []
            for c in range(ncol):
                cols = slice(COL_BLOCK * c, COL_BLOCK * c + COL_BLOCK)
                s = jnp.dot(k_next, qt[:, cols], preferred_element_type=F32)
                s_next[:, cols] = s
                mx_out.append(jnp.max(s, axis=0, keepdims=True))
                m_old = m[:, cols]
                m_new = jnp.maximum(m_old, mx_cur[:, cols])
                p = jnp.exp2(s_cur[:, cols] - m_new).astype(BF16)
                acc_ref[:, cols] = (jnp.exp2(m_old - m_new) * acc_ref[:, cols]
                                    + jnp.dot(v_cur, p, preferred_element_type=F32))
                m_out.append(m_new)
            return jnp.concatenate(m_out, axis=1), jnp.concatenate(mx_out, axis=1)

        def body(j, carry):
            m, mx_e = carry
            m, mx_o = overlap(lat_keys(2 * j + 1), s_odd, s_even, mx_e, vl_ref[0, 2 * j], m)
            m, mx_e = overlap(lat_keys(jnp.minimum(2 * j + 2, n_lat - 1)), s_even, s_odd, mx_o, vl_ref[0, 2 * j + 1], m)
            return m, mx_e
        m, _ = lax.fori_loop(0, n_lat // 2, body, (m, mx_e), unroll=ATTN_UNROLL)
    acc = acc_ref[...]
    num, den = acc[:HEAD_DIM], acc[HEAD_DIM:HEAD_DIM + 1]
    if has_sink:
        sk = sink_ref[pl.program_id(1)] * LOG2E
        m2 = jnp.maximum(m, sk)
        a = jnp.exp2(m - m2)
        den = den * a + jnp.exp2(sk - m2)
        num = num * a
    o_ref[0, 0] = (num * (1.0 / den)).astype(o_ref.dtype)


def _dense_attn(q, k_ctx, v_ctx, k_lat, v_lat, kgroup, vhead, sink=None, out_dtype=BF16):
    b, nh, _, s = q.shape
    tq = min(TQ, s)
    has_lat = k_lat is not None
    has_sink = sink is not None
    sc = k_ctx.shape[1]
    in_specs = [pl.BlockSpec((1, 1, QPAD, tq), lambda bi, h, i: (bi, h, 0, i)),
                pl.BlockSpec((1, sc, LANES), lambda bi, h, i: (bi, 0, kgroup(h))),
                pl.BlockSpec((1, 1, VROWS, sc), lambda bi, h, i: (bi, 0, vhead(h), 0))]
    args = [q, k_ctx, v_ctx]
    n_lat, kch = 0, sc
    if has_lat:
        sl = k_lat.shape[1]
        kch = v_lat.shape[-1]
        n_lat = sl // kch
        assert n_lat % 2 == 0 and kch >= sc
        in_specs += [pl.BlockSpec((1, sl, LANES), lambda bi, h, i: (bi, 0, kgroup(h))),
                     pl.BlockSpec((1, n_lat, VROWS, kch), lambda bi, h, i: (bi, 0, vhead(h), 0))]
        args += [k_lat, v_lat]
    if has_sink:
        in_specs = [pl.BlockSpec(memory_space=pltpu.SMEM)] + in_specs
        args = [sink] + args
    return pl.pallas_call(
        functools.partial(_dense_body, has_lat=has_lat, has_sink=has_sink, n_lat=n_lat),
        grid=(b, nh, s // tq),
        in_specs=in_specs,
        out_specs=pl.BlockSpec((1, 1, HEAD_DIM, tq), lambda bi, h, i: (bi, h, 0, i)),
        out_shape=jax.ShapeDtypeStruct((b, nh, HEAD_DIM, s), out_dtype),
        scratch_shapes=[pltpu.VMEM((kch, tq), F32), pltpu.VMEM((kch, tq), F32), pltpu.VMEM((VROWS, tq), F32)],
        compiler_params=_params("arbitrary", "arbitrary", "arbitrary"),
        name="dense_attn",
    )(*args)


def _window_body(sink_ref, q_ref, kc_ref, vc_ref, *rest, n_chunks):
    nw = len(WINDOW_CHUNKS)
    k_refs, v_refs = rest[:nw], rest[nw:2 * nw]
    o_ref, s_even, s_odd, acc_ref = rest[2 * nw:]
    i = pl.program_id(1)
    nh, tq = q_ref.shape[1], q_ref.shape[-1]
    width = nh * tq
    qt = jnp.concatenate([q_ref[0, hh] for hh in range(nh)], axis=1)
    dist = lax.broadcasted_iota(I32, (TOK, tq), 0) - lax.broadcasted_iota(I32, (TOK, tq), 1)

    def in_window(rel):
        chunk = (tq // TOK) * i + rel
        return (jnp.abs(dist + (rel * TOK)) <= WINDOW) & (chunk >= 0) & (chunk < n_chunks)

    def produce(k_ref, ok, s_ref):
        s = jnp.dot(k_ref[0], qt, preferred_element_type=F32)
        if ok is not None:
            s = jnp.concatenate([jnp.where(ok, s[:, tq * hh: tq * hh + tq], NEG_INF) for hh in range(nh)], axis=1)
        s_ref[...] = s
        return jnp.max(s, axis=0, keepdims=True)

    def consume(s_ref, mx, v_ref, m):
        m_new = jnp.maximum(m, mx)
        p = jnp.exp2(s_ref[...] - m_new).astype(BF16)
        pv = [jnp.dot(v_ref[0, 0, VROWS * g: VROWS * g + VROWS, :], p[:, 2 * tq * g: 2 * tq * g + 2 * tq],
                      preferred_element_type=F32) for g in range(A_KV)]
        acc_ref[...] = jnp.exp2(m - m_new) * acc_ref[...] + jnp.concatenate(pv, axis=1)
        return m_new

    acc_ref[...] = jnp.zeros(acc_ref.shape, F32)
    bufs = (s_even, s_odd)
    mx = produce(kc_ref, None, s_even)
    m = jnp.full((1, width), NEG_INF, F32)
    prev_v = vc_ref
    for c, rel in enumerate(WINDOW_CHUNKS):
        mx_next = produce(k_refs[c], in_window(rel), bufs[(c + 1) % 2])
        m = consume(bufs[c % 2], mx, prev_v, m)
        mx, prev_v = mx_next, v_refs[c]
    m = consume(bufs[nw % 2], mx, prev_v, m)

    sk = jnp.concatenate([jnp.full((1, tq), sink_ref[hh] * LOG2E, F32) for hh in range(nh)], axis=1)
    m2 = jnp.maximum(m, sk)
    a = jnp.exp2(m - m2)
    acc = acc_ref[...]
    out = acc[:HEAD_DIM] * a / (acc[HEAD_DIM:HEAD_DIM + 1] * a + jnp.exp2(sk - m2))
    for hh in range(nh):
        o_ref[0, hh] = out[:, tq * hh: tq * hh + tq].astype(o_ref.dtype)


def _window_attn(q, k_ctx, v_ctx, k_lat, v_lat, sink):
    b, nh, _, s = q.shape
    tq = TOK
    assert s % tq == 0 and WINDOW <= TOK
    n_chunks = s // TOK
    sc = k_ctx.shape[1]
    nw = len(WINDOW_CHUNKS)
    chunk = lambda rel: (lambda i: jnp.clip((tq // TOK) * i + rel, 0, n_chunks - 1))
    kspec = lambda f: pl.BlockSpec((1, TOK, LANES), lambda bi, i: (bi, f(i), 0))
    vspec = lambda f: pl.BlockSpec((1, 1, 2 * VROWS, TOK), lambda bi, i: (bi, f(i), 0, 0))
    return pl.pallas_call(
        functools.partial(_window_body, n_chunks=n_chunks),
        grid=(b, s // tq),
        in_specs=[pl.BlockSpec(memory_space=pltpu.SMEM),
                  pl.BlockSpec((1, nh, QPAD, tq), lambda bi, i: (bi, 0, 0, i)),
                  pl.BlockSpec((1, sc, LANES), lambda bi, i: (bi, 0, 0)),
                  pl.BlockSpec((1, 1, 2 * VROWS, sc), lambda bi, i: (bi, 0, 0, 0))]
                 + [kspec(chunk(rel)) for rel in WINDOW_CHUNKS] + [vspec(chunk(rel)) for rel in WINDOW_CHUNKS],
        out_specs=pl.BlockSpec((1, nh, HEAD_DIM, tq), lambda bi, i: (bi, 0, 0, i)),
        out_shape=jax.ShapeDtypeStruct((b, nh, HEAD_DIM, s), BF16),
        scratch_shapes=[pltpu.VMEM((TOK, nh * tq), F32), pltpu.VMEM((TOK, nh * tq), F32),
                        pltpu.VMEM((VROWS, nh * tq), F32)],
        compiler_params=_params("arbitrary", "arbitrary"),
        name="window_attn",
    )(sink, q, k_ctx, v_ctx, *([k_lat] * nw), *([v_lat] * nw))


def _outproj_body(oa_ref, ob_ref, oc_ref, od_ref, lam_ref, subln_ref, w_ref, x_ref, gt_ref,
                  g_ref, sh_ref, sc_ref, wr_ref, xo_ref, h_ref, lg_ref, *, lam_init):
    lp = lam_ref[...]
    lam = (jnp.exp(jnp.sum(lp[0:1] * lp[1:2], axis=1, keepdims=True))
           - jnp.exp(jnp.sum(lp[2:3] * lp[3:4], axis=1, keepdims=True)) + lam_init)
    parts = [oa_ref[0, hh] for hh in range(4)] + [ob_ref[0, hh] for hh in range(4)]
    for hh in range(C_HEADS):
        o = oc_ref[0, 2 * hh] - lam * oc_ref[0, 2 * hh + 1]
        parts.append((_rms_t(o, subln_ref[...]) * (1.0 - lam_init)).astype(BF16))
    parts += [od_ref[0, hh] for hh in range(4)]
    ot = jnp.concatenate(parts, axis=0)
    out = lax.dot_general(ot, w_ref[...], TN_DIMS, preferred_element_type=F32)
    xn = x_ref[0] + gt_ref[0] * out
    xo_ref[0] = xn
    h = _modulate(xn, g_ref[...], sh_ref[0], sc_ref[0])
    h_hi = h.astype(BF16)
    h_ref[0] = h_hi
    h_lo = (h - h_hi.astype(F32)).astype(BF16)
    both = lax.dot_general(wr_ref[...], h_hi, NT_DIMS, preferred_element_type=F32)
    low = lax.dot_general(wr_ref[0:N_EXPERTS], h_lo, NT_DIMS, preferred_element_type=F32)
    lg_ref[0] = both[:N_EXPERTS] + both[N_EXPERTS:] + low


def _outproj(oa, ob, oc, od, lam_p, subln, w_out, x, gate, g_ffn, shift, scale, w_router_t, lam_init):
    b, s, d = x.shape
    tm = min(2 * TOK, s)
    ospec = lambda nh: pl.BlockSpec((1, nh, HEAD_DIM, tm), lambda bi, i: (bi, 0, 0, i))
    full = lambda a: pl.BlockSpec(a.shape, lambda bi, i: (0,) * a.ndim)
    row = pl.BlockSpec((1, 1, d), lambda bi, i: (bi, 0, 0))
    return pl.pallas_call(
        functools.partial(_outproj_body, lam_init=lam_init),
        grid=(b, s // tm),
        in_specs=[ospec(4), ospec(4), ospec(8), ospec(4), full(lam_p), full(subln), full(w_out),
                  pl.BlockSpec((1, tm, d), lambda bi, i: (bi, i, 0)), row, full(g_ffn), row, row,
                  full(w_router_t)],
        out_specs=[pl.BlockSpec((1, tm, d), lambda bi, i: (bi, i, 0)),
                   pl.BlockSpec((1, tm, d), lambda bi, i: (bi, i, 0)),
                   pl.BlockSpec((1, N_EXPERTS, tm), lambda bi, i: (bi, 0, i))],
        out_shape=[jax.ShapeDtypeStruct((b, s, d), F32), jax.ShapeDtypeStruct((b, s, d), BF16),
                   jax.ShapeDtypeStruct((b, N_EXPERTS, s), F32)],
        compiler_params=_params("arbitrary", "arbitrary"),
        name="outproj",
    )(oa, ob, oc, od, lam_p, subln, w_out, x, gate, g_ffn, shift, scale, w_router_t)


def _router_body(lg_ref, slot_ref, aff_ref, st_ref, *, cap, nblk):
    lg = lg_ref[0]
    ex = jnp.exp(lg - jnp.max(lg, axis=0, keepdims=True))
    aff = ex / jnp.sum(ex, axis=0, keepdims=True)
    aff_ref[0] = aff

    def search(_, bounds):
        lo, hi = bounds
        mid = (lo + hi) * 0.5
        enough = jnp.sum((aff >= mid).astype(I32), axis=1, keepdims=True) >= cap
        return jnp.where(enough, mid, lo), jnp.where(enough, hi, mid)

    lo, hi = lax.fori_loop(0, BISECT_STEPS, search,
                           (jnp.zeros((N_EXPERTS, 1), F32), jnp.full((N_EXPERTS, 1), 2.0, F32)))
    gt = aff >= hi
    eq = (aff >= lo) & jnp.logical_not(gt)
    need = (cap - jnp.sum(gt.astype(I32), axis=1, keepdims=True)).astype(F32)
    tri = (lax.broadcasted_iota(I32, (TOK, TOK), 0) <= lax.broadcasted_iota(I32, (TOK, TOK), 1)).astype(BF16)
    lane = lax.broadcasted_iota(I32, (N_EXPERTS, LANES), 1)
    eq_seen = jnp.zeros((N_EXPERTS, 1), F32)
    base = jnp.zeros((N_EXPERTS, 1), I32)
    starts = jnp.zeros((N_EXPERTS, LANES), I32)
    for j in range(nblk):
        cols = slice(TOK * j, TOK * j + TOK)
        eq_c = eq[:, cols]
        eq_cum = jnp.dot(eq_c.astype(BF16), tri, preferred_element_type=F32) + eq_seen
        eq_seen = eq_seen + jnp.sum(eq_c.astype(F32), axis=1, keepdims=True)
        sel = gt[:, cols] | (eq_c & (eq_cum <= need))
        cum = jnp.dot(sel.astype(BF16), tri, preferred_element_type=F32).astype(I32)
        slot_ref[0, :, cols] = jnp.where(sel, base + cum - 1, -1)
        starts = jnp.where(lane == j, base, starts)
        cnt = jnp.sum(sel.astype(I32), axis=1, keepdims=True)
        base = base + jnp.bitwise_and(cnt + (SUBLANES - 1), -SUBLANES)
    st_ref[0] = jnp.where(lane == nblk, base, starts)


def _router(logits_t, cap):
    b, e, s = logits_t.shape
    nblk = s // TOK
    return pl.pallas_call(
        functools.partial(_router_body, cap=cap, nblk=nblk),
        grid=(b,),
        in_specs=[pl.BlockSpec((1, e, s), lambda bi: (bi, 0, 0))],
        out_specs=[pl.BlockSpec((1, e, s), lambda bi: (bi, 0, 0)),
                   pl.BlockSpec((1, e, s), lambda bi: (bi, 0, 0)),
                   pl.BlockSpec((1, e, LANES), lambda bi: (bi, 0, 0))],
        out_shape=[jax.ShapeDtypeStruct((b, e, s), I32), jax.ShapeDtypeStruct((b, e, s), F32),
                   jax.ShapeDtypeStruct((b, e, LANES), I32)],
        compiler_params=_params("arbitrary"),
        name="router",
    )(logits_t)


def _onehot_t(slot_ref, first, win):
    rows = lax.broadcasted_iota(I32, (win, TOK), 0)
    return [rows == (slot_ref[0, e:e + 1, :] - first[e]) for e in range(N_EXPERTS)]


def _dispatch_body(st_ref, h_ref, slot_ref, aff_ref, xs_ref, stage, sems, count, *, win):
    bi, j = pl.program_id(0), pl.program_id(1)
    d = h_ref.shape[-1]

    @pl.when((bi == 0) & (j == 0))
    def _():
        count[0] = 0

    rows_alloc = xs_ref.shape[2]
    last_block = j == pl.num_programs(1) - 1
    st = [st_ref[bi, e, j] for e in range(N_EXPERTS)]
    ends = [jnp.where(last_block, rows_alloc, st_ref[bi, e, j + 1]) for e in range(N_EXPERTS)]
    width = ends[0] - st[0]
    for e in range(1, N_EXPERTS):
        width = jnp.maximum(width, ends[e] - st[e])
    rounds = jnp.maximum((width + win - 1) // win, 1)

    def copies(buf, first):
        return [pltpu.make_async_copy(stage.at[buf, pl.ds(e * win, win), :],
                                      xs_ref.at[bi, e, pl.ds(pl.multiple_of(first[e], SUBLANES), win), :],
                                      sems.at[e]) for e in range(N_EXPERTS)]

    def one_round(r, carry):
        k = count[0]
        buf = k % 2
        first = [jnp.minimum(st[e] + r * win, rows_alloc - win) for e in range(N_EXPERTS)]
        hot = _onehot_t(slot_ref, first, win)
        p = jnp.concatenate(hot, axis=0).astype(BF16)
        stage[buf, :, 0:d] = jnp.dot(p, h_ref[0], preferred_element_type=F32)
        gates = [jnp.sum(jnp.where(hot[e], aff_ref[0, e:e + 1, :], 0.0), axis=1, keepdims=True)
                 for e in range(N_EXPERTS)]
        stage[buf, :, d:d + LANES] = jnp.broadcast_to(jnp.concatenate(gates, axis=0), (N_EXPERTS * win, LANES))

        @pl.when(k == 0)
        def _():
            for c in copies(buf, first):
                c.start()

        @pl.when(k > 0)
        def _():
            for done, c in zip(copies(1 - buf, st), copies(buf, first)):
                done.wait()
                c.start()

        count[0] = k + 1
        return carry

    lax.fori_loop(0, rounds, one_round, 0)

    @pl.when((bi == pl.num_programs(0) - 1) & (j == pl.num_programs(1) - 1))
    def _():
        for c in copies(0, st):
            c.wait()


def _dispatch(starts, h, slot, aff, rows_alloc, win):
    b, s, d = h.shape
    nblk = s // TOK
    e = N_EXPERTS
    return pl.pallas_call(
        functools.partial(_dispatch_body, win=win),
        grid_spec=pltpu.PrefetchScalarGridSpec(
            num_scalar_prefetch=1, grid=(b, nblk),
            in_specs=[pl.BlockSpec((1, TOK, d), lambda bi, j, st: (bi, j, 0)),
                      pl.BlockSpec((1, e, TOK), lambda bi, j, st: (bi, 0, j)),
                      pl.BlockSpec((1, e, TOK), lambda bi, j, st: (bi, 0, j))],
            out_specs=pl.BlockSpec(memory_space=pl.ANY),
            scratch_shapes=[pltpu.VMEM((2, e * win, d + LANES), F32),
                            pltpu.SemaphoreType.DMA((e,)),
                            pltpu.SMEM((1,), I32)]),
        out_shape=jax.ShapeDtypeStruct((b, e, rows_alloc, d + LANES), F32),
        compiler_params=_params("arbitrary", "arbitrary"),
        name="dispatch",
    )(starts, h, slot, aff)


def _ffn_body(*refs, layer, sets, steps):
    n = len(sets)
    st_refs, xs_refs = refs[:n], refs[n:2 * n]
    wg_in, wu_in, wd_in = refs[2 * n: 2 * n + 3]
    y_refs = refs[2 * n + 3: 3 * n + 3]
    bufs, sems = refs[3 * n + 3: 3 * n + 6], refs[3 * n + 6]
    wg_ref, wu_ref, wd_ref = refs[3 * n + 7: 3 * n + 10]
    e, bi, i = pl.program_id(0), pl.program_id(1), pl.program_id(2)
    d = y_refs[0].shape[-1]

    def fetch(expert, slot, piece):
        k, p = divmod(piece, WEIGHT_PIECES)
        w, buf = (wg_in, wu_in, wd_in)[k], bufs[k]
        rows = w.shape[2] // WEIGHT_PIECES
        return pltpu.make_async_copy(w.at[layer, expert, pl.ds(p * rows, rows), :],
                                     buf.at[slot, pl.ds(p * rows, rows), :], sems.at[slot, piece])

    pieces = range(3 * WEIGHT_PIECES)
    step = bi * pl.num_programs(2) + i
    slot = e % 2

    @pl.when(step == 0)
    def _():
        @pl.when(e == 0)
        def _():
            for piece in pieces:
                fetch(0, 0, piece).start()

        for piece in pieces:
            fetch(e, slot, piece).wait()
        for out, buf in zip((wg_ref, wu_ref, wd_ref), bufs):
            out[...] = buf[slot].astype(BF16)

    for piece in pieces:
        @pl.when((step == min(piece, steps - 1)) & (e + 1 < pl.num_programs(0)))
        def _(piece=piece):
            fetch(e + 1, 1 - slot, piece).start()

    for (first, count, nblk), st_ref, xs_ref, y_ref in zip(sets, st_refs, xs_refs, y_refs):
        tm = xs_ref.shape[2]
        used = st_ref[bi, e, nblk]
        base = (i - first) * tm
        mine = (i >= first) & (i < first + count)

        def run(rows, xs_ref=xs_ref, y_ref=y_ref, tm=tm, used=used, base=base):
            valid = (base + lax.broadcasted_iota(I32, (rows, 1), 0)) < used
            xa = xs_ref[0, 0, 0:rows, :]
            x = jnp.where(valid, xa[:, 0:d], 0.0).astype(BF16)
            gate = jnp.where(valid, xa[:, d:d + 1], 0.0)
            a = jnp.dot(x, wg_ref[...], preferred_element_type=F32)
            u = jnp.dot(x, wu_ref[...], preferred_element_type=F32)
            mid = (a * jax.nn.sigmoid(a) * u).astype(BF16)
            y_ref[0, 0, 0:rows, :] = (jnp.dot(mid, wd_ref[...], preferred_element_type=F32) * gate).astype(BF16)
            if rows < tm:
                y_ref[0, 0, rows:tm, :] = jnp.zeros((tm - rows, d), BF16)

        piece = FFN_STEP if tm % FFN_STEP == 0 else tm
        left = used - base
        for rows in range(piece, tm + 1, piece):
            covers = (left > rows - piece) if rows == tm else (left > rows - piece) & (left <= rows)
            pl.when(mine & covers)(functools.partial(run, rows))

        @pl.when(mine & (left <= 0))
        def _(y_ref=y_ref):
            y_ref[0, 0] = jnp.zeros(y_ref.shape[2:], BF16)


def _ffn(routed, wg, wu, wd, layer):
    b, e = routed[0]["xs"].shape[:2]
    d, f = wg.shape[-2:]
    sets, first = [], 0
    for r in routed:
        count = r["rows"] // r["tm"]
        sets.append((first, count, r["nblk"]))
        first += count

    def tile_spec(r, first_tile, count, width):
        def index(ei, bi, i, *st):
            return bi, ei, jnp.clip(i - first_tile, 0, count - 1), 0
        return pl.BlockSpec((1, 1, r["tm"], width), index)

    in_specs = [tile_spec(r, s[0], s[1], d + LANES) for r, s in zip(routed, sets)]
    out_specs = [tile_spec(r, s[0], s[1], d) for r, s in zip(routed, sets)]
    out_shape = [jax.ShapeDtypeStruct((b, e, r["rows"], d), BF16) for r in routed]
    scratch = ([pltpu.VMEM((2,) + w.shape[2:], F32) for w in (wg, wu, wd)]
               + [pltpu.SemaphoreType.DMA((2, 3 * WEIGHT_PIECES))]
               + [pltpu.VMEM(w.shape[2:], BF16) for w in (wg, wu, wd)])
    return pl.pallas_call(
        functools.partial(_ffn_body, layer=layer, sets=tuple(sets), steps=b * first),
        grid_spec=pltpu.PrefetchScalarGridSpec(
            num_scalar_prefetch=len(routed), grid=(e, b, first),
            in_specs=in_specs + [pl.BlockSpec(memory_space=pl.ANY)] * 3,
            out_specs=out_specs, scratch_shapes=scratch),
        out_shape=out_shape,
        compiler_params=_params("arbitrary", "arbitrary", "arbitrary"),
        name="expert_ffn",
    )(*[r["starts"] for r in routed], *[r["xs"] for r in routed], wg, wu, wd)


def _combine_body(st_ref, slot_ref, x_ref, gt_ref, gf_ref, y_ref, o_ref, ybuf, sems, *, win, rows, final_norm):
    bi, j = pl.program_id(0), pl.program_id(1)
    nj = pl.num_programs(1)
    d = x_ref.shape[-1]
    step = bi * nj + j
    cur = step % 2
    st = [st_ref[bi, e, j] for e in range(N_EXPERTS)]
    width = st_ref[bi, 0, j + 1] - st[0]
    for e in range(1, N_EXPERTS):
        width = jnp.maximum(width, st_ref[bi, e, j + 1] - st[e])
    rounds = (width + win - 1) // win

    wrows = win + BF16_ROWS

    def window(b_, j_, r):
        lo = [st_ref[b_, e, j_] + r * win for e in range(N_EXPERTS)]
        first = [jnp.minimum(jnp.bitwise_and(lo[e], -BF16_ROWS), rows - wrows) for e in range(N_EXPERTS)]
        return lo, first

    def copies(b_, first, buf):
        return [pltpu.make_async_copy(y_ref.at[b_, e, pl.ds(pl.multiple_of(first[e], BF16_ROWS), wrows), :],
                                      ybuf.at[buf, pl.ds(e * wrows, wrows), :], sems.at[buf, e])
                for e in range(N_EXPERTS)]

    def scatter(lo, first, buf, acc):
        hot = _onehot_t(slot_ref, first, wrows)
        hot = [hot[e] & (slot_ref[0, e:e + 1, :] >= lo[e]) & (slot_ref[0, e:e + 1, :] < lo[e] + win)
               for e in range(N_EXPERTS)]
        p = jnp.concatenate(hot, axis=0).astype(BF16)
        return acc + lax.dot_general(p, ybuf[buf], TN_DIMS, preferred_element_type=F32)

    lo0, first0 = window(bi, j, 0)

    @pl.when(step == 0)
    def _():
        for c in copies(bi, first0, cur):
            c.start()

    @pl.when(step < pl.num_programs(0) * nj - 1)
    def _():
        wrap = j == nj - 1
        b_next, j_next = jnp.where(wrap, bi + 1, bi), jnp.where(wrap, 0, j + 1)
        for c in copies(b_next, window(b_next, j_next, 0)[1], 1 - cur):
            c.start()

    for c in copies(bi, first0, cur):
        c.wait()
    acc = scatter(lo0, first0, cur, jnp.zeros((TOK, d), F32))

    def later_round(r, acc):
        lo, first = window(bi, j, r)
        for c in copies(bi, first, 2):
            c.start()
        for c in copies(bi, first, 2):
            c.wait()
        return scatter(lo, first, 2, acc)

    acc = lax.fori_loop(1, rounds, later_round, acc)
    out = x_ref[0] + gt_ref[0] * acc
    if final_norm:
        ms = jnp.mean(out * out, axis=-1, keepdims=True)
        out = out * lax.rsqrt(ms + EPS) * gf_ref[...]
    o_ref[0] = out


def _combine(starts, slot, x, gate, g_final, y, win, final_norm):
    b, s, d = x.shape
    e = N_EXPERTS
    rows = y.shape[2]
    return pl.pallas_call(
        functools.partial(_combine_body, win=win, rows=rows, final_norm=final_norm),
        grid_spec=pltpu.PrefetchScalarGridSpec(
            num_scalar_prefetch=1, grid=(b, s // TOK),
            in_specs=[pl.BlockSpec((1, e, TOK), lambda bi, j, st: (bi, 0, j)),
                      pl.BlockSpec((1, TOK, d), lambda bi, j, st: (bi, j, 0)),
                      pl.BlockSpec((1, 1, d), lambda bi, j, st: (bi, 0, 0)),
                      pl.BlockSpec((1, d), lambda bi, j, st: (0, 0)),
                      pl.BlockSpec(memory_space=pl.ANY)],
            out_specs=pl.BlockSpec((1, TOK, d), lambda bi, j, st: (bi, j, 0)),
            scratch_shapes=[pltpu.VMEM((3, e * (win + BF16_ROWS), d), BF16), pltpu.SemaphoreType.DMA((3, e))]),
        out_shape=jax.ShapeDtypeStruct((b, s, d), F32),
        compiler_params=_params("arbitrary", "arbitrary"),
        name="combine",
    )(starts, slot, x, gate, g_final, y)


def _route(h, logits_t):
    n = h.shape[1]
    nblk = n // TOK
    cap = CAPACITY_FACTOR * n // N_EXPERTS
    win = min(SLOT_WINDOW, cap)
    rows = cap + SUBLANES * nblk
    rows = -(-rows // BF16_ROWS) * BF16_ROWS
    tm = min(FFN_TILE, rows)
    rows = -(-rows // tm) * tm
    slot, aff, starts = _router(logits_t, cap)
    xs = _dispatch(starts, h, slot, aff, rows + win, win)
    return dict(slot=slot, starts=starts, xs=xs, win=win, rows=rows, tm=tm, nblk=nblk)


def _rope_tables(n_tokens, rot_dim):
    rows = n_tokens // GRID_W
    row = jnp.repeat(jnp.arange(rows), GRID_W)
    col = jnp.tile(jnp.arange(GRID_W), rows)
    quarter = rot_dim // 4
    inv_freq = ROPE_THETA ** (-jnp.arange(quarter, dtype=F32) / quarter)
    ang = jnp.concatenate([inv_freq[:, None] * row[None, :], inv_freq[:, None] * col[None, :]], axis=0)
    return jnp.cos(ang), jnp.sin(ang)


def _identity_tables(n_tokens, rot_dim):
    return jnp.ones((rot_dim // 2, n_tokens), F32), jnp.zeros((rot_dim // 2, n_tokens), F32)


def kernel(x, c, ctx, c_ctx, w_mod, b_mod, g_attn, g_ffn, w_in, a_sink, b_q_norm, b_w_uq, b_kv_norm, b_w_ukv, c_lambda, c_subln, d_q_norm, d_k_norm, w_out, w_router, w_gate, w_up, w_down, g_final):
    b, s, d = x.shape
    depth = w_mod.shape[0]
    n_ctx = ctx.shape[1]
    rope64, rope32 = _rope_tables(s, HEAD_DIM), _rope_tables(s, C_QK)
    id64, id32 = _identity_tables(n_ctx, HEAD_DIM), _identity_tables(n_ctx, C_QK)

    cond = jnp.zeros((SUBLANES, d), F32).at[:b].set(c).at[b].set(c_ctx)
    mod = _adaln(cond, w_mod, b_mod)

    col = lambda v: v.reshape(-1, 1)
    xl, xc = x, ctx
    for l in range(depth):
        last = l == depth - 1
        lam_init = 0.8 - 0.6 * math.exp(-0.3 * l)
        m6 = mod[l].reshape(SUBLANES, 6, d)
        lat = [m6[:b, k][:, None, :] for k in range(6)]
        cx = [jnp.broadcast_to(m6[b, k][None, None, :], (b, 1, d)) for k in range(6)]
        w_in_t = w_in[l].T.astype(BF16)
        wuq_t = b_w_uq[l].T.astype(BF16)
        wukv_t = b_w_ukv[l].T.astype(BF16)
        w_out_b = w_out[l].astype(BF16)
        w_router_t = w_router[l].T
        w_router_hi = w_router_t.astype(BF16)
        w_router_t = jnp.concatenate([w_router_hi, (w_router_t - w_router_hi.astype(F32)).astype(BF16)], axis=0)
        g_a, g_f = g_attn[l][None, :], g_ffn[l][None, :]
        small = (col(b_q_norm[l]), wuq_t, col(b_kv_norm[l]), wukv_t, col(d_q_norm[l]), col(d_k_norm[l]))

        pl_ = _proj(xl, g_a, lat[0], lat[1], w_in_t, rope64, rope32, *small)
        pc_ = _proj(xc, g_a, cx[0], cx[1], w_in_t, id64, id32, *small)
        qa, ka, va, qb, kb, vb, qc, kc, vc, qd, kd, vd = pl_
        qa_c, ka_c, va_c, qb_c, kb_c, vb_c, qc_c, kc_c, vc_c, qd_c, kd_c, vd_c = pc_

        zero = lambda h: 0
        oa = _window_attn(qa, ka_c, va_c, ka, va, a_sink[l])
        ob = _dense_attn(qb, kb_c, vb_c, kb, vb, lambda h: h, lambda h: h)
        oc = _dense_attn(qc, kc_c, vc_c, kc, vc, lambda h: h // 4, lambda h: h // 2, out_dtype=F32)
        od = _dense_attn(qd, kd_c, vd_c, kd, vd, zero, lambda h: h // 2)
        outproj = functools.partial(_outproj, lam_p=c_lambda[l], subln=col(c_subln[l]), w_out=w_out_b,
                                    g_ffn=g_f, w_router_t=w_router_t, lam_init=lam_init)
        xl, hl, lg = outproj(oa, ob, oc, od, x=xl, gate=lat[2], shift=lat[3], scale=lat[4])
        routed = [_route(hl, lg)]
        if not last:
            oa = _dense_attn(qa_c, ka_c, va_c, None, None, zero, lambda h: h // 2, sink=a_sink[l])
            ob = _dense_attn(qb_c, kb_c, vb_c, None, None, lambda h: h, lambda h: h)
            oc = _dense_attn(qc_c, kc_c, vc_c, None, None, lambda h: h // 4, lambda h: h // 2, out_dtype=F32)
            od = _dense_attn(qd_c, kd_c, vd_c, None, None, zero, lambda h: h // 2)
            xc, hc, lgc = outproj(oa, ob, oc, od, x=xc, gate=cx[2], shift=cx[3], scale=cx[4])
            routed.append(_route(hc, lgc))
        ys = _ffn(routed, w_gate, w_up, w_down, l)
        combine = lambda r, y, x_, gate, norm: _combine(r["starts"], r["slot"], x_, gate, g_final[None, :], y,
                                                        r["win"], norm)
        xl = combine(routed[0], ys[0], xl, lat[5], last)
        if not last:
            xc = combine(routed[1], ys[1], xc, cx[5], False)
    return xl
```

```python
import functools
import math

import jax
import jax.numpy as jnp
from jax import lax
from jax.experimental import pallas as pl
from jax.experimental.pallas import tpu as pltpu

F32, BF16, I32 = jnp.float32, jnp.bfloat16, jnp.int32
HIGHEST = lax.Precision.HIGHEST

SUBLANES = 8
LANES = 128
BF16_ROWS = 16

GRID_W = 64
HEAD_DIM = 64
WINDOW = 128
EPS = 1e-6
ROPE_THETA = 10000.0
NEG_INF = -1e30
A_HEADS, A_KV = 4, 2
B_HEADS, B_Q_RANK, B_KV_RANK, B_NOPE, B_ROPE, B_V = 4, 192, 128, 64, 32, 64
C_HEADS, C_QK, C_V = 4, 32, 64
D_HEADS, D_KV = 4, 2
N_EXPERTS = 16
CAPACITY_FACTOR = 2

A_COLS = A_HEADS * HEAD_DIM + 2 * A_KV * HEAD_DIM
B_COLS = B_Q_RANK + B_KV_RANK + B_ROPE
C_COLS = 4 * C_HEADS * C_QK + C_HEADS * C_V
D_COLS = D_HEADS * HEAD_DIM + 2 * D_KV * HEAD_DIM
A_OFF, B_OFF, C_OFF, D_OFF = 0, A_COLS, A_COLS + B_COLS, A_COLS + B_COLS + C_COLS

BISECT_STEPS = 152
TOK = 256
QPAD = 128
VROWS = HEAD_DIM + 16
TQ = 2048
COL_BLOCK = 256
ATTN_UNROLL = 8
KEY_CHUNK = 256
WINDOW_CHUNKS = (0, -1, 1)
FFN_TILE = 640
FFN_STEP = 128
WEIGHT_PIECES = 2
SLOT_WINDOW = 48
LOG2E = math.log2(math.e)

NT_DIMS = (((1,), (1,)), ((), ()))
TN_DIMS = (((0,), (0,)), ((), ()))


def _params(*sem):
    return pltpu.CompilerParams(dimension_semantics=sem, vmem_limit_bytes=56 * 1024 * 1024)


def _adaln_body(c_ref, w_ref, b_ref, o_ref):
    c = c_ref[...]
    s = c * jax.nn.sigmoid(c)
    o_ref[0] = jnp.dot(s, w_ref[0], precision=HIGHEST, preferred_element_type=F32) + b_ref[0]


def _adaln(cond, w_mod, b_mod):
    depth, d, n = w_mod.shape
    tn = n // 4
    return pl.pallas_call(
        _adaln_body,
        grid=(depth, n // tn),
        in_specs=[pl.BlockSpec((SUBLANES, d), lambda l, j: (0, 0)),
                  pl.BlockSpec((1, d, tn), lambda l, j: (l, 0, j)),
                  pl.BlockSpec((1, 1, tn), lambda l, j: (l, 0, j))],
        out_specs=pl.BlockSpec((1, SUBLANES, tn), lambda l, j: (l, 0, j)),
        out_shape=jax.ShapeDtypeStruct((depth, SUBLANES, n), F32),
        compiler_params=_params("arbitrary", "arbitrary"),
        name="adaln",
    )(cond, w_mod, b_mod.reshape(depth, 1, n))


def _rope_t(xt, cos, sin):
    half = xt.shape[0] // 2
    x1, x2 = xt[:half], xt[half:]
    return jnp.concatenate([x1 * cos - x2 * sin, x1 * sin + x2 * cos], axis=0)


def _rms_t(xt, g):
    ms = jnp.mean(xt * xt, axis=0, keepdims=True)
    return xt * lax.rsqrt(ms + EPS) * g


def _modulate(x, g, shift, scale):
    ms = jnp.mean(x * x, axis=-1, keepdims=True)
    return x * lax.rsqrt(ms + EPS) * g * (1.0 + scale) + shift


def _proj_body(x_ref, g_ref, sh_ref, sc_ref, w_ref, c64_ref, s64_ref, c32_ref, s32_ref,
               bqn_ref, wuq_ref, bkvn_ref, wukv_ref, dqn_ref, dkn_ref,
               qa_ref, ka_ref, va_ref, qb_ref, kb_ref, vb_ref,
               qc_ref, kc_ref, vc_ref, qd_ref, kd_ref, vd_ref):
    tm = x_ref.shape[1]
    h = _modulate(x_ref[0], g_ref[...], sh_ref[0], sc_ref[0]).astype(BF16)
    pt = lax.dot_general(w_ref[...], h, NT_DIMS, preferred_element_type=F32)
    c64, s64, c32, s32 = c64_ref[...], s64_ref[...], c32_ref[...], s32_ref[...]
    z64 = jnp.zeros((64, tm), F32)
    z32 = jnp.zeros((32, tm), F32)
    scale64 = HEAD_DIM ** -0.5 * LOG2E
    scale_b = (B_NOPE + B_ROPE) ** -0.5 * LOG2E
    scale_c = C_QK ** -0.5 * LOG2E
    def put_values(v_ref, hh, vt):
        chunk = v_ref.shape[-1]
        for c in range(tm // chunk):
            v_ref[0, c, VROWS * hh: VROWS * hh + HEAD_DIM] = vt[:, chunk * c: chunk * c + chunk].astype(BF16)
            v_ref[0, c, VROWS * hh + HEAD_DIM: VROWS * hh + VROWS] = jnp.ones((VROWS - HEAD_DIM, chunk), BF16)

    def gqa(off, q_ref, k_ref, v_ref, qn, kn):
        for hh in range(4):
            q = pt[off + 64 * hh: off + 64 * hh + 64]
            if qn is not None:
                q = _rms_t(q, qn)
            q = _rope_t(q, c64, s64) * scale64
            q_ref[0, hh] = (jnp.concatenate([q, z64], axis=0) if hh // 2 == 0
                            else jnp.concatenate([z64, q], axis=0)).astype(BF16)
        ks = []
        for j in range(2):
            k = pt[off + 256 + 64 * j: off + 256 + 64 * j + 64]
            if kn is not None:
                k = _rms_t(k, kn)
            ks.append(_rope_t(k, c64, s64))
        k_ref[0] = jnp.concatenate(ks, axis=0).T.astype(BF16)
        for j in range(2):
            put_values(v_ref, j, pt[off + 384 + 64 * j: off + 384 + 64 * j + 64])

    gqa(A_OFF, qa_ref, ka_ref, va_ref, None, None)
    gqa(D_OFF, qd_ref, kd_ref, vd_ref, dqn_ref[...], dkn_ref[...])

    cq = _rms_t(pt[B_OFF: B_OFF + B_Q_RANK], bqn_ref[...]).astype(BF16)
    ckv = _rms_t(pt[B_OFF + B_Q_RANK: B_OFF + B_Q_RANK + B_KV_RANK], bkvn_ref[...]).astype(BF16)
    kr = _rope_t(pt[B_OFF + B_Q_RANK + B_KV_RANK: B_OFF + B_COLS], c32, s32)
    qt = jnp.dot(wuq_ref[...], cq, preferred_element_type=F32)
    kvt = jnp.dot(wukv_ref[...], ckv, preferred_element_type=F32)
    kparts = []
    dq = B_NOPE + B_ROPE
    for hh in range(B_HEADS):
        qn_ = qt[dq * hh: dq * hh + B_NOPE]
        qr = _rope_t(qt[dq * hh + B_NOPE: dq * hh + dq], c32, s32)
        qb_ref[0, hh] = (jnp.concatenate([qn_, qr, z32], axis=0) * scale_b).astype(BF16)
        kparts.append(jnp.concatenate([kvt[128 * hh: 128 * hh + B_NOPE], kr, z32], axis=0))
        put_values(vb_ref, hh, kvt[128 * hh + B_NOPE: 128 * hh + 128])
    kb_ref[0] = jnp.concatenate(kparts, axis=0).T.astype(BF16)

    for j in range(2 * C_HEADS):
        q = _rope_t(pt[C_OFF + 32 * j: C_OFF + 32 * j + 32], c32, s32) * scale_c
        pieces = [z32, z32, z32, z32]
        pieces[j % 4] = q
        qc_ref[0, j] = jnp.concatenate(pieces, axis=0).astype(BF16)
    kc = [_rope_t(pt[C_OFF + 256 + 32 * j: C_OFF + 256 + 32 * j + 32], c32, s32) for j in range(2 * C_HEADS)]
    kc_ref[0] = jnp.concatenate(kc, axis=0).T.astype(BF16)
    for hh in range(C_HEADS):
        put_values(vc_ref, hh, pt[C_OFF + 512 + 64 * hh: C_OFF + 512 + 64 * hh + 64])


def _proj(x, g, shift, scale, w_in_t, rope64, rope32, bqn, wuq_t, bkvn, wukv_t, dqn, dkn):
    b, s, d = x.shape
    tm = min(2 * TOK, s)
    n = s // tm
    full = lambda a: pl.BlockSpec(a.shape, lambda bi, i: (0,) * a.ndim)
    q_spec = lambda nh: pl.BlockSpec((1, nh, QPAD, tm), lambda bi, i: (bi, 0, 0, i))
    k_spec = lambda w: pl.BlockSpec((1, tm, w), lambda bi, i: (bi, i, 0))
    kch = min(KEY_CHUNK, tm)
    v_spec = lambda r, c=kch: pl.BlockSpec((1, tm // c, r, c), lambda bi, i: (bi, i, 0, 0))
    q_shape = lambda nh: jax.ShapeDtypeStruct((b, nh, QPAD, s), BF16)
    k_shape = lambda w: jax.ShapeDtypeStruct((b, s, w), BF16)
    v_shape = lambda r, c=kch: jax.ShapeDtypeStruct((b, s // c, r, c), BF16)
    tab = lambda t: pl.BlockSpec((t.shape[0], tm), lambda bi, i: (0, i))
    c64, s64 = rope64
    c32, s32 = rope32
    return pl.pallas_call(
        _proj_body,
        grid=(b, n),
        in_specs=[pl.BlockSpec((1, tm, d), lambda bi, i: (bi, i, 0)), full(g),
                  pl.BlockSpec((1, 1, d), lambda bi, i: (bi, 0, 0)),
                  pl.BlockSpec((1, 1, d), lambda bi, i: (bi, 0, 0)),
                  full(w_in_t), tab(c64), tab(s64), tab(c32), tab(s32),
                  full(bqn), full(wuq_t), full(bkvn), full(wukv_t), full(dqn), full(dkn)],
        out_specs=[q_spec(4), k_spec(128), v_spec(2 * VROWS, TOK),
                   q_spec(4), k_spec(512), v_spec(4 * VROWS),
                   q_spec(8), k_spec(256), v_spec(4 * VROWS),
                   q_spec(4), k_spec(128), v_spec(2 * VROWS)],
        out_shape=[q_shape(4), k_shape(128), v_shape(2 * VROWS, TOK),
                   q_shape(4), k_shape(512), v_shape(4 * VROWS),
                   q_shape(8), k_shape(256), v_shape(4 * VROWS),
                   q_shape(4), k_shape(128), v_shape(2 * VROWS)],
        compiler_params=_params("arbitrary", "arbitrary"),
        name="proj",
    )(x, g, shift, scale, w_in_t, c64, s64, c32, s32, bqn, wuq_t, bkvn, wukv_t, dqn, dkn)


def _dense_body(*refs, has_lat, has_sink, n_lat):
    refs = list(refs)
    sink_ref = refs.pop(0) if has_sink else None
    q_ref, kc_ref, vc_ref = refs[:3]
    kl_ref, vl_ref = (refs[3], refs[4]) if has_lat else (None, None)
    o_ref, s_even, s_odd, acc_ref = refs[-4:]
    tq = q_ref.shape[-1]
    qt = q_ref[0, 0]

    def produce(kblk, s_ref):
        s = jnp.dot(kblk, qt, preferred_element_type=F32)
        s_ref[0:kblk.shape[0], :] = s
        return jnp.max(s, axis=0, keepdims=True)

    def consume(s_ref, mx, vblk, m):
        m_new = mx if m is None else jnp.maximum(m, mx)
        p = jnp.exp2(s_ref[0:vblk.shape[1], :] - m_new).astype(BF16)
        pv = jnp.dot(vblk, p, preferred_element_type=F32)
        acc_ref[...] = pv if m is None else jnp.exp2(m - m_new) * acc_ref[...] + pv
        return m_new

    kch = s_even.shape[0]

    def lat_keys(i):
        return kl_ref[0, pl.ds(pl.multiple_of(i * kch, kch), kch), :]

    mx_c = produce(kc_ref[0], s_odd)
    if not has_lat:
        m = consume(s_odd, mx_c, vc_ref[0, 0], None)
    else:
        ncol = tq // COL_BLOCK

        def overlap(k_next, s_next, s_cur, mx_cur, v_cur, m):
            m_out, mx_out = [], []
            for c in range(ncol):
                cols = slice(COL_BLOCK * c, COL_BLOCK * c + COL_BLOCK)
                s = jnp.dot(k_next, qt[:, cols], preferred_element_type=F32)
                s_next[:, cols] = s
                mx_out.append(jnp.max(s, axis=0, keepdims=True))
                m_new = mx_cur[:, cols] if m is None else jnp.maximum(m[:, cols], mx_cur[:, cols])
                p = jnp.exp2(s_cur[:, cols] - m_new).astype(BF16)
                pv = jnp.dot(v_cur, p, preferred_element_type=F32)
                acc_ref[:, cols] = pv if m is None else jnp.exp2(m[:, cols] - m_new) * acc_ref[:, cols] + pv
                m_out.append(m_new)
            return jnp.concatenate(m_out, axis=1), jnp.concatenate(mx_out, axis=1)

        m, mx_e = overlap(lat_keys(0), s_even, s_odd, mx_c, vc_ref[0, 0], None)

        def body(j, carry):
            m, mx_e = carry
            m, mx_o = overlap(lat_keys(2 * j + 1), s_odd, s_even, mx_e, vl_ref[0, 2 * j], m)
            m, mx_e = overlap(lat_keys(jnp.minimum(2 * j + 2, n_lat - 1)), s_even, s_odd, mx_o, vl_ref[0, 2 * j + 1], m)
            return m, mx_e
        m, _ = lax.fori_loop(0, n_lat // 2, body, (m, mx_e), unroll=ATTN_UNROLL)
    acc = acc_ref[...]
    num, den = acc[:HEAD_DIM], acc[HEAD_DIM:HEAD_DIM + 1]
    if has_sink:
        sk = sink_ref[pl.program_id(1)] * LOG2E
        m2 = jnp.maximum(m, sk)
        a = jnp.exp2(m - m2)
        den = den * a + jnp.exp2(sk - m2)
        num = num * a
    o_ref[0, 0] = (num * (1.0 / den)).astype(o_ref.dtype)


def _dense_attn(q, k_ctx, v_ctx, k_lat, v_lat, kgroup, vhead, sink=None, out_dtype=BF16):
    b, nh, _, s = q.shape
    tq = min(TQ, s)
    has_lat = k_lat is not None
    has_sink = sink is not None
    sc = k_ctx.shape[1]
    in_specs = [pl.BlockSpec((1, 1, QPAD, tq), lambda bi, h, i: (bi, h, 0, i)),
                pl.BlockSpec((1, sc, LANES), lambda bi, h, i: (bi, 0, kgroup(h))),
                pl.BlockSpec((1, 1, VROWS, sc), lambda bi, h, i: (bi, 0, vhead(h), 0))]
    args = [q, k_ctx, v_ctx]
    n_lat, kch = 0, sc
    if has_lat:
        sl = k_lat.shape[1]
        kch = v_lat.shape[-1]
        n_lat = sl // kch
        assert n_lat % 2 == 0 and kch == sc and tq % COL_BLOCK == 0
        in_specs += [pl.BlockSpec((1, sl, LANES), lambda bi, h, i: (bi, 0, kgroup(h))),
                     pl.BlockSpec((1, n_lat, VROWS, kch), lambda bi, h, i: (bi, 0, vhead(h), 0))]
        args += [k_lat, v_lat]
    if has_sink:
        in_specs = [pl.BlockSpec(memory_space=pltpu.SMEM)] + in_specs
        args = [sink] + args
    return pl.pallas_call(
        functools.partial(_dense_body, has_lat=has_lat, has_sink=has_sink, n_lat=n_lat),
        grid=(b, nh, s // tq),
        in_specs=in_specs,
        out_specs=pl.BlockSpec((1, 1, HEAD_DIM, tq), lambda bi, h, i: (bi, h, 0, i)),
        out_shape=jax.ShapeDtypeStruct((b, nh, HEAD_DIM, s), out_dtype),
        scratch_shapes=[pltpu.VMEM((kch, tq), F32), pltpu.VMEM((kch, tq), F32), pltpu.VMEM((VROWS, tq), F32)],
        compiler_params=_params("arbitrary", "arbitrary", "arbitrary"),
        name="dense_attn",
    )(*args)


def _window_body(sink_ref, q_ref, kc_ref, vc_ref, *rest, n_chunks):
    nw = len(WINDOW_CHUNKS)
    k_refs, v_refs = rest[:nw], rest[nw:2 * nw]
    o_ref, s_even, s_odd, acc_ref = rest[2 * nw:]
    i = pl.program_id(1)
    nh, tq = q_ref.shape[1], q_ref.shape[-1]
    width = nh * tq
    qt = jnp.concatenate([q_ref[0, hh] for hh in range(nh)], axis=1)
    dist = lax.broadcasted_iota(I32, (TOK, tq), 0) - lax.broadcasted_iota(I32, (TOK, tq), 1)

    def in_window(rel):
        chunk = (tq // TOK) * i + rel
        return (jnp.abs(dist + (rel * TOK)) <= WINDOW) & (chunk >= 0) & (chunk < n_chunks)

    def produce(k_ref, ok, s_ref):
        s = jnp.dot(k_ref[0], qt, preferred_element_type=F32)
        if ok is not None:
            s = jnp.concatenate([jnp.where(ok, s[:, tq * hh: tq * hh + tq], NEG_INF) for hh in range(nh)], axis=1)
        s_ref[...] = s
        return jnp.max(s, axis=0, keepdims=True)

    def consume(s_ref, mx, v_ref, m):
        m_new = jnp.maximum(m, mx)
        p = jnp.exp2(s_ref[...] - m_new).astype(BF16)
        pv = [jnp.dot(v_ref[0, 0, VROWS * g: VROWS * g + VROWS, :], p[:, 2 * tq * g: 2 * tq * g + 2 * tq],
                      preferred_element_type=F32) for g in range(A_KV)]
        acc_ref[...] = jnp.exp2(m - m_new) * acc_ref[...] + jnp.concatenate(pv, axis=1)
        return m_new

    acc_ref[...] = jnp.zeros(acc_ref.shape, F32)
    bufs = (s_even, s_odd)
    mx = produce(kc_ref, None, s_even)
    m = jnp.full((1, width), NEG_INF, F32)
    prev_v = vc_ref
    for c, rel in enumerate(WINDOW_CHUNKS):
        mx_next = produce(k_refs[c], in_window(rel), bufs[(c + 1) % 2])
        m = consume(bufs[c % 2], mx, prev_v, m)
        mx, prev_v = mx_next, v_refs[c]
    m = consume(bufs[nw % 2], mx, prev_v, m)

    sk = jnp.concatenate([jnp.full((1, tq), sink_ref[hh] * LOG2E, F32) for hh in range(nh)], axis=1)
    m2 = jnp.maximum(m, sk)
    a = jnp.exp2(m - m2)
    acc = acc_ref[...]
    out = acc[:HEAD_DIM] * a / (acc[HEAD_DIM:HEAD_DIM + 1] * a + jnp.exp2(sk - m2))
    for hh in range(nh):
        o_ref[0, hh] = out[:, tq * hh: tq * hh + tq].astype(o_ref.dtype)


def _window_attn(q, k_ctx, v_ctx, k_lat, v_lat, sink):
    b, nh, _, s = q.shape
    tq = TOK
    assert s % tq == 0 and WINDOW <= TOK
    n_chunks = s // TOK
    sc = k_ctx.shape[1]
    nw = len(WINDOW_CHUNKS)
    chunk = lambda rel: (lambda i: jnp.clip((tq // TOK) * i + rel, 0, n_chunks - 1))
    kspec = lambda f: pl.BlockSpec((1, TOK, LANES), lambda bi, i: (bi, f(i), 0))
    vspec = lambda f: pl.BlockSpec((1, 1, 2 * VROWS, TOK), lambda bi, i: (bi, f(i), 0, 0))
    return pl.pallas_call(
        functools.partial(_window_body, n_chunks=n_chunks),
        grid=(b, s // tq),
        in_specs=[pl.BlockSpec(memory_space=pltpu.SMEM),
                  pl.BlockSpec((1, nh, QPAD, tq), lambda bi, i: (bi, 0, 0, i)),
                  pl.BlockSpec((1, sc, LANES), lambda bi, i: (bi, 0, 0)),
                  pl.BlockSpec((1, 1, 2 * VROWS, sc), lambda bi, i: (bi, 0, 0, 0))]
                 + [kspec(chunk(rel)) for rel in WINDOW_CHUNKS] + [vspec(chunk(rel)) for rel in WINDOW_CHUNKS],
        out_specs=pl.BlockSpec((1, nh, HEAD_DIM, tq), lambda bi, i: (bi, 0, 0, i)),
        out_shape=jax.ShapeDtypeStruct((b, nh, HEAD_DIM, s), BF16),
        scratch_shapes=[pltpu.VMEM((TOK, nh * tq), F32), pltpu.VMEM((TOK, nh * tq), F32),
                        pltpu.VMEM((VROWS, nh * tq), F32)],
        compiler_params=_params("arbitrary", "arbitrary"),
        name="window_attn",
    )(sink, q, k_ctx, v_ctx, *([k_lat] * nw), *([v_lat] * nw))


def _outproj_body(oa_ref, ob_ref, oc_ref, od_ref, lam_ref, subln_ref, w_ref, x_ref, gt_ref,
                  g_ref, sh_ref, sc_ref, wr_ref, xo_ref, h_ref, lg_ref, *, lam_init):
    lp = lam_ref[...]
    lam = (jnp.exp(jnp.sum(lp[0:1] * lp[1:2], axis=1, keepdims=True))
           - jnp.exp(jnp.sum(lp[2:3] * lp[3:4], axis=1, keepdims=True)) + lam_init)
    parts = [oa_ref[0, hh] for hh in range(4)] + [ob_ref[0, hh] for hh in range(4)]
    for hh in range(C_HEADS):
        o = oc_ref[0, 2 * hh] - lam * oc_ref[0, 2 * hh + 1]
        parts.append((_rms_t(o, subln_ref[...]) * (1.0 - lam_init)).astype(BF16))
    parts += [od_ref[0, hh] for hh in range(4)]
    ot = jnp.concatenate(parts, axis=0)
    out = lax.dot_general(ot, w_ref[...], TN_DIMS, preferred_element_type=F32)
    xn = x_ref[0] + gt_ref[0] * out
    xo_ref[0] = xn
    h = _modulate(xn, g_ref[...], sh_ref[0], sc_ref[0])
    h_hi = h.astype(BF16)
    h_ref[0] = h_hi
    h_lo = (h - h_hi.astype(F32)).astype(BF16)
    both = lax.dot_general(wr_ref[...], h_hi, NT_DIMS, preferred_element_type=F32)
    low = lax.dot_general(wr_ref[0:N_EXPERTS], h_lo, NT_DIMS, preferred_element_type=F32)
    lg_ref[0] = both[:N_EXPERTS] + both[N_EXPERTS:] + low


def _outproj(oa, ob, oc, od, lam_p, subln, w_out, x, gate, g_ffn, shift, scale, w_router_t, lam_init):
    b, s, d = x.shape
    tm = min(2 * TOK, s)
    ospec = lambda nh: pl.BlockSpec((1, nh, HEAD_DIM, tm), lambda bi, i: (bi, 0, 0, i))
    full = lambda a: pl.BlockSpec(a.shape, lambda bi, i: (0,) * a.ndim)
    row = pl.BlockSpec((1, 1, d), lambda bi, i: (bi, 0, 0))
    return pl.pallas_call(
        functools.partial(_outproj_body, lam_init=lam_init),
        grid=(b, s // tm),
        in_specs=[ospec(4), ospec(4), ospec(8), ospec(4), full(lam_p), full(subln), full(w_out),
                  pl.BlockSpec((1, tm, d), lambda bi, i: (bi, i, 0)), row, full(g_ffn), row, row,
                  full(w_router_t)],
        out_specs=[pl.BlockSpec((1, tm, d), lambda bi, i: (bi, i, 0)),
                   pl.BlockSpec((1, tm, d), lambda bi, i: (bi, i, 0)),
                   pl.BlockSpec((1, N_EXPERTS, tm), lambda bi, i: (bi, 0, i))],
        out_shape=[jax.ShapeDtypeStruct((b, s, d), F32), jax.ShapeDtypeStruct((b, s, d), BF16),
                   jax.ShapeDtypeStruct((b, N_EXPERTS, s), F32)],
        compiler_params=_params("arbitrary", "arbitrary"),
        name="outproj",
    )(oa, ob, oc, od, lam_p, subln, w_out, x, gate, g_ffn, shift, scale, w_router_t)


def _router_body(lg_ref, slot_ref, aff_ref, st_ref, *, cap, nblk):
    lg = lg_ref[0]
    ex = jnp.exp(lg - jnp.max(lg, axis=0, keepdims=True))
    aff = ex / jnp.sum(ex, axis=0, keepdims=True)
    aff_ref[0] = aff

    def search(_, bounds):
        lo, hi = bounds
        mid = (lo + hi) * 0.5
        enough = jnp.sum((aff >= mid).astype(I32), axis=1, keepdims=True) >= cap
        return jnp.where(enough, mid, lo), jnp.where(enough, hi, mid)

    lo, hi = lax.fori_loop(0, BISECT_STEPS, search,
                           (jnp.zeros((N_EXPERTS, 1), F32), jnp.full((N_EXPERTS, 1), 2.0, F32)))
    gt = aff >= hi
    eq = (aff >= lo) & jnp.logical_not(gt)
    need = (cap - jnp.sum(gt.astype(I32), axis=1, keepdims=True)).astype(F32)
    tri = (lax.broadcasted_iota(I32, (TOK, TOK), 0) <= lax.broadcasted_iota(I32, (TOK, TOK), 1)).astype(BF16)
    lane = lax.broadcasted_iota(I32, (N_EXPERTS, LANES), 1)
    eq_seen = jnp.zeros((N_EXPERTS, 1), F32)
    base = jnp.zeros((N_EXPERTS, 1), I32)
    starts = jnp.zeros((N_EXPERTS, LANES), I32)
    for j in range(nblk):
        cols = slice(TOK * j, TOK * j + TOK)
        eq_c = eq[:, cols]
        eq_cum = jnp.dot(eq_c.astype(BF16), tri, preferred_element_type=F32) + eq_seen
        eq_seen = eq_seen + jnp.sum(eq_c.astype(F32), axis=1, keepdims=True)
        sel = gt[:, cols] | (eq_c & (eq_cum <= need))
        cum = jnp.dot(sel.astype(BF16), tri, preferred_element_type=F32).astype(I32)
        slot_ref[0, :, cols] = jnp.where(sel, base + cum - 1, -1)
        starts = jnp.where(lane == j, base, starts)
        cnt = jnp.sum(sel.astype(I32), axis=1, keepdims=True)
        base = base + jnp.bitwise_and(cnt + (SUBLANES - 1), -SUBLANES)
    st_ref[0] = jnp.where(lane == nblk, base, starts)


def _router(logits_t, cap):
    b, e, s = logits_t.shape
    nblk = s // TOK
    return pl.pallas_call(
        functools.partial(_router_body, cap=cap, nblk=nblk),
        grid=(b,),
        in_specs=[pl.BlockSpec((1, e, s), lambda bi: (bi, 0, 0))],
        out_specs=[pl.BlockSpec((1, e, s), lambda bi: (bi, 0, 0)),
                   pl.BlockSpec((1, e, s), lambda bi: (bi, 0, 0)),
                   pl.BlockSpec((1, e, LANES), lambda bi: (bi, 0, 0))],
        out_shape=[jax.ShapeDtypeStruct((b, e, s), I32), jax.ShapeDtypeStruct((b, e, s), F32),
                   jax.ShapeDtypeStruct((b, e, LANES), I32)],
        compiler_params=_params("arbitrary"),
        name="router",
    )(logits_t)


def _onehot_t(slot_ref, first, win):
    rows = lax.broadcasted_iota(I32, (win, TOK), 0)
    return [rows == (slot_ref[0, e:e + 1, :] - first[e]) for e in range(N_EXPERTS)]


def _dispatch_body(st_ref, h_ref, slot_ref, aff_ref, xs_ref, stage, sems, count, *, win):
    bi, j = pl.program_id(0), pl.program_id(1)
    d = h_ref.shape[-1]

    @pl.when((bi == 0) & (j == 0))
    def _():
        count[0] = 0

    rows_alloc = xs_ref.shape[2]
    last_block = j == pl.num_programs(1) - 1
    st = [st_ref[bi, e, j] for e in range(N_EXPERTS)]
    ends = [jnp.where(last_block, rows_alloc, st_ref[bi, e, j + 1]) for e in range(N_EXPERTS)]
    width = ends[0] - st[0]
    for e in range(1, N_EXPERTS):
        width = jnp.maximum(width, ends[e] - st[e])
    rounds = jnp.maximum((width + win - 1) // win, 1)

    def copies(buf, first):
        return [pltpu.make_async_copy(stage.at[buf, pl.ds(e * win, win), :],
                                      xs_ref.at[bi, e, pl.ds(pl.multiple_of(first[e], SUBLANES), win), :],
                                      sems.at[e]) for e in range(N_EXPERTS)]

    def one_round(r, carry):
        k = count[0]
        buf = k % 2
        first = [jnp.minimum(st[e] + r * win, rows_alloc - win) for e in range(N_EXPERTS)]
        hot = _onehot_t(slot_ref, first, win)
        p = jnp.concatenate(hot, axis=0).astype(BF16)
        stage[buf, :, 0:d] = jnp.dot(p, h_ref[0], preferred_element_type=F32)
        gates = [jnp.sum(jnp.where(hot[e], aff_ref[0, e:e + 1, :], 0.0), axis=1, keepdims=True)
                 for e in range(N_EXPERTS)]
        stage[buf, :, d:d + LANES] = jnp.broadcast_to(jnp.concatenate(gates, axis=0), (N_EXPERTS * win, LANES))

        @pl.when(k == 0)
        def _():
            for c in copies(buf, first):
                c.start()

        @pl.when(k > 0)
        def _():
            for done, c in zip(copies(1 - buf, st), copies(buf, first)):
                done.wait()
                c.start()

        count[0] = k + 1
        return carry

    lax.fori_loop(0, rounds, one_round, 0)

    @pl.when((bi == pl.num_programs(0) - 1) & (j == pl.num_programs(1) - 1))
    def _():
        for c in copies(0, st):
            c.wait()


def _dispatch(starts, h, slot, aff, rows_alloc, win):
    b, s, d = h.shape
    nblk = s // TOK
    e = N_EXPERTS
    return pl.pallas_call(
        functools.partial(_dispatch_body, win=win),
        grid_spec=pltpu.PrefetchScalarGridSpec(
            num_scalar_prefetch=1, grid=(b, nblk),
            in_specs=[pl.BlockSpec((1, TOK, d), lambda bi, j, st: (bi, j, 0)),
                      pl.BlockSpec((1, e, TOK), lambda bi, j, st: (bi, 0, j)),
                      pl.BlockSpec((1, e, TOK), lambda bi, j, st: (bi, 0, j))],
            out_specs=pl.BlockSpec(memory_space=pl.ANY),
            scratch_shapes=[pltpu.VMEM((2, e * win, d + LANES), F32),
                            pltpu.SemaphoreType.DMA((e,)),
                            pltpu.SMEM((1,), I32)]),
        out_shape=jax.ShapeDtypeStruct((b, e, rows_alloc, d + LANES), F32),
        compiler_params=_params("arbitrary", "arbitrary"),
        name="dispatch",
    )(starts, h, slot, aff)


def _ffn_body(*refs, layer, sets, steps):
    n = len(sets)
    st_refs, xs_refs = refs[:n], refs[n:2 * n]
    wg_in, wu_in, wd_in = refs[2 * n: 2 * n + 3]
    y_refs = refs[2 * n + 3: 3 * n + 3]
    bufs, sems = refs[3 * n + 3: 3 * n + 6], refs[3 * n + 6]
    wg_ref, wu_ref, wd_ref = refs[3 * n + 7: 3 * n + 10]
    e, bi, i = pl.program_id(0), pl.program_id(1), pl.program_id(2)
    d = y_refs[0].shape[-1]

    def fetch(expert, slot, piece):
        k, p = divmod(piece, WEIGHT_PIECES)
        w, buf = (wg_in, wu_in, wd_in)[k], bufs[k]
        rows = w.shape[2] // WEIGHT_PIECES
        return pltpu.make_async_copy(w.at[layer, expert, pl.ds(p * rows, rows), :],
                                     buf.at[slot, pl.ds(p * rows, rows), :], sems.at[slot, piece])

    pieces = range(3 * WEIGHT_PIECES)
    step = bi * pl.num_programs(2) + i
    slot = e % 2

    @pl.when(step == 0)
    def _():
        @pl.when(e == 0)
        def _():
            for piece in pieces:
                fetch(0, 0, piece).start()

        for piece in pieces:
            fetch(e, slot, piece).wait()
        for out, buf in zip((wg_ref, wu_ref, wd_ref), bufs):
            out[...] = buf[slot].astype(BF16)

    for piece in pieces:
        @pl.when((step == min(piece, steps - 1)) & (e + 1 < pl.num_programs(0)))
        def _(piece=piece):
            fetch(e + 1, 1 - slot, piece).start()

    for (first, count, nblk), st_ref, xs_ref, y_ref in zip(sets, st_refs, xs_refs, y_refs):
        tm = xs_ref.shape[2]
        used = st_ref[bi, e, nblk]
        base = (i - first) * tm
        mine = (i >= first) & (i < first + count)

        def run(rows, xs_ref=xs_ref, y_ref=y_ref, tm=tm, used=used, base=base):
            valid = (base + lax.broadcasted_iota(I32, (rows, 1), 0)) < used
            xa = xs_ref[0, 0, 0:rows, :]
            x = jnp.where(valid, xa[:, 0:d], 0.0).astype(BF16)
            gate = jnp.where(valid, xa[:, d:d + 1], 0.0)
            a = jnp.dot(x, wg_ref[...], preferred_element_type=F32)
            u = jnp.dot(x, wu_ref[...], preferred_element_type=F32)
            mid = (a * jax.nn.sigmoid(a) * u).astype(BF16)
            y_ref[0, 0, 0:rows, :] = (jnp.dot(mid, wd_ref[...], preferred_element_type=F32) * gate).astype(BF16)
            if rows < tm:
                y_ref[0, 0, rows:tm, :] = jnp.zeros((tm - rows, d), BF16)

        piece = FFN_STEP if tm % FFN_STEP == 0 else tm
        left = used - base
        for rows in range(piece, tm + 1, piece):
            covers = (left > rows - piece) if rows == tm else (left > rows - piece) & (left <= rows)
            pl.when(mine & covers)(functools.partial(run, rows))

        @pl.when(mine & (left <= 0))
        def _(y_ref=y_ref):
            y_ref[0, 0] = jnp.zeros(y_ref.shape[2:], BF16)


def _ffn(routed, wg, wu, wd, layer):
    b, e = routed[0]["xs"].shape[:2]
    d, f = wg.shape[-2:]
    sets, first = [], 0
    for r in routed:
        count = r["rows"] // r["tm"]
        sets.append((first, count, r["nblk"]))
        first += count

    def tile_spec(r, first_tile, count, width):
        def index(ei, bi, i, *st):
            return bi, ei, jnp.clip(i - first_tile, 0, count - 1), 0
        return pl.BlockSpec((1, 1, r["tm"], width), index)

    in_specs = [tile_spec(r, s[0], s[1], d + LANES) for r, s in zip(routed, sets)]
    out_specs = [tile_spec(r, s[0], s[1], d) for r, s in zip(routed, sets)]
    out_shape = [jax.ShapeDtypeStruct((b, e, r["rows"], d), BF16) for r in routed]
    scratch = ([pltpu.VMEM((2,) + w.shape[2:], F32) for w in (wg, wu, wd)]
               + [pltpu.SemaphoreType.DMA((2, 3 * WEIGHT_PIECES))]
               + [pltpu.VMEM(w.shape[2:], BF16) for w in (wg, wu, wd)])
    return pl.pallas_call(
        functools.partial(_ffn_body, layer=layer, sets=tuple(sets), steps=b * first),
        grid_spec=pltpu.PrefetchScalarGridSpec(
            num_scalar_prefetch=len(routed), grid=(e, b, first),
            in_specs=in_specs + [pl.BlockSpec(memory_space=pl.ANY)] * 3,
            out_specs=out_specs, scratch_shapes=scratch),
        out_shape=out_shape,
        compiler_params=_params("arbitrary", "arbitrary", "arbitrary"),
        name="expert_ffn",
    )(*[r["starts"] for r in routed], *[r["xs"] for r in routed], wg, wu, wd)


def _combine_body(st_ref, slot_ref, x_ref, gt_ref, gf_ref, y_ref, o_ref, ybuf, sems, *, win, rows, final_norm):
    bi, j = pl.program_id(0), pl.program_id(1)
    nj = pl.num_programs(1)
    d = x_ref.shape[-1]
    step = bi * nj + j
    cur = step % 2
    st = [st_ref[bi, e, j] for e in range(N_EXPERTS)]
    width = st_ref[bi, 0, j + 1] - st[0]
    for e in range(1, N_EXPERTS):
        width = jnp.maximum(width, st_ref[bi, e, j + 1] - st[e])
    rounds = (width + win - 1) // win

    wrows = win + BF16_ROWS

    def window(b_, j_, r):
        lo = [st_ref[b_, e, j_] + r * win for e in range(N_EXPERTS)]
        first = [jnp.minimum(jnp.bitwise_and(lo[e], -BF16_ROWS), rows - wrows) for e in range(N_EXPERTS)]
        return lo, first

    def copies(b_, first, buf):
        return [pltpu.make_async_copy(y_ref.at[b_, e, pl.ds(pl.multiple_of(first[e], BF16_ROWS), wrows), :],
                                      ybuf.at[buf, pl.ds(e * wrows, wrows), :], sems.at[buf, e])
                for e in range(N_EXPERTS)]

    def scatter(lo, first, buf, acc):
        hot = _onehot_t(slot_ref, first, wrows)
        hot = [hot[e] & (slot_ref[0, e:e + 1, :] >= lo[e]) & (slot_ref[0, e:e + 1, :] < lo[e] + win)
               for e in range(N_EXPERTS)]
        p = jnp.concatenate(hot, axis=0).astype(BF16)
        return acc + lax.dot_general(p, ybuf[buf], TN_DIMS, preferred_element_type=F32)

    lo0, first0 = window(bi, j, 0)

    @pl.when(step == 0)
    def _():
        for c in copies(bi, first0, cur):
            c.start()

    @pl.when(step < pl.num_programs(0) * nj - 1)
    def _():
        wrap = j == nj - 1
        b_next, j_next = jnp.where(wrap, bi + 1, bi), jnp.where(wrap, 0, j + 1)
        for c in copies(b_next, window(b_next, j_next, 0)[1], 1 - cur):
            c.start()

    for c in copies(bi, first0, cur):
        c.wait()
    acc = scatter(lo0, first0, cur, jnp.zeros((TOK, d), F32))

    def later_round(r, acc):
        lo, first = window(bi, j, r)
        for c in copies(bi, first, 2):
            c.start()
        for c in copies(bi, first, 2):
            c.wait()
        return scatter(lo, first, 2, acc)

    acc = lax.fori_loop(1, rounds, later_round, acc)
    out = x_ref[0] + gt_ref[0] * acc
    if final_norm:
        ms = jnp.mean(out * out, axis=-1, keepdims=True)
        out = out * lax.rsqrt(ms + EPS) * gf_ref[...]
    o_ref[0] = out


def _combine(starts, slot, x, gate, g_final, y, win, final_norm):
    b, s, d = x.shape
    e = N_EXPERTS
    rows = y.shape[2]
    return pl.pallas_call(
        functools.partial(_combine_body, win=win, rows=rows, final_norm=final_norm),
        grid_spec=pltpu.PrefetchScalarGridSpec(
            num_scalar_prefetch=1, grid=(b, s // TOK),
            in_specs=[pl.BlockSpec((1, e, TOK), lambda bi, j, st: (bi, 0, j)),
                      pl.BlockSpec((1, TOK, d), lambda bi, j, st: (bi, j, 0)),
                      pl.BlockSpec((1, 1, d), lambda bi, j, st: (bi, 0, 0)),
                      pl.BlockSpec((1, d), lambda bi, j, st: (0, 0)),
                      pl.BlockSpec(memory_space=pl.ANY)],
            out_specs=pl.BlockSpec((1, TOK, d), lambda bi, j, st: (bi, j, 0)),
            scratch_shapes=[pltpu.VMEM((3, e * (win + BF16_ROWS), d), BF16), pltpu.SemaphoreType.DMA((3, e))]),
        out_shape=jax.ShapeDtypeStruct((b, s, d), F32),
        compiler_params=_params("arbitrary", "arbitrary"),
        name="combine",
    )(starts, slot, x, gate, g_final, y)


def _route(h, logits_t):
    n = h.shape[1]
    nblk = n // TOK
    cap = CAPACITY_FACTOR * n // N_EXPERTS
    win = min(SLOT_WINDOW, cap)
    rows = cap + SUBLANES * nblk
    rows = -(-rows // BF16_ROWS) * BF16_ROWS
    tm = min(FFN_TILE, rows)
    rows = -(-rows // tm) * tm
    slot, aff, starts = _router(logits_t, cap)
    xs = _dispatch(starts, h, slot, aff, rows + win, win)
    return dict(slot=slot, starts=starts, xs=xs, win=win, rows=rows, tm=tm, nblk=nblk)


def _rope_tables(n_tokens, rot_dim):
    rows = n_tokens // GRID_W
    row = jnp.repeat(jnp.arange(rows), GRID_W)
    col = jnp.tile(jnp.arange(GRID_W), rows)
    quarter = rot_dim // 4
    inv_freq = ROPE_THETA ** (-jnp.arange(quarter, dtype=F32) / quarter)
    ang = jnp.concatenate([inv_freq[:, None] * row[None, :], inv_freq[:, None] * col[None, :]], axis=0)
    return jnp.cos(ang), jnp.sin(ang)


def _identity_tables(n_tokens, rot_dim):
    return jnp.ones((rot_dim // 2, n_tokens), F32), jnp.zeros((rot_dim // 2, n_tokens), F32)


def kernel(x, c, ctx, c_ctx, w_mod, b_mod, g_attn, g_ffn, w_in, a_sink, b_q_norm, b_w_uq, b_kv_norm, b_w_ukv, c_lambda, c_subln, d_q_norm, d_k_norm, w_out, w_router, w_gate, w_up, w_down, g_final):
    b, s, d = x.shape
    depth = w_mod.shape[0]
    n_ctx = ctx.shape[1]
    rope64, rope32 = _rope_tables(s, HEAD_DIM), _rope_tables(s, C_QK)
    id64, id32 = _identity_tables(n_ctx, HEAD_DIM), _identity_tables(n_ctx, C_QK)

    cond = jnp.zeros((SUBLANES, d), F32).at[:b].set(c).at[b].set(c_ctx)
    mod = _adaln(cond, w_mod, b_mod)

    col = lambda v: v.reshape(-1, 1)
    xl, xc = x, ctx
    for l in range(depth):
        last = l == depth - 1
        lam_init = 0.8 - 0.6 * math.exp(-0.3 * l)
        m6 = mod[l].reshape(SUBLANES, 6, d)
        lat = [m6[:b, k][:, None, :] for k in range(6)]
        cx = [jnp.broadcast_to(m6[b, k][None, None, :], (b, 1, d)) for k in range(6)]
        w_in_t = w_in[l].T.astype(BF16)
        wuq_t = b_w_uq[l].T.astype(BF16)
        wukv_t = b_w_ukv[l].T.astype(BF16)
        w_out_b = w_out[l].astype(BF16)
        w_router_t = w_router[l].T
        w_router_hi = w_router_t.astype(BF16)
        w_router_t = jnp.concatenate([w_router_hi, (w_router_t - w_router_hi.astype(F32)).astype(BF16)], axis=0)
        g_a, g_f = g_attn[l][None, :], g_ffn[l][None, :]
        small = (col(b_q_norm[l]), wuq_t, col(b_kv_norm[l]), wukv_t, col(d_q_norm[l]), col(d_k_norm[l]))

        pl_ = _proj(xl, g_a, lat[0], lat[1], w_in_t, rope64, rope32, *small)
        pc_ = _proj(xc, g_a, cx[0], cx[1], w_in_t, id64, id32, *small)
        qa, ka, va, qb, kb, vb, qc, kc, vc, qd, kd, vd = pl_
        qa_c, ka_c, va_c, qb_c, kb_c, vb_c, qc_c, kc_c, vc_c, qd_c, kd_c, vd_c = pc_

        zero = lambda h: 0
        oa = _window_attn(qa, ka_c, va_c, ka, va, a_sink[l])
        ob = _dense_attn(qb, kb_c, vb_c, kb, vb, lambda h: h, lambda h: h)
        oc = _dense_attn(qc, kc_c, vc_c, kc, vc, lambda h: h // 4, lambda h: h // 2, out_dtype=F32)
        od = _dense_attn(qd, kd_c, vd_c, kd, vd, zero, lambda h: h // 2)
        outproj = functools.partial(_outproj, lam_p=c_lambda[l], subln=col(c_subln[l]), w_out=w_out_b,
                                    g_ffn=g_f, w_router_t=w_router_t, lam_init=lam_init)
        xl, hl, lg = outproj(oa, ob, oc, od, x=xl, gate=lat[2], shift=lat[3], scale=lat[4])
        routed = [_route(hl, lg)]
        if not last:
            oa = _dense_attn(qa_c, ka_c, va_c, None, None, zero, lambda h: h // 2, sink=a_sink[l])
            ob = _dense_attn(qb_c, kb_c, vb_c, None, None, lambda h: h, lambda h: h)
            oc = _dense_attn(qc_c, kc_c, vc_c, None, None, lambda h: h // 4, lambda h: h // 2, out_dtype=F32)
            od = _dense_attn(qd_c, kd_c, vd_c, None, None, zero, lambda h: h // 2)
            xc, hc, lgc = outproj(oa, ob, oc, od, x=xc, gate=cx[2], shift=cx[3], scale=cx[4])
            routed.append(_route(hc, lgc))
        ys = _ffn(routed, w_gate, w_up, w_down, l)
        combine = lambda r, y, x_, gate, norm: _combine(r["starts"], r["slot"], x_, gate, g_final[None, :], y,
                                                        r["win"], norm)
        xl = combine(routed[0], ys[0], xl, lat[5], last)
        if not last:
            xc = combine(routed[1], ys[1], xc, cx[5], False)
    return xl
```

```python
import functools
import math

import jax
import jax.numpy as jnp
from jax import lax
from jax.experimental import pallas as pl
from jax.experimental.pallas import tpu as pltpu

F32, BF16, I32 = jnp.float32, jnp.bfloat16, jnp.int32
HIGHEST = lax.Precision.HIGHEST

SUBLANES = 8
LANES = 128
BF16_ROWS = 16

GRID_W = 64
HEAD_DIM = 64
WINDOW = 128
EPS = 1e-6
ROPE_THETA = 10000.0
NEG_INF = -1e30
A_HEADS, A_KV = 4, 2
B_HEADS, B_Q_RANK, B_KV_RANK, B_NOPE, B_ROPE, B_V = 4, 192, 128, 64, 32, 64
C_HEADS, C_QK, C_V = 4, 32, 64
D_HEADS, D_KV = 4, 2
N_EXPERTS = 16
CAPACITY_FACTOR = 2

A_COLS = A_HEADS * HEAD_DIM + 2 * A_KV * HEAD_DIM
B_COLS = B_Q_RANK + B_KV_RANK + B_ROPE
C_COLS = 4 * C_HEADS * C_QK + C_HEADS * C_V
D_COLS = D_HEADS * HEAD_DIM + 2 * D_KV * HEAD_DIM
A_OFF, B_OFF, C_OFF, D_OFF = 0, A_COLS, A_COLS + B_COLS, A_COLS + B_COLS + C_COLS

BISECT_STEPS = 152
TOK = 256
QPAD = 128
VROWS = HEAD_DIM + 16
TQ = 2048
COL_BLOCK = 256
ATTN_UNROLL = 8
KEY_CHUNK = 256
WINDOW_CHUNKS = (0, -1, 1)
FFN_TILE = 640
FFN_STEP = 128
WEIGHT_PIECES = 2
SLOT_WINDOW = 48
LOG2E = math.log2(math.e)

NT_DIMS = (((1,), (1,)), ((), ()))
TN_DIMS = (((0,), (0,)), ((), ()))


def _params(*sem):
    return pltpu.CompilerParams(dimension_semantics=sem, vmem_limit_bytes=56 * 1024 * 1024)


def _adaln_body(c_ref, w_ref, b_ref, o_ref):
    c = c_ref[...]
    s = c * jax.nn.sigmoid(c)
    o_ref[0] = jnp.dot(s, w_ref[0], precision=HIGHEST, preferred_element_type=F32) + b_ref[0]


def _adaln(cond, w_mod, b_mod):
    depth, d, n = w_mod.shape
    tn = n // 4
    return pl.pallas_call(
        _adaln_body,
        grid=(depth, n // tn),
        in_specs=[pl.BlockSpec((SUBLANES, d), lambda l, j: (0, 0)),
                  pl.BlockSpec((1, d, tn), lambda l, j: (l, 0, j)),
                  pl.BlockSpec((1, 1, tn), lambda l, j: (l, 0, j))],
        out_specs=pl.BlockSpec((1, SUBLANES, tn), lambda l, j: (l, 0, j)),
        out_shape=jax.ShapeDtypeStruct((depth, SUBLANES, n), F32),
        compiler_params=_params("arbitrary", "arbitrary"),
        name="adaln",
    )(cond, w_mod, b_mod.reshape(depth, 1, n))


def _rope_t(xt, cos, sin):
    half = xt.shape[0] // 2
    x1, x2 = xt[:half], xt[half:]
    return jnp.concatenate([x1 * cos - x2 * sin, x1 * sin + x2 * cos], axis=0)


def _rms_t(xt, g):
    ms = jnp.mean(xt * xt, axis=0, keepdims=True)
    return xt * lax.rsqrt(ms + EPS) * g


def _modulate(x, g, shift, scale):
    ms = jnp.mean(x * x, axis=-1, keepdims=True)
    return x * lax.rsqrt(ms + EPS) * g * (1.0 + scale) + shift


def _proj_body(x_ref, g_ref, sh_ref, sc_ref, w_ref, c64_ref, s64_ref, c32_ref, s32_ref,
               bqn_ref, wuq_ref, bkvn_ref, wukv_ref, dqn_ref, dkn_ref,
               qa_ref, ka_ref, va_ref, qb_ref, kb_ref, vb_ref,
               qc_ref, kc_ref, vc_ref, qd_ref, kd_ref, vd_ref):
    tm = x_ref.shape[1]
    h = _modulate(x_ref[0], g_ref[...], sh_ref[0], sc_ref[0]).astype(BF16)
    pt = lax.dot_general(w_ref[...], h, NT_DIMS, preferred_element_type=F32)
    c64, s64, c32, s32 = c64_ref[...], s64_ref[...], c32_ref[...], s32_ref[...]
    z64 = jnp.zeros((64, tm), F32)
    z32 = jnp.zeros((32, tm), F32)
    scale64 = HEAD_DIM ** -0.5 * LOG2E
    scale_b = (B_NOPE + B_ROPE) ** -0.5 * LOG2E
    scale_c = C_QK ** -0.5 * LOG2E
    def put_values(v_ref, hh, vt):
        chunk = v_ref.shape[-1]
        for c in range(tm // chunk):
            v_ref[0, c, VROWS * hh: VROWS * hh + HEAD_DIM] = vt[:, chunk * c: chunk * c + chunk].astype(BF16)
            v_ref[0, c, VROWS * hh + HEAD_DIM: VROWS * hh + VROWS] = jnp.ones((VROWS - HEAD_DIM, chunk), BF16)

    def gqa(off, q_ref, k_ref, v_ref, qn, kn):
        for hh in range(4):
            q = pt[off + 64 * hh: off + 64 * hh + 64]
            if qn is not None:
                q = _rms_t(q, qn)
            q = _rope_t(q, c64, s64) * scale64
            q_ref[0, hh] = (jnp.concatenate([q, z64], axis=0) if hh // 2 == 0
                            else jnp.concatenate([z64, q], axis=0)).astype(BF16)
        ks = []
        for j in range(2):
            k = pt[off + 256 + 64 * j: off + 256 + 64 * j + 64]
            if kn is not None:
                k = _rms_t(k, kn)
            ks.append(_rope_t(k, c64, s64))
        k_ref[0] = jnp.concatenate(ks, axis=0).T.astype(BF16)
        for j in range(2):
            put_values(v_ref, j, pt[off + 384 + 64 * j: off + 384 + 64 * j + 64])

    gqa(A_OFF, qa_ref, ka_ref, va_ref, None, None)
    gqa(D_OFF, qd_ref, kd_ref, vd_ref, dqn_ref[...], dkn_ref[...])

    cq = _rms_t(pt[B_OFF: B_OFF + B_Q_RANK], bqn_ref[...]).astype(BF16)
    ckv = _rms_t(pt[B_OFF + B_Q_RANK: B_OFF + B_Q_RANK + B_KV_RANK], bkvn_ref[...]).astype(BF16)
    kr = _rope_t(pt[B_OFF + B_Q_RANK + B_KV_RANK: B_OFF + B_COLS], c32, s32)
    qt = jnp.dot(wuq_ref[...], cq, preferred_element_type=F32)
    kvt = jnp.dot(wukv_ref[...], ckv, preferred_element_type=F32)
    kparts = []
    dq = B_NOPE + B_ROPE
    for hh in range(B_HEADS):
        qn_ = qt[dq * hh: dq * hh + B_NOPE]
        qr = _rope_t(qt[dq * hh + B_NOPE: dq * hh + dq], c32, s32)
        qb_ref[0, hh] = (jnp.concatenate([qn_, qr, z32], axis=0) * scale_b).astype(BF16)
        kparts.append(jnp.concatenate([kvt[128 * hh: 128 * hh + B_NOPE], kr, z32], axis=0))
        put_values(vb_ref, hh, kvt[128 * hh + B_NOPE: 128 * hh + 128])
    kb_ref[0] = jnp.concatenate(kparts, axis=0).T.astype(BF16)

    for j in range(2 * C_HEADS):
        q = _rope_t(pt[C_OFF + 32 * j: C_OFF + 32 * j + 32], c32, s32) * scale_c
        pieces = [z32, z32, z32, z32]
        pieces[j % 4] = q
        qc_ref[0, j] = jnp.concatenate(pieces, axis=0).astype(BF16)
    kc = [_rope_t(pt[C_OFF + 256 + 32 * j: C_OFF + 256 + 32 * j + 32], c32, s32) for j in range(2 * C_HEADS)]
    kc_ref[0] = jnp.concatenate(kc, axis=0).T.astype(BF16)
    for hh in range(C_HEADS):
        put_values(vc_ref, hh, pt[C_OFF + 512 + 64 * hh: C_OFF + 512 + 64 * hh + 64])


def _proj(x, g, shift, scale, w_in_t, rope64, rope32, bqn, wuq_t, bkvn, wukv_t, dqn, dkn):
    b, s, d = x.shape
    tm = min(2 * TOK, s)
    n = s // tm
    full = lambda a: pl.BlockSpec(a.shape, lambda bi, i: (0,) * a.ndim)
    q_spec = lambda nh: pl.BlockSpec((1, nh, QPAD, tm), lambda bi, i: (bi, 0, 0, i))
    k_spec = lambda w: pl.BlockSpec((1, tm, w), lambda bi, i: (bi, i, 0))
    kch = min(KEY_CHUNK, tm)
    v_spec = lambda r, c=kch: pl.BlockSpec((1, tm // c, r, c), lambda bi, i: (bi, i, 0, 0))
    q_shape = lambda nh: jax.ShapeDtypeStruct((b, nh, QPAD, s), BF16)
    k_shape = lambda w: jax.ShapeDtypeStruct((b, s, w), BF16)
    v_shape = lambda r, c=kch: jax.ShapeDtypeStruct((b, s // c, r, c), BF16)
    tab = lambda t: pl.BlockSpec((t.shape[0], tm), lambda bi, i: (0, i))
    c64, s64 = rope64
    c32, s32 = rope32
    return pl.pallas_call(
        _proj_body,
        grid=(b, n),
        in_specs=[pl.BlockSpec((1, tm, d), lambda bi, i: (bi, i, 0)), full(g),
                  pl.BlockSpec((1, 1, d), lambda bi, i: (bi, 0, 0)),
                  pl.BlockSpec((1, 1, d), lambda bi, i: (bi, 0, 0)),
                  full(w_in_t), tab(c64), tab(s64), tab(c32), tab(s32),
                  full(bqn), full(wuq_t), full(bkvn), full(wukv_t), full(dqn), full(dkn)],
        out_specs=[q_spec(4), k_spec(128), v_spec(2 * VROWS, TOK),
                   q_spec(4), k_spec(512), v_spec(4 * VROWS),
                   q_spec(8), k_spec(256), v_spec(4 * VROWS),
                   q_spec(4), k_spec(128), v_spec(2 * VROWS)],
        out_shape=[q_shape(4), k_shape(128), v_shape(2 * VROWS, TOK),
                   q_shape(4), k_shape(512), v_shape(4 * VROWS),
                   q_shape(8), k_shape(256), v_shape(4 * VROWS),
                   q_shape(4), k_shape(128), v_shape(2 * VROWS)],
        compiler_params=_params("arbitrary", "arbitrary"),
        name="proj",
    )(x, g, shift, scale, w_in_t, c64, s64, c32, s32, bqn, wuq_t, bkvn, wukv_t, dqn, dkn)


def _dense_body(*refs, has_lat, has_sink, n_lat):
    refs = list(refs)
    sink_ref = refs.pop(0) if has_sink else None
    q_ref, kc_ref, vc_ref = refs[:3]
    kl_ref, vl_ref = (refs[3], refs[4]) if has_lat else (None, None)
    o_ref, s_even, s_odd, acc_ref = refs[-4:]
    tq = q_ref.shape[-1]
    qt = q_ref[0, 0]

    def produce(kblk, s_ref):
        s = jnp.dot(kblk, qt, preferred_element_type=F32)
        s_ref[0:kblk.shape[0], :] = s
        return jnp.max(s, axis=0, keepdims=True)

    def consume(s_ref, mx, vblk, m):
        m_new = mx if m is None else jnp.maximum(m, mx)
        p = jnp.exp2(s_ref[0:vblk.shape[1], :] - m_new).astype(BF16)
        pv = jnp.dot(vblk, p, preferred_element_type=F32)
        acc_ref[...] = pv if m is None else jnp.exp2(m - m_new) * acc_ref[...] + pv
        return m_new

    kch = s_even.shape[0]

    def lat_keys(i):
        return kl_ref[0, pl.ds(pl.multiple_of(i * kch, kch), kch), :]

    mx_c = produce(kc_ref[0], s_odd)
    if not has_lat:
        m = consume(s_odd, mx_c, vc_ref[0, 0], None)
    else:
        ncol = tq // COL_BLOCK

        def overlap(k_next, s_next, s_cur, mx_cur, v_cur, m):
            m_out, mx_out = [], []
            for c in range(ncol):
                cols = slice(COL_BLOCK * c, COL_BLOCK * c + COL_BLOCK)
                s = jnp.dot(k_next, qt[:, cols], preferred_element_type=F32)
                s_next[:, cols] = s
                mx_out.append(jnp.max(s, axis=0, keepdims=True))
                m_new = mx_cur[:, cols] if m is None else jnp.maximum(m[:, cols], mx_cur[:, cols])
                p = jnp.exp2(s_cur[:, cols] - m_new).astype(BF16)
                pv = jnp.dot(v_cur, p, preferred_element_type=F32)
                acc_ref[:, cols] = pv if m is None else jnp.exp2(m[:, cols] - m_new) * acc_ref[:, cols] + pv
                m_out.append(m_new)
            return jnp.concatenate(m_out, axis=1), jnp.concatenate(mx_out, axis=1)

        m, mx_e = overlap(lat_keys(0), s_even, s_odd, mx_c, vc_ref[0, 0], None)

        def body(j, carry):
            m, mx_e = carry
            m, mx_o = overlap(lat_keys(2 * j + 1), s_odd, s_even, mx_e, vl_ref[0, 2 * j], m)
            m, mx_e = overlap(lat_keys(jnp.minimum(2 * j + 2, n_lat - 1)), s_even, s_odd, mx_o, vl_ref[0, 2 * j + 1], m)
            return m, mx_e
        m, _ = lax.fori_loop(0, n_lat // 2, body, (m, mx_e), unroll=ATTN_UNROLL)
    acc = acc_ref[...]
    num, den = acc[:HEAD_DIM], acc[HEAD_DIM:HEAD_DIM + 1]
    if has_sink:
        sk = sink_ref[pl.program_id(1)] * LOG2E
        m2 = jnp.maximum(m, sk)
        a = jnp.exp2(m - m2)
        den = den * a + jnp.exp2(sk - m2)
        num = num * a
    o_ref[0, 0] = (num * (1.0 / den)).astype(o_ref.dtype)


def _dense_attn(q, k_ctx, v_ctx, k_lat, v_lat, kgroup, vhead, sink=None, out_dtype=BF16):
    b, nh, _, s = q.shape
    tq = min(TQ, s)
    has_lat = k_lat is not None
    has_sink = sink is not None
    sc = k_ctx.shape[1]
    in_specs = [pl.BlockSpec((1, 1, QPAD, tq), lambda bi, h, i: (bi, h, 0, i)),
                pl.BlockSpec((1, sc, LANES), lambda bi, h, i: (bi, 0, kgroup(h))),
                pl.BlockSpec((1, 1, VROWS, sc), lambda bi, h, i: (bi, 0, vhead(h), 0))]
    args = [q, k_ctx, v_ctx]
    n_lat, kch = 0, sc
    if has_lat:
        sl = k_lat.shape[1]
        kch = v_lat.shape[-1]
        n_lat = sl // kch
        assert n_lat % 2 == 0 and kch == sc and tq % COL_BLOCK == 0
        in_specs += [pl.BlockSpec((1, sl, LANES), lambda bi, h, i: (bi, 0, kgroup(h))),
                     pl.BlockSpec((1, n_lat, VROWS, kch), lambda bi, h, i: (bi, 0, vhead(h), 0))]
        args += [k_lat, v_lat]
    if has_sink:
        in_specs = [pl.BlockSpec(memory_space=pltpu.SMEM)] + in_specs
        args = [sink] + args
    return pl.pallas_call(
        functools.partial(_dense_body, has_lat=has_lat, has_sink=has_sink, n_lat=n_lat),
        grid=(b, nh, s // tq),
        in_specs=in_specs,
        out_specs=pl.BlockSpec((1, 1, HEAD_DIM, tq), lambda bi, h, i: (bi, h, 0, i)),
        out_shape=jax.ShapeDtypeStruct((b, nh, HEAD_DIM, s), out_dtype),
        scratch_shapes=[pltpu.VMEM((kch, tq), F32), pltpu.VMEM((kch, tq), F32), pltpu.VMEM((VROWS, tq), F32)],
        compiler_params=_params("arbitrary", "arbitrary", "arbitrary"),
        name="dense_attn",
    )(*args)


def _window_body(sink_ref, q_ref, kc_ref, vc_ref, *rest, n_chunks):
    nw = len(WINDOW_CHUNKS)
    k_refs, v_refs = rest[:nw], rest[nw:2 * nw]
    o_ref, s_even, s_odd, acc_ref = rest[2 * nw:]
    i = pl.program_id(1)
    nh, tq = q_ref.shape[1], q_ref.shape[-1]
    width = nh * tq
    qt = jnp.concatenate([q_ref[0, hh] for hh in range(nh)], axis=1)
    dist = lax.broadcasted_iota(I32, (TOK, tq), 0) - lax.broadcasted_iota(I32, (TOK, tq), 1)

    def in_window(rel):
        chunk = (tq // TOK) * i + rel
        return (jnp.abs(dist + (rel * TOK)) <= WINDOW) & (chunk >= 0) & (chunk < n_chunks)

    def produce(k_ref, ok, s_ref):
        s = jnp.dot(k_ref[0], qt, preferred_element_type=F32)
        if ok is not None:
            s = jnp.concatenate([jnp.where(ok, s[:, tq * hh: tq * hh + tq], NEG_INF) for hh in range(nh)], axis=1)
        s_ref[...] = s
        return jnp.max(s, axis=0, keepdims=True)

    def consume(s_ref, mx, v_ref, m):
        m_new = jnp.maximum(m, mx)
        p = jnp.exp2(s_ref[...] - m_new).astype(BF16)
        pv = [jnp.dot(v_ref[0, 0, VROWS * g: VROWS * g + VROWS, :], p[:, 2 * tq * g: 2 * tq * g + 2 * tq],
                      preferred_element_type=F32) for g in range(A_KV)]
        acc_ref[...] = jnp.exp2(m - m_new) * acc_ref[...] + jnp.concatenate(pv, axis=1)
        return m_new

    acc_ref[...] = jnp.zeros(acc_ref.shape, F32)
    bufs = (s_even, s_odd)
    mx = produce(kc_ref, None, s_even)
    m = jnp.full((1, width), NEG_INF, F32)
    prev_v = vc_ref
    for c, rel in enumerate(WINDOW_CHUNKS):
        mx_next = produce(k_refs[c], in_window(rel), bufs[(c + 1) % 2])
        m = consume(bufs[c % 2], mx, prev_v, m)
        mx, prev_v = mx_next, v_refs[c]
    m = consume(bufs[nw % 2], mx, prev_v, m)

    sk = jnp.concatenate([jnp.full((1, tq), sink_ref[hh] * LOG2E, F32) for hh in range(nh)], axis=1)
    m2 = jnp.maximum(m, sk)
    a = jnp.exp2(m - m2)
    acc = acc_ref[...]
    out = acc[:HEAD_DIM] * a / (acc[HEAD_DIM:HEAD_DIM + 1] * a + jnp.exp2(sk - m2))
    for hh in range(nh):
        o_ref[0, hh] = out[:, tq * hh: tq * hh + tq].astype(o_ref.dtype)


def _window_attn(q, k_ctx, v_ctx, k_lat, v_lat, sink):
    b, nh, _, s = q.shape
    tq = TOK
    assert s % tq == 0 and WINDOW <= TOK
    n_chunks = s // TOK
    sc = k_ctx.shape[1]
    nw = len(WINDOW_CHUNKS)
    chunk = lambda rel: (lambda i: jnp.clip((tq // TOK) * i + rel, 0, n_chunks - 1))
    kspec = lambda f: pl.BlockSpec((1, TOK, LANES), lambda bi, i: (bi, f(i), 0))
    vspec = lambda f: pl.BlockSpec((1, 1, 2 * VROWS, TOK), lambda bi, i: (bi, f(i), 0, 0))
    return pl.pallas_call(
        functools.partial(_window_body, n_chunks=n_chunks),
        grid=(b, s // tq),
        in_specs=[pl.BlockSpec(memory_space=pltpu.SMEM),
                  pl.BlockSpec((1, nh, QPAD, tq), lambda bi, i: (bi, 0, 0, i)),
                  pl.BlockSpec((1, sc, LANES), lambda bi, i: (bi, 0, 0)),
                  pl.BlockSpec((1, 1, 2 * VROWS, sc), lambda bi, i: (bi, 0, 0, 0))]
                 + [kspec(chunk(rel)) for rel in WINDOW_CHUNKS] + [vspec(chunk(rel)) for rel in WINDOW_CHUNKS],
        out_specs=pl.BlockSpec((1, nh, HEAD_DIM, tq), lambda bi, i: (bi, 0, 0, i)),
        out_shape=jax.ShapeDtypeStruct((b, nh, HEAD_DIM, s), BF16),
        scratch_shapes=[pltpu.VMEM((TOK, nh * tq), F32), pltpu.VMEM((TOK, nh * tq), F32),
                        pltpu.VMEM((VROWS, nh * tq), F32)],
        compiler_params=_params("arbitrary", "arbitrary"),
        name="window_attn",
    )(sink, q, k_ctx, v_ctx, *([k_lat] * nw), *([v_lat] * nw))


def _outproj_body(oa_ref, ob_ref, oc_ref, od_ref, lam_ref, subln_ref, w_ref, x_ref, gt_ref,
                  g_ref, sh_ref, sc_ref, wr_ref, xo_ref, h_ref, lg_ref, *, lam_init):
    lp = lam_ref[...]
    lam = (jnp.exp(jnp.sum(lp[0:1] * lp[1:2], axis=1, keepdims=True))
           - jnp.exp(jnp.sum(lp[2:3] * lp[3:4], axis=1, keepdims=True)) + lam_init)
    parts = [oa_ref[0, hh] for hh in range(4)] + [ob_ref[0, hh] for hh in range(4)]
    for hh in range(C_HEADS):
        o = oc_ref[0, 2 * hh] - lam * oc_ref[0, 2 * hh + 1]
        parts.append((_rms_t(o, subln_ref[...]) * (1.0 - lam_init)).astype(BF16))
    parts += [od_ref[0, hh] for hh in range(4)]
    ot = jnp.concatenate(parts, axis=0)
    out = lax.dot_general(ot, w_ref[...], TN_DIMS, preferred_element_type=F32)
    xn = x_ref[0] + gt_ref[0] * out
    xo_ref[0] = xn
    h = _modulate(xn, g_ref[...], sh_ref[0], sc_ref[0])
    h_hi = h.astype(BF16)
    h_ref[0] = h_hi
    h_lo = (h - h_hi.astype(F32)).astype(BF16)
    both = lax.dot_general(wr_ref[...], h_hi, NT_DIMS, preferred_element_type=F32)
    low = lax.dot_general(wr_ref[0:N_EXPERTS], h_lo, NT_DIMS, preferred_element_type=F32)
    lg_ref[0] = both[:N_EXPERTS] + both[N_EXPERTS:] + low


def _outproj(oa, ob, oc, od, lam_p, subln, w_out, x, gate, g_ffn, shift, scale, w_router_t, lam_init):
    b, s, d = x.shape
    tm = min(2 * TOK, s)
    ospec = lambda nh: pl.BlockSpec((1, nh, HEAD_DIM, tm), lambda bi, i: (bi, 0, 0, i))
    full = lambda a: pl.BlockSpec(a.shape, lambda bi, i: (0,) * a.ndim)
    row = pl.BlockSpec((1, 1, d), lambda bi, i: (bi, 0, 0))
    return pl.pallas_call(
        functools.partial(_outproj_body, lam_init=lam_init),
        grid=(b, s // tm),
        in_specs=[ospec(4), ospec(4), ospec(8), ospec(4), full(lam_p), full(subln), full(w_out),
                  pl.BlockSpec((1, tm, d), lambda bi, i: (bi, i, 0)), row, full(g_ffn), row, row,
                  full(w_router_t)],
        out_specs=[pl.BlockSpec((1, tm, d), lambda bi, i: (bi, i, 0)),
                   pl.BlockSpec((1, tm, d), lambda bi, i: (bi, i, 0)),
                   pl.BlockSpec((1, N_EXPERTS, tm), lambda bi, i: (bi, 0, i))],
        out_shape=[jax.ShapeDtypeStruct((b, s, d), F32), jax.ShapeDtypeStruct((b, s, d), BF16),
                   jax.ShapeDtypeStruct((b, N_EXPERTS, s), F32)],
        compiler_params=_params("arbitrary", "arbitrary"),
        name="outproj",
    )(oa, ob, oc, od, lam_p, subln, w_out, x, gate, g_ffn, shift, scale, w_router_t)


def _router_body(lg_ref, slot_ref, aff_ref, st_ref, *, cap, nblk):
    lg = lg_ref[0]
    ex = jnp.exp(lg - jnp.max(lg, axis=0, keepdims=True))
    aff = ex / jnp.sum(ex, axis=0, keepdims=True)
    aff_ref[0] = aff

    def search(_, bounds):
        lo, hi = bounds
        mid = (lo + hi) * 0.5
        enough = jnp.sum((aff >= mid).astype(I32), axis=1, keepdims=True) >= cap
        return jnp.where(enough, mid, lo), jnp.where(enough, hi, mid)

    lo, hi = lax.fori_loop(0, BISECT_STEPS, search,
                           (jnp.zeros((N_EXPERTS, 1), F32), jnp.full((N_EXPERTS, 1), 2.0, F32)))
    gt = aff >= hi
    eq = (aff >= lo) & jnp.logical_not(gt)
    need = (cap - jnp.sum(gt.astype(I32), axis=1, keepdims=True)).astype(F32)
    tri = (lax.broadcasted_iota(I32, (TOK, TOK), 0) <= lax.broadcasted_iota(I32, (TOK, TOK), 1)).astype(BF16)
    lane = lax.broadcasted_iota(I32, (N_EXPERTS, LANES), 1)
    eq_seen = jnp.zeros((N_EXPERTS, 1), F32)
    base = jnp.zeros((N_EXPERTS, 1), I32)
    starts = jnp.zeros((N_EXPERTS, LANES), I32)
    for j in range(nblk):
        cols = slice(TOK * j, TOK * j + TOK)
        eq_c = eq[:, cols]
        eq_cum = jnp.dot(eq_c.astype(BF16), tri, preferred_element_type=F32) + eq_seen
        eq_seen = eq_seen + jnp.sum(eq_c.astype(F32), axis=1, keepdims=True)
        sel = gt[:, cols] | (eq_c & (eq_cum <= need))
        cum = jnp.dot(sel.astype(BF16), tri, preferred_element_type=F32).astype(I32)
        slot_ref[0, :, cols] = jnp.where(sel, base + cum - 1, -1)
        starts = jnp.where(lane == j, base, starts)
        cnt = jnp.sum(sel.astype(I32), axis=1, keepdims=True)
        base = base + jnp.bitwise_and(cnt + (SUBLANES - 1), -SUBLANES)
    st_ref[0] = jnp.where(lane == nblk, base, starts)


def _router(logits_t, cap):
    b, e, s = logits_t.shape
    nblk = s // TOK
    return pl.pallas_call(
        functools.partial(_router_body, cap=cap, nblk=nblk),
        grid=(b,),
        in_specs=[pl.BlockSpec((1, e, s), lambda bi: (bi, 0, 0))],
        out_specs=[pl.BlockSpec((1, e, s), lambda bi: (bi, 0, 0)),
                   pl.BlockSpec((1, e, s), lambda bi: (bi, 0, 0)),
                   pl.BlockSpec((1, e, LANES), lambda bi: (bi, 0, 0))],
        out_shape=[jax.ShapeDtypeStruct((b, e, s), I32), jax.ShapeDtypeStruct((b, e, s), F32),
                   jax.ShapeDtypeStruct((b, e, LANES), I32)],
        compiler_params=_params("arbitrary"),
        name="router",
    )(logits_t)


def _onehot_t(slot_ref, first, win):
    rows = lax.broadcasted_iota(I32, (win, TOK), 0)
    return [rows == (slot_ref[0, e:e + 1, :] - first[e]) for e in range(N_EXPERTS)]


def _dispatch_body(st_ref, h_ref, slot_ref, aff_ref, xs_ref, stage, sems, count, *, win):
    bi, j = pl.program_id(0), pl.program_id(1)
    d = h_ref.shape[-1]

    @pl.when((bi == 0) & (j == 0))
    def _():
        count[0] = 0

    rows_alloc = xs_ref.shape[2]
    last_block = j == pl.num_programs(1) - 1
    st = [st_ref[bi, e, j] for e in range(N_EXPERTS)]
    ends = [jnp.where(last_block, rows_alloc, st_ref[bi, e, j + 1]) for e in range(N_EXPERTS)]
    width = ends[0] - st[0]
    for e in range(1, N_EXPERTS):
        width = jnp.maximum(width, ends[e] - st[e])
    rounds = jnp.maximum((width + win - 1) // win, 1)

    def copies(buf, first):
        return [pltpu.make_async_copy(stage.at[buf, pl.ds(e * win, win), :],
                                      xs_ref.at[bi, e, pl.ds(pl.multiple_of(first[e], SUBLANES), win), :],
                                      sems.at[e]) for e in range(N_EXPERTS)]

    def one_round(r, carry):
        k = count[0]
        buf = k % 2
        first = [jnp.minimum(st[e] + r * win, rows_alloc - win) for e in range(N_EXPERTS)]
        hot = _onehot_t(slot_ref, first, win)
        p = jnp.concatenate(hot, axis=0).astype(BF16)
        stage[buf, :, 0:d] = jnp.dot(p, h_ref[0], preferred_element_type=F32)
        gates = [jnp.sum(jnp.where(hot[e], aff_ref[0, e:e + 1, :], 0.0), axis=1, keepdims=True)
                 for e in range(N_EXPERTS)]
        stage[buf, :, d:d + LANES] = jnp.broadcast_to(jnp.concatenate(gates, axis=0), (N_EXPERTS * win, LANES))

        @pl.when(k == 0)
        def _():
            for c in copies(buf, first):
                c.start()

        @pl.when(k > 0)
        def _():
            for done, c in zip(copies(1 - buf, st), copies(buf, first)):
                done.wait()
                c.start()

        count[0] = k + 1
        return carry

    lax.fori_loop(0, rounds, one_round, 0)

    @pl.when((bi == pl.num_programs(0) - 1) & (j == pl.num_programs(1) - 1))
    def _():
        for c in copies(0, st):
            c.wait()


def _dispatch(starts, h, slot, aff, rows_alloc, win):
    b, s, d = h.shape
    nblk = s // TOK
    e = N_EXPERTS
    return pl.pallas_call(
        functools.partial(_dispatch_body, win=win),
        grid_spec=pltpu.PrefetchScalarGridSpec(
            num_scalar_prefetch=1, grid=(b, nblk),
            in_specs=[pl.BlockSpec((1, TOK, d), lambda bi, j, st: (bi, j, 0)),
                      pl.BlockSpec((1, e, TOK), lambda bi, j, st: (bi, 0, j)),
                      pl.BlockSpec((1, e, TOK), lambda bi, j, st: (bi, 0, j))],
            out_specs=pl.BlockSpec(memory_space=pl.ANY),
            scratch_shapes=[pltpu.VMEM((2, e * win, d + LANES), F32),
                            pltpu.SemaphoreType.DMA((e,)),
                            pltpu.SMEM((1,), I32)]),
        out_shape=jax.ShapeDtypeStruct((b, e, rows_alloc, d + LANES), F32),
        compiler_params=_params("arbitrary", "arbitrary"),
        name="dispatch",
    )(starts, h, slot, aff)


def _ffn_body(*refs, layer, sets, steps):
    n = len(sets)
    st_refs, xs_refs = refs[:n], refs[n:2 * n]
    wg_in, wu_in, wd_in = refs[2 * n: 2 * n + 3]
    y_refs = refs[2 * n + 3: 3 * n + 3]
    bufs, sems = refs[3 * n + 3: 3 * n + 6], refs[3 * n + 6]
    wg_ref, wu_ref, wd_ref = refs[3 * n + 7: 3 * n + 10]
    e, bi, i = pl.program_id(0), pl.program_id(1), pl.program_id(2)
    d = y_refs[0].shape[-1]

    def fetch(expert, slot, piece):
        k, p = divmod(piece, WEIGHT_PIECES)
        w, buf = (wg_in, wu_in, wd_in)[k], bufs[k]
        rows = w.shape[2] // WEIGHT_PIECES
        return pltpu.make_async_copy(w.at[layer, expert, pl.ds(p * rows, rows), :],
                                     buf.at[slot, pl.ds(p * rows, rows), :], sems.at[slot, piece])

    pieces = range(3 * WEIGHT_PIECES)
    step = bi * pl.num_programs(2) + i
    slot = e % 2

    @pl.when(step == 0)
    def _():
        @pl.when(e == 0)
        def _():
            for piece in pieces:
                fetch(0, 0, piece).start()

        for piece in pieces:
            fetch(e, slot, piece).wait()
        for out, buf in zip((wg_ref, wu_ref, wd_ref), bufs):
            out[...] = buf[slot].astype(BF16)

    for piece in pieces:
        @pl.when((step == min(piece, steps - 1)) & (e + 1 < pl.num_programs(0)))
        def _(piece=piece):
            fetch(e + 1, 1 - slot, piece).start()

    for (first, count, nblk), st_ref, xs_ref, y_ref in zip(sets, st_refs, xs_refs, y_refs):
        tm = xs_ref.shape[2]
        used = st_ref[bi, e, nblk]
        base = (i - first) * tm
        mine = (i >= first) & (i < first + count)

        def run(rows, xs_ref=xs_ref, y_ref=y_ref, tm=tm, used=used, base=base):
            valid = (base + lax.broadcasted_iota(I32, (rows, 1), 0)) < used
            xa = xs_ref[0, 0, 0:rows, :]
            x = jnp.where(valid, xa[:, 0:d], 0.0).astype(BF16)
            gate = jnp.where(valid, xa[:, d:d + 1], 0.0)
            y = jnp.zeros((rows, d), F32)
            for blk in range(wg_ref.shape[1] // COL_BLOCK):
                cs = slice(COL_BLOCK * blk, COL_BLOCK * blk + COL_BLOCK)
                a = jnp.dot(x, wg_ref[:, cs], preferred_element_type=F32)
                u = jnp.dot(x, wu_ref[:, cs], preferred_element_type=F32)
                mid = (a * jax.nn.sigmoid(a) * u).astype(BF16)
                y = y + jnp.dot(mid, wd_ref[cs, :], preferred_element_type=F32)
            y_ref[0, 0, 0:rows, :] = (y * gate).astype(BF16)
            if rows < tm:
                y_ref[0, 0, rows:tm, :] = jnp.zeros((tm - rows, d), BF16)

        piece = FFN_STEP if tm % FFN_STEP == 0 else tm
        left = used - base
        for rows in range(piece, tm + 1, piece):
            covers = (left > rows - piece) if rows == tm else (left > rows - piece) & (left <= rows)
            pl.when(mine & covers)(functools.partial(run, rows))

        @pl.when(mine & (left <= 0))
        def _(y_ref=y_ref):
            y_ref[0, 0] = jnp.zeros(y_ref.shape[2:], BF16)


def _ffn(routed, wg, wu, wd, layer):
    b, e = routed[0]["xs"].shape[:2]
    d, f = wg.shape[-2:]
    sets, first = [], 0
    for r in routed:
        count = r["rows"] // r["tm"]
        sets.append((first, count, r["nblk"]))
        first += count

    def tile_spec(r, first_tile, count, width):
        def index(ei, bi, i, *st):
            return bi, ei, jnp.clip(i - first_tile, 0, count - 1), 0
        return pl.BlockSpec((1, 1, r["tm"], width), index)

    in_specs = [tile_spec(r, s[0], s[1], d + LANES) for r, s in zip(routed, sets)]
    out_specs = [tile_spec(r, s[0], s[1], d) for r, s in zip(routed, sets)]
    out_shape = [jax.ShapeDtypeStruct((b, e, r["rows"], d), BF16) for r in routed]
    scratch = ([pltpu.VMEM((2,) + w.shape[2:], F32) for w in (wg, wu, wd)]
               + [pltpu.SemaphoreType.DMA((2, 3 * WEIGHT_PIECES))]
               + [pltpu.VMEM(w.shape[2:], BF16) for w in (wg, wu, wd)])
    return pl.pallas_call(
        functools.partial(_ffn_body, layer=layer, sets=tuple(sets), steps=b * first),
        grid_spec=pltpu.PrefetchScalarGridSpec(
            num_scalar_prefetch=len(routed), grid=(e, b, first),
            in_specs=in_specs + [pl.BlockSpec(memory_space=pl.ANY)] * 3,
            out_specs=out_specs, scratch_shapes=scratch),
        out_shape=out_shape,
        compiler_params=_params("arbitrary", "arbitrary", "arbitrary"),
        name="expert_ffn",
    )(*[r["starts"] for r in routed], *[r["xs"] for r in routed], wg, wu, wd)


def _combine_body(st_ref, slot_ref, x_ref, gt_ref, gf_ref, y_ref, o_ref, ybuf, sems, *, win, rows, final_norm):
    bi, j = pl.program_id(0), pl.program_id(1)
    nj = pl.num_programs(1)
    d = x_ref.shape[-1]
    step = bi * nj + j
    cur = step % 2
    st = [st_ref[bi, e, j] for e in range(N_EXPERTS)]
    width = st_ref[bi, 0, j + 1] - st[0]
    for e in range(1, N_EXPERTS):
        width = jnp.maximum(width, st_ref[bi, e, j + 1] - st[e])
    rounds = (width + win - 1) // win

    wrows = win + BF16_ROWS

    def window(b_, j_, r):
        lo = [st_ref[b_, e, j_] + r * win for e in range(N_EXPERTS)]
        first = [jnp.minimum(jnp.bitwise_and(lo[e], -BF16_ROWS), rows - wrows) for e in range(N_EXPERTS)]
        return lo, first

    def copies(b_, first, buf):
        return [pltpu.make_async_copy(y_ref.at[b_, e, pl.ds(pl.multiple_of(first[e], BF16_ROWS), wrows), :],
                                      ybuf.at[buf, pl.ds(e * wrows, wrows), :], sems.at[buf, e])
                for e in range(N_EXPERTS)]

    def scatter(lo, first, buf, acc):
        hot = _onehot_t(slot_ref, first, wrows)
        hot = [hot[e] & (slot_ref[0, e:e + 1, :] >= lo[e]) & (slot_ref[0, e:e + 1, :] < lo[e] + win)
               for e in range(N_EXPERTS)]
        p = jnp.concatenate(hot, axis=0).astype(BF16)
        return acc + lax.dot_general(p, ybuf[buf], TN_DIMS, preferred_element_type=F32)

    lo0, first0 = window(bi, j, 0)

    @pl.when(step == 0)
    def _():
        for c in copies(bi, first0, cur):
            c.start()

    @pl.when(step < pl.num_programs(0) * nj - 1)
    def _():
        wrap = j == nj - 1
        b_next, j_next = jnp.where(wrap, bi + 1, bi), jnp.where(wrap, 0, j + 1)
        for c in copies(b_next, window(b_next, j_next, 0)[1], 1 - cur):
            c.start()

    for c in copies(bi, first0, cur):
        c.wait()
    acc = scatter(lo0, first0, cur, jnp.zeros((TOK, d), F32))

    def later_round(r, acc):
        lo, first = window(bi, j, r)
        for c in copies(bi, first, 2):
            c.start()
        for c in copies(bi, first, 2):
            c.wait()
        return scatter(lo, first, 2, acc)

    acc = lax.fori_loop(1, rounds, later_round, acc)
    out = x_ref[0] + gt_ref[0] * acc
    if final_norm:
        ms = jnp.mean(out * out, axis=-1, keepdims=True)
        out = out * lax.rsqrt(ms + EPS) * gf_ref[...]
    o_ref[0] = out


def _combine(starts, slot, x, gate, g_final, y, win, final_norm):
    b, s, d = x.shape
    e = N_EXPERTS
    rows = y.shape[2]
    return pl.pallas_call(
        functools.partial(_combine_body, win=win, rows=rows, final_norm=final_norm),
        grid_spec=pltpu.PrefetchScalarGridSpec(
            num_scalar_prefetch=1, grid=(b, s // TOK),
            in_specs=[pl.BlockSpec((1, e, TOK), lambda bi, j, st: (bi, 0, j)),
                      pl.BlockSpec((1, TOK, d), lambda bi, j, st: (bi, j, 0)),
                      pl.BlockSpec((1, 1, d), lambda bi, j, st: (bi, 0, 0)),
                      pl.BlockSpec((1, d), lambda bi, j, st: (0, 0)),
                      pl.BlockSpec(memory_space=pl.ANY)],
            out_specs=pl.BlockSpec((1, TOK, d), lambda bi, j, st: (bi, j, 0)),
            scratch_shapes=[pltpu.VMEM((3, e * (win + BF16_ROWS), d), BF16), pltpu.SemaphoreType.DMA((3, e))]),
        out_shape=jax.ShapeDtypeStruct((b, s, d), F32),
        compiler_params=_params("arbitrary", "arbitrary"),
        name="combine",
    )(starts, slot, x, gate, g_final, y)


def _route(h, logits_t):
    n = h.shape[1]
    nblk = n // TOK
    cap = CAPACITY_FACTOR * n // N_EXPERTS
    win = min(SLOT_WINDOW, cap)
    rows = cap + SUBLANES * nblk
    rows = -(-rows // BF16_ROWS) * BF16_ROWS
    tm = min(FFN_TILE, rows)
    rows = -(-rows // tm) * tm
    slot, aff, starts = _router(logits_t, cap)
    xs = _dispatch(starts, h, slot, aff, rows + win, win)
    return dict(slot=slot, starts=starts, xs=xs, win=win, rows=rows, tm=tm, nblk=nblk)


def _rope_tables(n_tokens, rot_dim):
    rows = n_tokens // GRID_W
    row = jnp.repeat(jnp.arange(rows), GRID_W)
    col = jnp.tile(jnp.arange(GRID_W), rows)
    quarter = rot_dim // 4
    inv_freq = ROPE_THETA ** (-jnp.arange(quarter, dtype=F32) / quarter)
    ang = jnp.concatenate([inv_freq[:, None] * row[None, :], inv_freq[:, None] * col[None, :]], axis=0)
    return jnp.cos(ang), jnp.sin(ang)


def _identity_tables(n_tokens, rot_dim):
    return jnp.ones((rot_dim // 2, n_tokens), F32), jnp.zeros((rot_dim // 2, n_tokens), F32)


def kernel(x, c, ctx, c_ctx, w_mod, b_mod, g_attn, g_ffn, w_in, a_sink, b_q_norm, b_w_uq, b_kv_norm, b_w_ukv, c_lambda, c_subln, d_q_norm, d_k_norm, w_out, w_router, w_gate, w_up, w_down, g_final):
    b, s, d = x.shape
    depth = w_mod.shape[0]
    n_ctx = ctx.shape[1]
    rope64, rope32 = _rope_tables(s, HEAD_DIM), _rope_tables(s, C_QK)
    id64, id32 = _identity_tables(n_ctx, HEAD_DIM), _identity_tables(n_ctx, C_QK)

    cond = jnp.zeros((SUBLANES, d), F32).at[:b].set(c).at[b].set(c_ctx)
    mod = _adaln(cond, w_mod, b_mod)

    col = lambda v: v.reshape(-1, 1)
    xl, xc = x, ctx
    for l in range(depth):
        last = l == depth - 1
        lam_init = 0.8 - 0.6 * math.exp(-0.3 * l)
        m6 = mod[l].reshape(SUBLANES, 6, d)
        lat = [m6[:b, k][:, None, :] for k in range(6)]
        cx = [jnp.broadcast_to(m6[b, k][None, None, :], (b, 1, d)) for k in range(6)]
        w_in_t = w_in[l].T.astype(BF16)
        wuq_t = b_w_uq[l].T.astype(BF16)
        wukv_t = b_w_ukv[l].T.astype(BF16)
        w_out_b = w_out[l].astype(BF16)
        w_router_t = w_router[l].T
        w_router_hi = w_router_t.astype(BF16)
        w_router_t = jnp.concatenate([w_router_hi, (w_router_t - w_router_hi.astype(F32)).astype(BF16)], axis=0)
        g_a, g_f = g_attn[l][None, :], g_ffn[l][None, :]
        small = (col(b_q_norm[l]), wuq_t, col(b_kv_norm[l]), wukv_t, col(d_q_norm[l]), col(d_k_norm[l]))

        pl_ = _proj(xl, g_a, lat[0], lat[1], w_in_t, rope64, rope32, *small)
        pc_ = _proj(xc, g_a, cx[0], cx[1], w_in_t, id64, id32, *small)
        qa, ka, va, qb, kb, vb, qc, kc, vc, qd, kd, vd = pl_
        qa_c, ka_c, va_c, qb_c, kb_c, vb_c, qc_c, kc_c, vc_c, qd_c, kd_c, vd_c = pc_

        zero = lambda h: 0
        oa = _window_attn(qa, ka_c, va_c, ka, va, a_sink[l])
        ob = _dense_attn(qb, kb_c, vb_c, kb, vb, lambda h: h, lambda h: h)
        oc = _dense_attn(qc, kc_c, vc_c, kc, vc, lambda h: h // 4, lambda h: h // 2, out_dtype=F32)
        od = _dense_attn(qd, kd_c, vd_c, kd, vd, zero, lambda h: h // 2)
        outproj = functools.partial(_outproj, lam_p=c_lambda[l], subln=col(c_subln[l]), w_out=w_out_b,
                                    g_ffn=g_f, w_router_t=w_router_t, lam_init=lam_init)
        xl, hl, lg = outproj(oa, ob, oc, od, x=xl, gate=lat[2], shift=lat[3], scale=lat[4])
        routed = [_route(hl, lg)]
        if not last:
            oa = _dense_attn(qa_c, ka_c, va_c, None, None, zero, lambda h: h // 2, sink=a_sink[l])
            ob = _dense_attn(qb_c, kb_c, vb_c, None, None, lambda h: h, lambda h: h)
            oc = _dense_attn(qc_c, kc_c, vc_c, None, None, lambda h: h // 4, lambda h: h // 2, out_dtype=F32)
            od = _dense_attn(qd_c, kd_c, vd_c, None, None, zero, lambda h: h // 2)
            xc, hc, lgc = outproj(oa, ob, oc, od, x=xc, gate=cx[2], shift=cx[3], scale=cx[4])
            routed.append(_route(hc, lgc))
        ys = _ffn(routed, w_gate, w_up, w_down, l)
        combine = lambda r, y, x_, gate, norm: _combine(r["starts"], r["slot"], x_, gate, g_final[None, :], y,
                                                        r["win"], norm)
        xl = combine(routed[0], ys[0], xl, lat[5], last)
        if not last:
            xc = combine(routed[1], ys[1], xc, cx[5], False)
    return xl
```
